```python
import math
import jax, jax.numpy as jnp
from jax import lax
import numpy as np

D_MODEL = 2048
BATCH = 2
SEQ = 8192
DEPTH = 4

HEAD_DIM = 128
N_GROUP_HEADS = D_MODEL // HEAD_DIM
SGU_GROUPS = N_GROUP_HEADS // 4
RET_HEADS = (N_GROUP_HEADS - SGU_GROUPS) // 2
DIFF_HEADS = N_GROUP_HEADS - SGU_GROUPS - RET_HEADS
DIFF_MAP_DIM = HEAD_DIM // 2
RET_W = RET_HEADS * HEAD_DIM
DIFF_W = DIFF_HEADS * HEAD_DIM
SGU_W = SGU_GROUPS * HEAD_DIM
SGU_GROUP_DIM = SGU_W // SGU_GROUPS
IN_W = 4 * RET_W + 3 * DIFF_W + 2 * SGU_W
CHUNK = 128
FFN_HIDDEN = -(-8 * D_MODEL // (3 * 256)) * 256
EPS = 1e-6

kernel_name = "hybrid_retention_diffattn_sgu_block"


def rms_norm(x, g):
    xf = x.astype(jnp.float32)
    y = xf * lax.rsqrt(jnp.mean(xf * xf, axis=-1, keepdims=True) + EPS)
    return (y * g.astype(jnp.float32)).astype(x.dtype)


def retention(q, k, v, gate, gn_gain):
    B, T, _ = q.shape
    N = T // CHUNK
    H, d = RET_HEADS, HEAD_DIM
    log_gamma = jnp.log1p(-(2.0 ** (-5.0 - jnp.arange(H, dtype=jnp.float32))))

    def chunks(a):
        return a.astype(jnp.float32).reshape(B, N, CHUNK, H, d).transpose(1, 0, 3, 2, 4)

    qc = chunks(q)
    kc = chunks(k) * (d ** -0.5)
    vc = chunks(v)
    pos = jnp.arange(CHUNK, dtype=jnp.float32)
    rel = pos[:, None] - pos[None, :]
    decay = jnp.where(rel >= 0, jnp.exp(log_gamma[:, None, None] * jnp.maximum(rel, 0.0)), 0.0)
    scores = jnp.einsum('nbhcd,nbhed->nbhce', qc, kc) * decay
    inner = jnp.einsum('nbhce,nbhed->nbhcd', scores, vc)
    zeta = jnp.exp(log_gamma[:, None] * (CHUNK - 1.0 - pos))
    xi = jnp.exp(log_gamma[:, None] * (pos + 1.0))
    kv = jnp.einsum('nbhcd,nbhce->nbhde', kc * zeta[:, :, None], vc)
    chunk_decay = jnp.exp(log_gamma * CHUNK)[:, None, None]

    def step(R, kv_i):
        return kv_i + chunk_decay * R, R

    _, R_prev = lax.scan(step, jnp.zeros((B, H, d, d), jnp.float32), kv)
    cross = jnp.einsum('nbhcd,nbhde->nbhce', qc, R_prev) * xi[:, :, None]
    y = (inner + cross).transpose(1, 0, 3, 2, 4).reshape(B, T, H, d)
    mu = jnp.mean(y, axis=-1, keepdims=True)
    var = jnp.mean(jnp.square(y - mu), axis=-1, keepdims=True)
    y = (y - mu) * lax.rsqrt(var + EPS) * gn_gain.astype(jnp.float32)
    out = jax.nn.silu(gate.astype(jnp.float32)).reshape(B, T, H, d) * y
    return out.reshape(B, T, RET_W).astype(q.dtype)


def diff_attention(q, k, v, lam_q1, lam_k1, lam_q2, lam_k2, subln_gain, lambda_init):
    B, T, _ = q.shape
    H, dm, dv = DIFF_HEADS, DIFF_MAP_DIM, HEAD_DIM
    NB = T // CHUNK
    qf = q.astype(jnp.float32).reshape(B, T, H, 2, dm) * (dm ** -0.5)
    kt = k.astype(jnp.float32).reshape(B, T, H, 2, dm).transpose(0, 2, 3, 1, 4)
    vt = v.astype(jnp.float32).reshape(B, T, H, dv).transpose(0, 2, 1, 3)
    qb = qf.reshape(B, NB, CHUNK, H, 2, dm).transpose(1, 0, 3, 4, 2, 5)
    lam = (jnp.exp(jnp.sum(lam_q1.astype(jnp.float32) * lam_k1.astype(jnp.float32)))
           - jnp.exp(jnp.sum(lam_q2.astype(jnp.float32) * lam_k2.astype(jnp.float32)))
           + lambda_init)
    slopes = 2.0 ** (-8.0 * jnp.arange(1, H + 1, dtype=jnp.float32) / H)
    key_pos = jnp.arange(T)

    def block(args):
        q_blk, start = args
        qpos = start + jnp.arange(CHUNK)
        dist = (qpos[:, None] - key_pos[None, :]).astype(jnp.float32)
        s = jnp.einsum('bhmcd,bhmtd->bhmct', q_blk, kt) - slopes[None, :, None, None, None] * dist
        s = jnp.where(dist >= 0, s, -jnp.inf)
        p = jax.nn.softmax(s, axis=-1)
        a = p[:, :, 0] - lam * p[:, :, 1]
        return jnp.einsum('bhct,bhte->bhce', a, vt)

    starts = jnp.arange(NB) * CHUNK
    o = lax.map(block, (qb, starts))
    o = o.transpose(1, 0, 3, 2, 4).reshape(B, T, H, dv)
    o = o * lax.rsqrt(jnp.mean(o * o, axis=-1, keepdims=True) + EPS) * subln_gain.astype(jnp.float32)
    o = o * (1.0 - lambda_init)
    return o.reshape(B, T, DIFF_W).astype(q.dtype)


def spatial_gating(u, v, ln_g, ln_b, w_s, b_s):
    B, T, _ = u.shape
    N = T // CHUNK
    G, dg = SGU_GROUPS, SGU_GROUP_DIM
    uf = jax.nn.gelu(u.astype(jnp.float32)).reshape(B, N, CHUNK, G, dg)
    vf = jax.nn.gelu(v.astype(jnp.float32)).reshape(B, N, CHUNK, G, dg)
    mu = jnp.mean(vf, axis=-1, keepdims=True)
    var = jnp.mean(jnp.square(vf - mu), axis=-1, keepdims=True)
    vf = (vf - mu) * lax.rsqrt(var + EPS) * ln_g.astype(jnp.float32) + ln_b.astype(jnp.float32)
    mask = jnp.tril(jnp.ones((CHUNK, CHUNK), jnp.float32))
    w = w_s.astype(jnp.float32) * mask
    mixed = jnp.einsum('gts,bnsgd->bntgd', w, vf) + b_s.astype(jnp.float32).T[None, None, :, :, None]
    return (uf * mixed).reshape(B, T, SGU_W).astype(u.dtype)


def setup_inputs(seed: int = 0) -> dict:
    key = jax.random.key(seed)
    ks = jax.random.split(key, 24)
    f32 = jnp.float32
    L, D = DEPTH, D_MODEL

    def nrm(k, shape, scale):
        return jax.random.normal(k, shape, f32) * scale

    def gain(k, shape):
        return 1.0 + 0.05 * jax.random.normal(k, shape, f32)

    return {
        "x": jax.random.normal(ks[0], (BATCH, SEQ, D), f32),
        "pre_mix_g": gain(ks[1], (L, D)),
        "w_in": nrm(ks[2], (L, D, IN_W), D ** -0.5),
        "ret_gn_g": gain(ks[3], (L, RET_HEADS, HEAD_DIM)),
        "diff_lam_q1": nrm(ks[4], (L, DIFF_MAP_DIM), 0.1),
        "diff_lam_k1": nrm(ks[5], (L, DIFF_MAP_DIM), 0.1),
        "diff_lam_q2": nrm(ks[6], (L, DIFF_MAP_DIM), 0.1),
        "diff_lam_k2": nrm(ks[7], (L, DIFF_MAP_DIM), 0.1),
        "diff_subln_g": gain(ks[8], (L, DIFF_HEADS, HEAD_DIM)),
        "sgu_ln_g": gain(ks[9], (L, SGU_GROUPS, SGU_GROUP_DIM)),
        "sgu_ln_b": nrm(ks[10], (L, SGU_GROUPS, SGU_GROUP_DIM), 0.02),
        "sgu_w": nrm(ks[11], (L, SGU_GROUPS, CHUNK, CHUNK), CHUNK ** -0.5),
        "sgu_b": 1.0 + nrm(ks[12], (L, SGU_GROUPS, CHUNK), 0.1),
        "w_out": nrm(ks[13], (L, D, D), D ** -0.5),
        "post_mix_g": gain(ks[14], (L, D)),
        "pre_ffn_g": gain(ks[15], (L, D)),
        "w_gate": nrm(ks[16], (L, D, FFN_HIDDEN), D ** -0.5),
        "w_up": nrm(ks[17], (L, D, FFN_HIDDEN), D ** -0.5),
        "w_down": nrm(ks[18], (L, FFN_HIDDEN, D), FFN_HIDDEN ** -0.5),
        "post_ffn_g": gain(ks[19], (L, D)),
    }


def reference(x, pre_mix_g, w_in, ret_gn_g, diff_lam_q1, diff_lam_k1, diff_lam_q2, diff_lam_k2,
              diff_subln_g, sgu_ln_g, sgu_ln_b, sgu_w, sgu_b, w_out, post_mix_g, pre_ffn_g,
              w_gate, w_up, w_down, post_ffn_g):
    split_points = [RET_W, 2 * RET_W, 3 * RET_W, 4 * RET_W,
                    4 * RET_W + DIFF_W, 4 * RET_W + 2 * DIFF_W, 4 * RET_W + 3 * DIFF_W,
                    4 * RET_W + 3 * DIFF_W + SGU_W]
    for l in range(DEPTH):
        lambda_init = 0.8 - 0.6 * math.exp(-0.3 * l)
        h = rms_norm(x, pre_mix_g[l])
        proj = h @ w_in[l]
        rq, rk, rv, rg, dq, dk, dv, su, sv = jnp.split(proj, split_points, axis=-1)
        y_ret = retention(rq, rk, rv, rg, ret_gn_g[l])
        y_diff = diff_attention(dq, dk, dv, diff_lam_q1[l], diff_lam_k1[l], diff_lam_q2[l],
                                diff_lam_k2[l], diff_subln_g[l], lambda_init)
        y_sgu = spatial_gating(su, sv, sgu_ln_g[l], sgu_ln_b[l], sgu_w[l], sgu_b[l])
        mix = jnp.concatenate([y_ret, y_diff, y_sgu], axis=-1) @ w_out[l]
        x = x + rms_norm(mix, post_mix_g[l])
        h = rms_norm(x, pre_ffn_g[l])
        f = (jax.nn.silu(h @ w_gate[l]) * (h @ w_up[l])) @ w_down[l]
        x = x + rms_norm(f, post_ffn_g[l])
    return x
```

```python
import functools
import math

import numpy as np
import jax
import jax.numpy as jnp
from jax import lax
from jax.experimental import pallas as pl
from jax.experimental.pallas import tpu as pltpu

D_MODEL = 2048
HEAD_DIM = 128
RET_HEADS = 6
DIFF_HEADS = 6
SGU_GROUPS = 4
DIFF_MAP_DIM = 64
RET_W = RET_HEADS * HEAD_DIM
DIFF_W = DIFF_HEADS * HEAD_DIM
SGU_W = SGU_GROUPS * HEAD_DIM
IN_W = 4 * RET_W + 3 * DIFF_W + 2 * SGU_W
CHUNK = 128
EPS = 1e-6
LOG2E = math.log2(math.e)
NEG_BIG = -1e30

COL_RQ, COL_RK, COL_RV, COL_RG = 0, 6, 12, 18
COL_DQ, COL_DK, COL_DV = 24, 30, 36
COL_SU, COL_SV = 42, 46

V7X_VMEM_BYTES = 64 * 1024 * 1024
VMEM_LIMIT = 56 * 1024 * 1024

BF16 = jnp.bfloat16
F32 = jnp.float32


def _dot(a, b):
    return jnp.dot(a, b, preferred_element_type=F32)


def _dot_nt(a, b):
    return lax.dot_general(a, b, (((1,), (1,)), ((), ())), preferred_element_type=F32)


def _dot_tn(a, b):
    return lax.dot_general(a, b, (((0,), (0,)), ((), ())), preferred_element_type=F32)


def _rms(x, g):
    return x * lax.rsqrt(jnp.mean(x * x, axis=-1, keepdims=True) + EPS) * g


def _params(*sem):
    return pltpu.CompilerParams(dimension_semantics=sem, vmem_limit_bytes=VMEM_LIMIT)


def _rmsnorm_kernel(x_ref, g_ref, o_ref):
    o_ref[...] = _rms(x_ref[...], g_ref[...]).astype(o_ref.dtype)


def _rmsnorm(x, g, tm=512):
    m, d = x.shape
    return pl.pallas_call(
        _rmsnorm_kernel,
        grid=(m // tm,),
        in_specs=[pl.BlockSpec((tm, d), lambda i: (i, 0)), pl.BlockSpec((1, d), lambda i: (0, 0))],
        out_specs=pl.BlockSpec((tm, d), lambda i: (i, 0)),
        out_shape=jax.ShapeDtypeStruct((m, d), BF16),
        compiler_params=_params("parallel"),
        name="rmsnorm",
    )(x, g.reshape(1, d))


def _matmul_kernel(h_ref, w_ref, o_ref):
    o_ref[...] = _dot(h_ref[...], w_ref[...]).astype(o_ref.dtype)


def _inproj(h, w, tm=1024, tn=1280):
    m, k = h.shape
    n = w.shape[1]
    return pl.pallas_call(
        _matmul_kernel,
        grid=(m // tm, n // tn),
        in_specs=[pl.BlockSpec((tm, k), lambda i, j: (i, 0)), pl.BlockSpec((k, tn), lambda i, j: (0, j))],
        out_specs=pl.BlockSpec((tm, tn), lambda i, j: (i, j)),
        out_shape=jax.ShapeDtypeStruct((m, n), BF16),
        compiler_params=_params("parallel", "arbitrary"),
        name="inproj",
    )(h, w)


def _retention_kernel(lg_ref, q_ref, k_ref, v_ref, g_ref, gn_ref, o_ref, r_ref, *, tb):
    h = pl.program_id(1)
    t = pl.program_id(2)

    @pl.when(t == 0)
    def _():
        r_ref[...] = jnp.zeros_like(r_ref)

    lg = lg_ref[h]
    scale = HEAD_DIM ** -0.5
    row = lax.broadcasted_iota(jnp.int32, (CHUNK, CHUNK), 0)
    col = lax.broadcasted_iota(jnp.int32, (CHUNK, CHUNK), 1)
    rel = (row - col).astype(F32)
    decay = jnp.where(rel >= 0, jnp.exp(lg * jnp.maximum(rel, 0.0)), 0.0) * scale
    pos = lax.broadcasted_iota(jnp.int32, (CHUNK, 1), 0).astype(F32)
    zeta = jnp.exp(lg * (CHUNK - 1.0 - pos)) * scale
    xi = jnp.exp(lg * (pos + 1.0))
    chunk_decay = jnp.exp(jnp.full((1, 1), lg * CHUNK, F32))
    gn = gn_ref[0]

    for c in range(tb // CHUNK):
        sl = slice(c * CHUNK, (c + 1) * CHUNK)
        q = q_ref[sl, :]
        k = k_ref[sl, :]
        v = v_ref[sl, :]
        scores = _dot_nt(q, k) * decay
        inner = _dot(scores.astype(BF16), v)
        r_prev = r_ref[...]
        cross = _dot(q, r_prev.astype(BF16)) * xi
        kz = (k.astype(F32) * zeta).astype(BF16)
        r_ref[...] = _dot_tn(kz, v) + chunk_decay * r_prev
        y = inner + cross
        mu = jnp.mean(y, axis=-1, keepdims=True)
        yc = y - mu
        var = jnp.mean(yc * yc, axis=-1, keepdims=True)
        y = yc * lax.rsqrt(var + EPS) * gn
        gate = g_ref[sl, :].astype(F32)
        o_ref[sl, :] = (gate * jax.nn.sigmoid(gate) * y).astype(o_ref.dtype)


def _retention(proj, gn_gain, batch, seq, tb=1024):
    m = proj.shape[0]
    nt = seq // tb
    log_gamma = np.log1p(-(2.0 ** (-5.0 - np.arange(RET_HEADS, dtype=np.float32)))).astype(np.float32)

    def blk(col):
        return pl.BlockSpec((tb, HEAD_DIM), lambda b, h, t: (b * nt + t, col + h))

    return pl.pallas_call(
        functools.partial(_retention_kernel, tb=tb),
        grid=(batch, RET_HEADS, nt),
        in_specs=[pl.BlockSpec(memory_space=pltpu.SMEM),
                  blk(COL_RQ), blk(COL_RK), blk(COL_RV), blk(COL_RG),
                  pl.BlockSpec((1, 1, HEAD_DIM), lambda b, h, t: (h, 0, 0))],
        out_specs=blk(0),
        out_shape=jax.ShapeDtypeStruct((m, RET_W), BF16),
        scratch_shapes=[pltpu.VMEM((HEAD_DIM, HEAD_DIM), F32)],
        compiler_params=_params("arbitrary", "arbitrary", "arbitrary"),
        name="retention",
    )(jnp.asarray(log_gamma), proj, proj, proj, proj, gn_gain.reshape(RET_HEADS, 1, HEAD_DIM))


def _diffattn_kernel(slope_ref, q_ref, k_ref, v_ref, lq1_ref, lk1_ref, lq2_ref, lk2_ref, sg_ref, o_ref,
                     k1_ref, k2_ref, vt_ref, acc1_ref, acc2_ref, *, blk, nblk, lambda_init):
    h = pl.program_id(1)
    qi = pl.program_id(2)
    lane = lax.broadcasted_iota(jnp.int32, (blk, HEAD_DIM), 1)

    @pl.when(qi == 0)
    def _():
        slope2 = slope_ref[h] * LOG2E

        def prep(j, carry):
            start = pl.multiple_of(j * blk, blk)
            kb = k_ref[pl.ds(start, blk), :].astype(F32)
            kpos = (lax.broadcasted_iota(jnp.int32, (blk, HEAD_DIM), 0) + start).astype(F32)
            bias = kpos * slope2
            hi = bias.astype(BF16).astype(F32)
            r1 = bias - hi
            mid = r1.astype(BF16).astype(F32)
            lo = r1 - mid
            zero = jnp.zeros_like(kb)
            k1 = jnp.where(lane < 64, kb,
                           jnp.where(lane == 64, hi, jnp.where(lane == 65, mid, jnp.where(lane == 66, lo, zero))))
            k2 = jnp.where(lane >= 64, kb,
                           jnp.where(lane == 0, hi, jnp.where(lane == 1, mid, jnp.where(lane == 2, lo, zero))))
            k1_ref[j] = k1.astype(BF16)
            k2_ref[j] = k2.astype(BF16)
            vt_ref[j] = v_ref[pl.ds(start, blk), :].astype(F32).T.astype(BF16)
            return carry

        lax.fori_loop(0, nblk, prep, 0)

    qs = q_ref[...].astype(F32) * (DIFF_MAP_DIM ** -0.5 * LOG2E)
    one = jnp.ones_like(qs)
    zero = jnp.zeros_like(qs)
    q1 = jnp.where(lane < 64, qs, jnp.where(lane < 67, one, zero)).astype(BF16)
    q2 = jnp.where(lane >= 64, qs, jnp.where(lane < 3, one, zero)).astype(BF16)

    def scores(kref, j, qa):
        return _dot_nt(kref[j], qa)

    krow = lax.broadcasted_iota(jnp.int32, (blk, blk), 0)
    qcol = lax.broadcasted_iota(jnp.int32, (blk, blk), 1)
    causal = krow <= qcol

    def first(kref, qa, acc_ref):
        s = jnp.where(causal, scores(kref, qi, qa), NEG_BIG)
        m = jnp.max(s, axis=0, keepdims=True)
        p = jnp.exp2(s - m)
        l = jnp.sum(p, axis=0, keepdims=True)
        acc_ref[...] = _dot(vt_ref[qi], p.astype(BF16))
        return m, l

    def update(kref, qa, acc_ref, j, m, l):
        s = scores(kref, j, qa)
        mn = jnp.maximum(m, jnp.max(s, axis=0, keepdims=True))
        p = jnp.exp2(s - mn)
        alpha = jnp.exp2(m - mn)
        l = alpha * l + jnp.sum(p, axis=0, keepdims=True)
        acc_ref[...] = alpha * acc_ref[...] + _dot(vt_ref[j], p.astype(BF16))
        return mn, l

    m1, l1 = first(k1_ref, q1, acc1_ref)
    m2, l2 = first(k2_ref, q2, acc2_ref)

    def body(j, carry):
        m1, l1, m2, l2 = carry
        m1, l1 = update(k1_ref, q1, acc1_ref, j, m1, l1)
        m2, l2 = update(k2_ref, q2, acc2_ref, j, m2, l2)
        return m1, l1, m2, l2

    m1, l1, m2, l2 = lax.fori_loop(0, qi, body, (m1, l1, m2, l2))

    lam = (jnp.exp(jnp.sum(lq1_ref[...] * lk1_ref[...], axis=-1, keepdims=True))
           - jnp.exp(jnp.sum(lq2_ref[...] * lk2_ref[...], axis=-1, keepdims=True)) + lambda_init)
    o_t = acc1_ref[...] / l1 - lam * (acc2_ref[...] / l2)
    o = o_t.T
    o = _rms(o, sg_ref[0]) * (1.0 - lambda_init)
    o_ref[...] = o.astype(o_ref.dtype)


def _diffattn(proj, lq1, lk1, lq2, lk2, subln_g, lambda_init, batch, seq, blk=512):
    m = proj.shape[0]
    nblk = seq // blk
    slopes = (2.0 ** (-8.0 * np.arange(1, DIFF_HEADS + 1, dtype=np.float32) / DIFF_HEADS)).astype(np.float32)
    vec = pl.BlockSpec((1, DIFF_MAP_DIM), lambda b, h, i: (0, 0))
    return pl.pallas_call(
        functools.partial(_diffattn_kernel, blk=blk, nblk=nblk, lambda_init=lambda_init),
        grid=(batch, DIFF_HEADS, nblk),
        in_specs=[pl.BlockSpec(memory_space=pltpu.SMEM),
                  pl.BlockSpec((blk, HEAD_DIM), lambda b, h, i: (b * nblk + i, COL_DQ + h)),
                  pl.BlockSpec((seq, HEAD_DIM), lambda b, h, i: (b, COL_DK + h)),
                  pl.BlockSpec((seq, HEAD_DIM), lambda b, h, i: (b, COL_DV + h)),
                  vec, vec, vec, vec,
                  pl.BlockSpec((1, 1, HEAD_DIM), lambda b, h, i: (h, 0, 0))],
        out_specs=pl.BlockSpec((blk, HEAD_DIM), lambda b, h, i: (b * nblk + i, h)),
        out_shape=jax.ShapeDtypeStruct((m, DIFF_W), BF16),
        scratch_shapes=[pltpu.VMEM((nblk, blk, HEAD_DIM), BF16),
                        pltpu.VMEM((nblk, blk, HEAD_DIM), BF16),
                        pltpu.VMEM((nblk, HEAD_DIM, blk), BF16),
                        pltpu.VMEM((HEAD_DIM, blk), F32),
                        pltpu.VMEM((HEAD_DIM, blk), F32)],
        compiler_params=_params("arbitrary", "arbitrary", "arbitrary"),
        name="diffattn",
    )(jnp.asarray(slopes), proj, proj, proj,
      lq1.reshape(1, -1), lk1.reshape(1, -1), lq2.reshape(1, -1), lk2.reshape(1, -1),
      subln_g.reshape(DIFF_HEADS, 1, HEAD_DIM))


def _sgu_kernel(u_ref, v_ref, lng_ref, lnb_ref, w_ref, b_ref, o_ref, *, tb):
    row = lax.broadcasted_iota(jnp.int32, (CHUNK, CHUNK), 0)
    col = lax.broadcasted_iota(jnp.int32, (CHUNK, CHUNK), 1)
    w = jnp.where(row >= col, w_ref[0], 0.0).astype(BF16)
    bias = b_ref[0]
    lng = lng_ref[0]
    lnb = lnb_ref[0]
    for c in range(tb // CHUNK):
        sl = slice(c * CHUNK, (c + 1) * CHUNK)
        u = jax.nn.gelu(u_ref[sl, :].astype(F32))
        v = jax.nn.gelu(v_ref[sl, :].astype(F32))
        mu = jnp.mean(v, axis=-1, keepdims=True)
        vc = v - mu
        var = jnp.mean(vc * vc, axis=-1, keepdims=True)
        vn = vc * lax.rsqrt(var + EPS) * lng + lnb
        mixed = _dot(w, vn.astype(BF16)) + bias
        o_ref[sl, :] = (u * mixed).astype(o_ref.dtype)


def _sgu(proj, ln_g, ln_b, w_s, b_s, tb=1024):
    m = proj.shape[0]
    g = SGU_GROUPS
    per_group = lambda i, j: (j, 0, 0)
    return pl.pallas_call(
        functools.partial(_sgu_kernel, tb=tb),
        grid=(m // tb, g),
        in_specs=[pl.BlockSpec((tb, HEAD_DIM), lambda i, j: (i, COL_SU + j)),
                  pl.BlockSpec((tb, HEAD_DIM), lambda i, j: (i, COL_SV + j)),
                  pl.BlockSpec((1, 1, HEAD_DIM), per_group),
                  pl.BlockSpec((1, 1, HEAD_DIM), per_group),
                  pl.BlockSpec((1, CHUNK, CHUNK), per_group),
                  pl.BlockSpec((1, CHUNK, 1), per_group)],
        out_specs=pl.BlockSpec((tb, HEAD_DIM), lambda i, j: (i, j)),
        out_shape=jax.ShapeDtypeStruct((m, SGU_W), BF16),
        compiler_params=_params("parallel", "arbitrary"),
        name="sgu",
    )(proj, proj, ln_g.reshape(g, 1, HEAD_DIM), ln_b.reshape(g, 1, HEAD_DIM), w_s, b_s.reshape(g, CHUNK, 1))


def _outproj_kernel(yr_ref, yd_ref, ys_ref, w_ref, x_ref, gpost_ref, gnext_ref, xo_ref, ho_ref):
    mix = (_dot(yr_ref[...], w_ref[0:RET_W, :])
           + _dot(yd_ref[...], w_ref[RET_W:RET_W + DIFF_W, :])
           + _dot(ys_ref[...], w_ref[RET_W + DIFF_W:D_MODEL, :]))
    xn = x_ref[...] + _rms(mix, gpost_ref[...])
    xo_ref[...] = xn
    ho_ref[...] = _rms(xn, gnext_ref[...]).astype(ho_ref.dtype)


def _outproj(y_ret, y_diff, y_sgu, w, x, g_post, g_next, tm=512):
    m, d = x.shape
    row = lambda i: (i, 0)
    const = lambda i: (0, 0)
    return pl.pallas_call(
        _outproj_kernel,
        grid=(m // tm,),
        in_specs=[pl.BlockSpec((tm, RET_W), row), pl.BlockSpec((tm, DIFF_W), row), pl.BlockSpec((tm, SGU_W), row),
                  pl.BlockSpec((d, d), const), pl.BlockSpec((tm, d), row),
                  pl.BlockSpec((1, d), const), pl.BlockSpec((1, d), const)],
        out_specs=[pl.BlockSpec((tm, d), row), pl.BlockSpec((tm, d), row)],
        out_shape=[jax.ShapeDtypeStruct((m, d), F32), jax.ShapeDtypeStruct((m, d), BF16)],
        compiler_params=_params("parallel"),
        name="outproj",
    )(y_ret, y_diff, y_sgu, w, x, g_post.reshape(1, d), g_next.reshape(1, d))


def _ffn_kernel(h_ref, x_ref, wg_ref, wu_ref, wd_ref, gpost_ref, gnext_ref, xo_ref, ho_ref, acc_ref):
    j = pl.program_id(1)
    h = h_ref[...]
    gate = _dot(h, wg_ref[...])
    up = _dot(h, wu_ref[...])
    act = (gate * jax.nn.sigmoid(gate) * up).astype(BF16)
    part = _dot(act, wd_ref[...])

    @pl.when(j == 0)
    def _():
        acc_ref[...] = part

    @pl.when(j > 0)
    def _():
        acc_ref[...] += part

    @pl.when(j == pl.num_programs(1) - 1)
    def _():
        xn = x_ref[...] + _rms(acc_ref[...], gpost_ref[...])
        xo_ref[...] = xn
        ho_ref[...] = _rms(xn, gnext_ref[...]).astype(ho_ref.dtype)


def _ffn(h, x, wg, wu, wd, g_post, g_next, tm=512, th=512):
    m, d = x.shape
    hidden = wg.shape[1]
    row = lambda i, j: (i, 0)
    const = lambda i, j: (0, 0)
    return pl.pallas_call(
        _ffn_kernel,
        grid=(m // tm, hidden // th),
        in_specs=[pl.BlockSpec((tm, d), row), pl.BlockSpec((tm, d), row),
                  pl.BlockSpec((d, th), lambda i, j: (0, j)), pl.BlockSpec((d, th), lambda i, j: (0, j)),
                  pl.BlockSpec((th, d), lambda i, j: (j, 0)),
                  pl.BlockSpec((1, d), const), pl.BlockSpec((1, d), const)],
        out_specs=[pl.BlockSpec((tm, d), row), pl.BlockSpec((tm, d), row)],
        out_shape=[jax.ShapeDtypeStruct((m, d), F32), jax.ShapeDtypeStruct((m, d), BF16)],
        scratch_shapes=[pltpu.VMEM((tm, d), F32)],
        compiler_params=_params("parallel", "arbitrary"),
        name="ffn",
    )(h, x, wg, wu, wd, g_post.reshape(1, d), g_next.reshape(1, d))


def kernel(x, pre_mix_g, w_in, ret_gn_g, diff_lam_q1, diff_lam_k1, diff_lam_q2, diff_lam_k2, diff_subln_g,
           sgu_ln_g, sgu_ln_b, sgu_w, sgu_b, w_out, post_mix_g, pre_ffn_g, w_gate, w_up, w_down, post_ffn_g):
    batch, seq, d = x.shape
    depth = w_in.shape[0]
    xf = x.reshape(batch * seq, d)
    h = _rmsnorm(xf, pre_mix_g[0])
    for l in range(depth):
        lambda_init = 0.8 - 0.6 * math.exp(-0.3 * l)
        proj = _inproj(h, w_in[l].astype(BF16))
        y_ret = _retention(proj, ret_gn_g[l], batch, seq)
        y_diff = _diffattn(proj, diff_lam_q1[l], diff_lam_k1[l], diff_lam_q2[l], diff_lam_k2[l],
                           diff_subln_g[l], lambda_init, batch, seq)
        y_sgu = _sgu(proj, sgu_ln_g[l], sgu_ln_b[l], sgu_w[l], sgu_b[l])
        xf, h = _outproj(y_ret, y_diff, y_sgu, w_out[l].astype(BF16), xf, post_mix_g[l], pre_ffn_g[l])
        g_next = pre_mix_g[(l + 1) % depth]
        xf, h = _ffn(h, xf, w_gate[l].astype(BF16), w_up[l].astype(BF16), w_down[l].astype(BF16),
                     post_ffn_g[l], g_next)
    return xf.reshape(batch, seq, d)
```

```python
import functools
import math

import numpy as np
import jax
import jax.numpy as jnp
from jax import lax
from jax.experimental import pallas as pl
from jax.experimental.pallas import tpu as pltpu

D_MODEL = 2048
HEAD_DIM = 128
RET_HEADS = 6
DIFF_HEADS = 6
SGU_GROUPS = 4
DIFF_MAP_DIM = 64
RET_W = RET_HEADS * HEAD_DIM
DIFF_W = DIFF_HEADS * HEAD_DIM
SGU_W = SGU_GROUPS * HEAD_DIM
IN_W = 4 * RET_W + 3 * DIFF_W + 2 * SGU_W
CHUNK = 128
EPS = 1e-6
LOG2E = math.log2(math.e)
NEG_BIG = -1e30
SUM_ROWS = 16

COL_RQ, COL_RK, COL_RV, COL_RG = 0, 6, 12, 18
COL_DQ, COL_DK, COL_DV = 24, 30, 36
COL_SU, COL_SV = 42, 46

V7X_VMEM_BYTES = 64 * 1024 * 1024
VMEM_LIMIT = 56 * 1024 * 1024

BF16 = jnp.bfloat16
F32 = jnp.float32


def _dot(a, b):
    return jnp.dot(a, b, preferred_element_type=F32)


def _dot_nt(a, b):
    return lax.dot_general(a, b, (((1,), (1,)), ((), ())), preferred_element_type=F32)


def _dot_tn(a, b):
    return lax.dot_general(a, b, (((0,), (0,)), ((), ())), preferred_element_type=F32)


def _rms(x, g):
    return x * lax.rsqrt(jnp.mean(x * x, axis=-1, keepdims=True) + EPS) * g


def _params(*sem):
    return pltpu.CompilerParams(dimension_semantics=sem, vmem_limit_bytes=VMEM_LIMIT)


def _rmsnorm_kernel(x_ref, g_ref, o_ref):
    o_ref[...] = _rms(x_ref[...], g_ref[...]).astype(o_ref.dtype)


def _rmsnorm(x, g, tm=512):
    m, d = x.shape
    return pl.pallas_call(
        _rmsnorm_kernel,
        grid=(m // tm,),
        in_specs=[pl.BlockSpec((tm, d), lambda i: (i, 0)), pl.BlockSpec((1, d), lambda i: (0, 0))],
        out_specs=pl.BlockSpec((tm, d), lambda i: (i, 0)),
        out_shape=jax.ShapeDtypeStruct((m, d), BF16),
        compiler_params=_params("parallel"),
        name="rmsnorm",
    )(x, g.reshape(1, d))


def _matmul_kernel(h_ref, w_ref, o_ref):
    o_ref[...] = _dot(h_ref[...], w_ref[...]).astype(o_ref.dtype)


def _inproj(h, w, tm=1024, tn=1280):
    m, k = h.shape
    n = w.shape[1]
    return pl.pallas_call(
        _matmul_kernel,
        grid=(m // tm, n // tn),
        in_specs=[pl.BlockSpec((tm, k), lambda i, j: (i, 0)), pl.BlockSpec((k, tn), lambda i, j: (0, j))],
        out_specs=pl.BlockSpec((tm, tn), lambda i, j: (i, j)),
        out_shape=jax.ShapeDtypeStruct((m, n), BF16),
        compiler_params=_params("parallel", "arbitrary"),
        name="inproj",
    )(h, w)


def _retention_kernel(lg_ref, q_ref, k_ref, v_ref, g_ref, gn_ref, o_ref, r_ref, *, tb):
    h = pl.program_id(1)
    t = pl.program_id(2)

    @pl.when(t == 0)
    def _():
        r_ref[...] = jnp.zeros_like(r_ref)

    lg = lg_ref[h]
    scale = HEAD_DIM ** -0.5
    row = lax.broadcasted_iota(jnp.int32, (CHUNK, CHUNK), 0)
    col = lax.broadcasted_iota(jnp.int32, (CHUNK, CHUNK), 1)
    rel = (row - col).astype(F32)
    decay = jnp.where(rel >= 0, jnp.exp(lg * jnp.maximum(rel, 0.0)), 0.0) * scale
    pos = lax.broadcasted_iota(jnp.int32, (CHUNK, 1), 0).astype(F32)
    zeta = jnp.exp(lg * (CHUNK - 1.0 - pos)) * scale
    xi = jnp.exp(lg * (pos + 1.0))
    chunk_decay = jnp.exp(jnp.full((1, 1), lg * CHUNK, F32))
    gn = gn_ref[0]

    for c in range(tb // CHUNK):
        sl = slice(c * CHUNK, (c + 1) * CHUNK)
        q = q_ref[sl, :]
        k = k_ref[sl, :]
        v = v_ref[sl, :]
        scores = _dot_nt(q, k) * decay
        inner = _dot(scores.astype(BF16), v)
        r_prev = r_ref[...]
        cross = _dot(q, r_prev.astype(BF16)) * xi
        kz = (k.astype(F32) * zeta).astype(BF16)
        r_ref[...] = _dot_tn(kz, v) + chunk_decay * r_prev
        y = inner + cross
        mu = jnp.mean(y, axis=-1, keepdims=True)
        yc = y - mu
        var = jnp.mean(yc * yc, axis=-1, keepdims=True)
        y = yc * lax.rsqrt(var + EPS) * gn
        gate = g_ref[sl, :].astype(F32)
        o_ref[sl, :] = (gate * jax.nn.sigmoid(gate) * y).astype(o_ref.dtype)


def _retention(proj, gn_gain, batch, seq, tb=1024):
    m = proj.shape[0]
    nt = seq // tb
    log_gamma = np.log1p(-(2.0 ** (-5.0 - np.arange(RET_HEADS, dtype=np.float32)))).astype(np.float32)

    def blk(col):
        return pl.BlockSpec((tb, HEAD_DIM), lambda b, h, t: (b * nt + t, col + h))

    return pl.pallas_call(
        functools.partial(_retention_kernel, tb=tb),
        grid=(batch, RET_HEADS, nt),
        in_specs=[pl.BlockSpec(memory_space=pltpu.SMEM),
                  blk(COL_RQ), blk(COL_RK), blk(COL_RV), blk(COL_RG),
                  pl.BlockSpec((1, 1, HEAD_DIM), lambda b, h, t: (h, 0, 0))],
        out_specs=blk(0),
        out_shape=jax.ShapeDtypeStruct((m, RET_W), BF16),
        scratch_shapes=[pltpu.VMEM((HEAD_DIM, HEAD_DIM), F32)],
        compiler_params=_params("arbitrary", "arbitrary", "arbitrary"),
        name="retention",
    )(jnp.asarray(log_gamma), proj, proj, proj, proj, gn_gain.reshape(RET_HEADS, 1, HEAD_DIM))


def _diffattn_kernel(slope_ref, q_ref, k_ref, v_ref, lq1_ref, lk1_ref, lq2_ref, lk2_ref, sg_ref, o_ref,
                     k1_ref, k2_ref, vt_ref, acc_ref, sa_ref, sb_ref, pa_ref, pb_ref, *, blk, nblk, lambda_init):
    h = pl.program_id(1)
    qi = pl.program_id(2)
    lane = lax.broadcasted_iota(jnp.int32, (blk, HEAD_DIM), 1)

    @pl.when(qi == 0)
    def _():
        slope2 = slope_ref[h] * LOG2E

        def prep(j, carry):
            start = pl.multiple_of(j * blk, blk)
            kb = k_ref[pl.ds(start, blk), :].astype(F32)
            kpos = (lax.broadcasted_iota(jnp.int32, (blk, HEAD_DIM), 0) + start).astype(F32)
            bias = kpos * slope2
            hi = bias.astype(BF16).astype(F32)
            r1 = bias - hi
            mid = r1.astype(BF16).astype(F32)
            lo = r1 - mid
            zero = jnp.zeros_like(kb)
            k1 = jnp.where(lane < 64, kb,
                           jnp.where(lane == 64, hi, jnp.where(lane == 65, mid, jnp.where(lane == 66, lo, zero))))
            k2 = jnp.where(lane >= 64, kb,
                           jnp.where(lane == 0, hi, jnp.where(lane == 1, mid, jnp.where(lane == 2, lo, zero))))
            k1_ref[j] = k1.astype(BF16)
            k2_ref[j] = k2.astype(BF16)
            vt_ref[j, 0:HEAD_DIM, :] = v_ref[pl.ds(start, blk), :].astype(F32).T.astype(BF16)
            vt_ref[j, HEAD_DIM:, :] = jnp.ones((SUM_ROWS, blk), BF16)
            return carry

        lax.fori_loop(0, nblk, prep, 0)

    qs = q_ref[...].astype(F32) * (DIFF_MAP_DIM ** -0.5 * LOG2E)
    one = jnp.ones_like(qs)
    zero = jnp.zeros_like(qs)
    qa = (jnp.where(lane < 64, qs, jnp.where(lane < 67, one, zero)).astype(BF16),
          jnp.where(lane >= 64, qs, jnp.where(lane < 3, one, zero)).astype(BF16))
    ka = (k1_ref, k2_ref)

    def scores(mp, j):
        return _dot_nt(ka[mp][j], qa[mp])

    def pv_update(mp, j, p_ref, alpha):
        acc_ref[mp] = alpha * acc_ref[mp] + _dot(vt_ref[j], p_ref[mp])

    def stage(pj, nxt, s_in, s_out, p_in, p_out, carry):
        out = []
        for mp in range(2):
            m, alpha, mb = carry[3 * mp:3 * mp + 3]
            pv_update(mp, pj, p_in, alpha)
            mn = jnp.maximum(m, mb)
            p_out[mp] = jnp.exp2(s_in[mp] - mn).astype(BF16)
            alpha = jnp.exp2(m - mn)
            if nxt is not None:
                s = scores(mp, nxt)
                s_out[mp] = s
                mb = jnp.max(s, axis=0, keepdims=True)
            out += [mn, alpha, mb]
        return tuple(out)

    krow = lax.broadcasted_iota(jnp.int32, (blk, blk), 0)
    qcol = lax.broadcasted_iota(jnp.int32, (blk, blk), 1)
    causal = krow <= qcol
    carry = []
    for mp in range(2):
        s = jnp.where(causal, scores(mp, qi), NEG_BIG)
        m = jnp.max(s, axis=0, keepdims=True)
        pa_ref[mp] = jnp.exp2(s - m).astype(BF16)
        acc_ref[mp] = jnp.zeros((HEAD_DIM + SUM_ROWS, blk), F32)
        s = scores(mp, 0)
        sa_ref[mp] = s
        carry += [m, jnp.ones_like(m), jnp.max(s, axis=0, keepdims=True)]
    carry = tuple(carry)

    def pair(u, carry):
        t = 2 * u
        carry = stage(jnp.where(t == 0, qi, t - 1), t + 1, sa_ref, sb_ref, pa_ref, pb_ref, carry)
        return stage(t, jnp.minimum(t + 2, qi - 1), sb_ref, sa_ref, pb_ref, pa_ref, carry)

    carry = lax.fori_loop(0, qi // 2, pair, carry)
    last = jnp.maximum(qi - 1, 0)

    def odd_tail(carry):
        t = qi - 1
        carry = stage(jnp.where(t == 0, qi, t - 1), None, sa_ref, sb_ref, pa_ref, pb_ref, carry)
        for mp in range(2):
            pv_update(mp, last, pb_ref, carry[3 * mp + 1])
        return carry

    def even_tail(carry):
        for mp in range(2):
            pv_update(mp, last, pa_ref, carry[3 * mp + 1])
        return carry

    lax.cond(qi % 2 == 1, odd_tail, even_tail, carry)

    lam = (jnp.exp(jnp.sum(lq1_ref[...] * lk1_ref[...], axis=-1, keepdims=True))
           - jnp.exp(jnp.sum(lq2_ref[...] * lk2_ref[...], axis=-1, keepdims=True)) + lambda_init)
    l1 = acc_ref[0, HEAD_DIM:HEAD_DIM + 1, :]
    l2 = acc_ref[1, HEAD_DIM:HEAD_DIM + 1, :]
    o_t = acc_ref[0, 0:HEAD_DIM, :] / l1 - lam * (acc_ref[1, 0:HEAD_DIM, :] / l2)
    o = o_t.T
    o = _rms(o, sg_ref[0]) * (1.0 - lambda_init)
    o_ref[...] = o.astype(o_ref.dtype)


def _diffattn(proj, lq1, lk1, lq2, lk2, subln_g, lambda_init, batch, seq, blk=512):
    m = proj.shape[0]
    nblk = seq // blk
    slopes = (2.0 ** (-8.0 * np.arange(1, DIFF_HEADS + 1, dtype=np.float32) / DIFF_HEADS)).astype(np.float32)
    vec = pl.BlockSpec((1, DIFF_MAP_DIM), lambda b, h, i: (0, 0))
    return pl.pallas_call(
        functools.partial(_diffattn_kernel, blk=blk, nblk=nblk, lambda_init=lambda_init),
        grid=(batch, DIFF_HEADS, nblk),
        in_specs=[pl.BlockSpec(memory_space=pltpu.SMEM),
                  pl.BlockSpec((blk, HEAD_DIM), lambda b, h, i: (b * nblk + i, COL_DQ + h)),
                  pl.BlockSpec((seq, HEAD_DIM), lambda b, h, i: (b, COL_DK + h)),
                  pl.BlockSpec((seq, HEAD_DIM), lambda b, h, i: (b, COL_DV + h)),
                  vec, vec, vec, vec,
                  pl.BlockSpec((1, 1, HEAD_DIM), lambda b, h, i: (h, 0, 0))],
        out_specs=pl.BlockSpec((blk, HEAD_DIM), lambda b, h, i: (b * nblk + i, h)),
        out_shape=jax.ShapeDtypeStruct((m, DIFF_W), BF16),
        scratch_shapes=[pltpu.VMEM((nblk, blk, HEAD_DIM), BF16),
                        pltpu.VMEM((nblk, blk, HEAD_DIM), BF16),
                        pltpu.VMEM((nblk, HEAD_DIM + SUM_ROWS, blk), BF16),
                        pltpu.VMEM((2, HEAD_DIM + SUM_ROWS, blk), F32),
                        pltpu.VMEM((2, blk, blk), F32),
                        pltpu.VMEM((2, blk, blk), F32),
                        pltpu.VMEM((2, blk, blk), BF16),
                        pltpu.VMEM((2, blk, blk), BF16)],
        compiler_params=_params("arbitrary", "arbitrary", "arbitrary"),
        name="diffattn",
    )(jnp.asarray(slopes), proj, proj, proj,
      lq1.reshape(1, -1), lk1.reshape(1, -1), lq2.reshape(1, -1), lk2.reshape(1, -1),
      subln_g.reshape(DIFF_HEADS, 1, HEAD_DIM))


def _sgu_kernel(u_ref, v_ref, lng_ref, lnb_ref, w_ref, b_ref, o_ref, *, tb):
    row = lax.broadcasted_iota(jnp.int32, (CHUNK, CHUNK), 0)
    col = lax.broadcasted_iota(jnp.int32, (CHUNK, CHUNK), 1)
    w = jnp.where(row >= col, w_ref[0], 0.0).astype(BF16)
    bias = b_ref[0]
    lng = lng_ref[0]
    lnb = lnb_ref[0]
    for c in range(tb // CHUNK):
        sl = slice(c * CHUNK, (c + 1) * CHUNK)
        u = jax.nn.gelu(u_ref[sl, :].astype(F32))
        v = jax.nn.gelu(v_ref[sl, :].astype(F32))
        mu = jnp.mean(v, axis=-1, keepdims=True)
        vc = v - mu
        var = jnp.mean(vc * vc, axis=-1, keepdims=True)
        vn = vc * lax.rsqrt(var + EPS) * lng + lnb
        mixed = _dot(w, vn.astype(BF16)) + bias
        o_ref[sl, :] = (u * mixed).astype(o_ref.dtype)


def _sgu(proj, ln_g, ln_b, w_s, b_s, tb=1024):
    m = proj.shape[0]
    g = SGU_GROUPS
    per_group = lambda i, j: (j, 0, 0)
    return pl.pallas_call(
        functools.partial(_sgu_kernel, tb=tb),
        grid=(m // tb, g),
        in_specs=[pl.BlockSpec((tb, HEAD_DIM), lambda i, j: (i, COL_SU + j)),
                  pl.BlockSpec((tb, HEAD_DIM), lambda i, j: (i, COL_SV + j)),
                  pl.BlockSpec((1, 1, HEAD_DIM), per_group),
                  pl.BlockSpec((1, 1, HEAD_DIM), per_group),
                  pl.BlockSpec((1, CHUNK, CHUNK), per_group),
                  pl.BlockSpec((1, CHUNK, 1), per_group)],
        out_specs=pl.BlockSpec((tb, HEAD_DIM), lambda i, j: (i, j)),
        out_shape=jax.ShapeDtypeStruct((m, SGU_W), BF16),
        compiler_params=_params("parallel", "arbitrary"),
        name="sgu",
    )(proj, proj, ln_g.reshape(g, 1, HEAD_DIM), ln_b.reshape(g, 1, HEAD_DIM), w_s, b_s.reshape(g, CHUNK, 1))


def _outproj_kernel(yr_ref, yd_ref, ys_ref, w_ref, x_ref, gpost_ref, gnext_ref, xo_ref, ho_ref):
    mix = (_dot(yr_ref[...], w_ref[0:RET_W, :])
           + _dot(yd_ref[...], w_ref[RET_W:RET_W + DIFF_W, :])
           + _dot(ys_ref[...], w_ref[RET_W + DIFF_W:D_MODEL, :]))
    xn = x_ref[...] + _rms(mix, gpost_ref[...])
    xo_ref[...] = xn
    ho_ref[...] = _rms(xn, gnext_ref[...]).astype(ho_ref.dtype)


def _outproj(y_ret, y_diff, y_sgu, w, x, g_post, g_next, tm=512):
    m, d = x.shape
    row = lambda i: (i, 0)
    const = lambda i: (0, 0)
    return pl.pallas_call(
        _outproj_kernel,
        grid=(m // tm,),
        in_specs=[pl.BlockSpec((tm, RET_W), row), pl.BlockSpec((tm, DIFF_W), row), pl.BlockSpec((tm, SGU_W), row),
                  pl.BlockSpec((d, d), const), pl.BlockSpec((tm, d), row),
                  pl.BlockSpec((1, d), const), pl.BlockSpec((1, d), const)],
        out_specs=[pl.BlockSpec((tm, d), row), pl.BlockSpec((tm, d), row)],
        out_shape=[jax.ShapeDtypeStruct((m, d), F32), jax.ShapeDtypeStruct((m, d), BF16)],
        compiler_params=_params("parallel"),
        name="outproj",
    )(y_ret, y_diff, y_sgu, w, x, g_post.reshape(1, d), g_next.reshape(1, d))


def _ffn_kernel(h_ref, x_ref, wg_ref, wu_ref, wd_ref, gpost_ref, gnext_ref, xo_ref, ho_ref, acc_ref):
    j = pl.program_id(1)
    h = h_ref[...]
    gate = _dot(h, wg_ref[...])
    up = _dot(h, wu_ref[...])
    act = (gate * jax.nn.sigmoid(gate) * up).astype(BF16)
    part = _dot(act, wd_ref[...])

    @pl.when(j == 0)
    def _():
        acc_ref[...] = part

    @pl.when(j > 0)
    def _():
        acc_ref[...] += part

    @pl.when(j == pl.num_programs(1) - 1)
    def _():
        xn = x_ref[...] + _rms(acc_ref[...], gpost_ref[...])
        xo_ref[...] = xn
        ho_ref[...] = _rms(xn, gnext_ref[...]).astype(ho_ref.dtype)


def _ffn(h, x, wg, wu, wd, g_post, g_next, tm=512, th=512):
    m, d = x.shape
    hidden = wg.shape[1]
    row = lambda i, j: (i, 0)
    const = lambda i, j: (0, 0)
    return pl.pallas_call(
        _ffn_kernel,
        grid=(m // tm, hidden // th),
        in_specs=[pl.BlockSpec((tm, d), row), pl.BlockSpec((tm, d), row),
                  pl.BlockSpec((d, th), lambda i, j: (0, j)), pl.BlockSpec((d, th), lambda i, j: (0, j)),
                  pl.BlockSpec((th, d), lambda i, j: (j, 0)),
                  pl.BlockSpec((1, d), const), pl.BlockSpec((1, d), const)],
        out_specs=[pl.BlockSpec((tm, d), row), pl.BlockSpec((tm, d), row)],
        out_shape=[jax.ShapeDtypeStruct((m, d), F32), jax.ShapeDtypeStruct((m, d), BF16)],
        scratch_shapes=[pltpu.VMEM((tm, d), F32)],
        compiler_params=_params("parallel", "arbitrary"),
        name="ffn",
    )(h, x, wg, wu, wd, g_post.reshape(1, d), g_next.reshape(1, d))


def kernel(x, pre_mix_g, w_in, ret_gn_g, diff_lam_q1, diff_lam_k1, diff_lam_q2, diff_lam_k2, diff_subln_g,
           sgu_ln_g, sgu_ln_b, sgu_w, sgu_b, w_out, post_mix_g, pre_ffn_g, w_gate, w_up, w_down, post_ffn_g):
    batch, seq, d = x.shape
    depth = w_in.shape[0]
    xf = x.reshape(batch * seq, d)
    h = _rmsnorm(xf, pre_mix_g[0])
    for l in range(depth):
        lambda_init = 0.8 - 0.6 * math.exp(-0.3 * l)
        proj = _inproj(h, w_in[l].astype(BF16))
        y_ret = _retention(proj, ret_gn_g[l], batch, seq)
        y_diff = _diffattn(proj, diff_lam_q1[l], diff_lam_k1[l], diff_lam_q2[l], diff_lam_k2[l],
                           diff_subln_g[l], lambda_init, batch, seq)
        y_sgu = _sgu(proj, sgu_ln_g[l], sgu_ln_b[l], sgu_w[l], sgu_b[l])
        xf, h = _outproj(y_ret, y_diff, y_sgu, w_out[l].astype(BF16), xf, post_mix_g[l], pre_ffn_g[l])
        g_next = pre_mix_g[(l + 1) % depth]
        xf, h = _ffn(h, xf, w_gate[l].astype(BF16), w_up[l].astype(BF16), w_down[l].astype(BF16),
                     post_ffn_g[l], g_next)
    return xf.reshape(batch, seq, d)
```

```python
import functools
import math

import numpy as np
import jax
import jax.numpy as jnp
from jax import lax
from jax.experimental import pallas as pl
from jax.experimental.pallas import tpu as pltpu

D_MODEL = 2048
HEAD_DIM = 128
RET_HEADS = 6
DIFF_HEADS = 6
SGU_GROUPS = 4
DIFF_MAP_DIM = 64
RET_W = RET_HEADS * HEAD_DIM
DIFF_W = DIFF_HEADS * HEAD_DIM
SGU_W = SGU_GROUPS * HEAD_DIM
IN_W = 4 * RET_W + 3 * DIFF_W + 2 * SGU_W
CHUNK = 128
EPS = 1e-6
LOG2E = math.log2(math.e)
NEG_BIG = -1e30
SUM_ROWS = 16
FFN_TILES = 11
QCHUNK = 256

COL_RQ, COL_RK, COL_RV, COL_RG = 0, 6, 12, 18
COL_DQ, COL_DK, COL_DV = 24, 30, 36
COL_SU, COL_SV = 42, 46

V7X_VMEM_BYTES = 64 * 1024 * 1024
VMEM_LIMIT = 56 * 1024 * 1024

BF16 = jnp.bfloat16
F32 = jnp.float32


def _dot(a, b):
    return jnp.dot(a, b, preferred_element_type=F32)


def _dot_nt(a, b):
    return lax.dot_general(a, b, (((1,), (1,)), ((), ())), preferred_element_type=F32)


def _dot_tn(a, b):
    return lax.dot_general(a, b, (((0,), (0,)), ((), ())), preferred_element_type=F32)


def _rms(x, g):
    return x * lax.rsqrt(jnp.mean(x * x, axis=-1, keepdims=True) + EPS) * g


def _params(*sem):
    return pltpu.CompilerParams(dimension_semantics=sem, vmem_limit_bytes=VMEM_LIMIT)


def _rmsnorm_kernel(x_ref, g_ref, o_ref):
    o_ref[...] = _rms(x_ref[...], g_ref[...]).astype(o_ref.dtype)


def _rmsnorm(x, g, tm=512):
    m, d = x.shape
    return pl.pallas_call(
        _rmsnorm_kernel,
        grid=(m // tm,),
        in_specs=[pl.BlockSpec((tm, d), lambda i: (i, 0)), pl.BlockSpec((1, d), lambda i: (0, 0))],
        out_specs=pl.BlockSpec((tm, d), lambda i: (i, 0)),
        out_shape=jax.ShapeDtypeStruct((m, d), BF16),
        compiler_params=_params("parallel"),
        name="rmsnorm",
    )(x, g.reshape(1, d))


def _matmul_kernel(h_ref, w_ref, o_ref):
    o_ref[...] = _dot(h_ref[...], w_ref[...]).astype(o_ref.dtype)


def _inproj(h, w, tm=1024, tn=1280):
    m, k = h.shape
    n = w.shape[1]
    return pl.pallas_call(
        _matmul_kernel,
        grid=(m // tm, n // tn),
        in_specs=[pl.BlockSpec((tm, k), lambda i, j: (i, 0)), pl.BlockSpec((k, tn), lambda i, j: (0, j))],
        out_specs=pl.BlockSpec((tm, tn), lambda i, j: (i, j)),
        out_shape=jax.ShapeDtypeStruct((m, n), BF16),
        compiler_params=_params("parallel", "arbitrary"),
        name="inproj",
    )(h, w)


def _retention_kernel(lg_ref, q_ref, k_ref, v_ref, g_ref, gn_ref, o_ref, r_ref, *, tb):
    h = pl.program_id(1)
    t = pl.program_id(2)

    @pl.when(t == 0)
    def _():
        r_ref[...] = jnp.zeros_like(r_ref)

    lg = lg_ref[h]
    scale = HEAD_DIM ** -0.5
    row = lax.broadcasted_iota(jnp.int32, (CHUNK, CHUNK), 0)
    col = lax.broadcasted_iota(jnp.int32, (CHUNK, CHUNK), 1)
    rel = (row - col).astype(F32)
    decay = jnp.where(rel >= 0, jnp.exp(lg * jnp.maximum(rel, 0.0)), 0.0) * scale
    pos = lax.broadcasted_iota(jnp.int32, (CHUNK, 1), 0).astype(F32)
    zeta = jnp.exp(lg * (CHUNK - 1.0 - pos)) * scale
    xi = jnp.exp(lg * (pos + 1.0))
    chunk_decay = jnp.exp(jnp.full((1, 1), lg * CHUNK, F32))
    gn = gn_ref[0]

    for c in range(tb // CHUNK):
        sl = slice(c * CHUNK, (c + 1) * CHUNK)
        q = q_ref[sl, :]
        k = k_ref[sl, :]
        v = v_ref[sl, :]
        scores = _dot_nt(q, k) * decay
        inner = _dot(scores.astype(BF16), v)
        r_prev = r_ref[...]
        cross = _dot(q, r_prev.astype(BF16)) * xi
        kz = (k.astype(F32) * zeta).astype(BF16)
        r_ref[...] = _dot_tn(kz, v) + chunk_decay * r_prev
        y = inner + cross
        mu = jnp.mean(y, axis=-1, keepdims=True)
        yc = y - mu
        var = jnp.mean(yc * yc, axis=-1, keepdims=True)
        y = yc * lax.rsqrt(var + EPS) * gn
        gate = g_ref[sl, :].astype(F32)
        o_ref[sl, :] = (gate * jax.nn.sigmoid(gate) * y).astype(o_ref.dtype)


def _retention(proj, gn_gain, batch, seq, tb=1024):
    m = proj.shape[0]
    nt = seq // tb
    log_gamma = np.log1p(-(2.0 ** (-5.0 - np.arange(RET_HEADS, dtype=np.float32)))).astype(np.float32)

    def blk(col):
        return pl.BlockSpec((tb, HEAD_DIM), lambda b, h, t: (b * nt + t, col + h))

    return pl.pallas_call(
        functools.partial(_retention_kernel, tb=tb),
        grid=(batch, RET_HEADS, nt),
        in_specs=[pl.BlockSpec(memory_space=pltpu.SMEM),
                  blk(COL_RQ), blk(COL_RK), blk(COL_RV), blk(COL_RG),
                  pl.BlockSpec((1, 1, HEAD_DIM), lambda b, h, t: (h, 0, 0))],
        out_specs=blk(0),
        out_shape=jax.ShapeDtypeStruct((m, RET_W), BF16),
        scratch_shapes=[pltpu.VMEM((HEAD_DIM, HEAD_DIM), F32)],
        compiler_params=_params("arbitrary", "arbitrary", "arbitrary"),
        name="retention",
    )(jnp.asarray(log_gamma), proj, proj, proj, proj, gn_gain.reshape(RET_HEADS, 1, HEAD_DIM))


def _diffattn_kernel(slope_ref, q_ref, k_ref, v_ref, lq1_ref, lk1_ref, lq2_ref, lk2_ref, sg_ref, o_ref,
                     k1_ref, k2_ref, vt_ref, acc_ref, sa_ref, sb_ref, pa_ref, pb_ref, *, blk, nblk, lambda_init):
    h = pl.program_id(1)
    qi = pl.program_id(2)
    lane = lax.broadcasted_iota(jnp.int32, (blk, HEAD_DIM), 1)

    @pl.when(qi == 0)
    def _():
        slope2 = slope_ref[h] * LOG2E

        def prep(j, carry):
            start = pl.multiple_of(j * blk, blk)
            kb = k_ref[pl.ds(start, blk), :].astype(F32)
            kpos = (lax.broadcasted_iota(jnp.int32, (blk, HEAD_DIM), 0) + start).astype(F32)
            bias = kpos * slope2
            hi = bias.astype(BF16).astype(F32)
            r1 = bias - hi
            mid = r1.astype(BF16).astype(F32)
            lo = r1 - mid
            zero = jnp.zeros_like(kb)
            k1 = jnp.where(lane < 64, kb,
                           jnp.where(lane == 64, hi, jnp.where(lane == 65, mid, jnp.where(lane == 66, lo, zero))))
            k2 = jnp.where(lane >= 64, kb,
                           jnp.where(lane == 0, hi, jnp.where(lane == 1, mid, jnp.where(lane == 2, lo, zero))))
            k1_ref[j] = k1.astype(BF16)
            k2_ref[j] = k2.astype(BF16)
            vt_ref[j, 0:HEAD_DIM, :] = v_ref[pl.ds(start, blk), :].astype(F32).T.astype(BF16)
            vt_ref[j, HEAD_DIM:, :] = jnp.ones((SUM_ROWS, blk), BF16)
            return carry

        lax.fori_loop(0, nblk, prep, 0)

    qs = q_ref[...].astype(F32) * (DIFF_MAP_DIM ** -0.5 * LOG2E)
    one = jnp.ones_like(qs)
    zero = jnp.zeros_like(qs)
    qa = (jnp.where(lane < 64, qs, jnp.where(lane < 67, one, zero)).astype(BF16),
          jnp.where(lane >= 64, qs, jnp.where(lane < 3, one, zero)).astype(BF16))
    ka = (k1_ref, k2_ref)

    def scores(mp, j):
        return _dot_nt(ka[mp][j], qa[mp])

    def pv_update(mp, j, p_ref, alpha):
        acc_ref[mp] = alpha * acc_ref[mp] + _dot(vt_ref[j], p_ref[mp])

    def stage(pj, nxt, s_in, s_out, p_in, p_out, carry):
        out = []
        for mp in range(2):
            m, alpha, mb = carry[3 * mp:3 * mp + 3]
            mn = jnp.maximum(m, mb)
            mbs = []
            for c in range(blk // QCHUNK):
                cs = slice(c * QCHUNK, (c + 1) * QCHUNK)
                acc_ref[mp, :, cs] = alpha[:, cs] * acc_ref[mp, :, cs] + _dot(vt_ref[pj], p_in[mp, :, cs])
                p_out[mp, :, cs] = jnp.exp2(s_in[mp, :, cs] - mn[:, cs]).astype(BF16)
                if nxt is not None:
                    s = _dot_nt(ka[mp][nxt], qa[mp][cs, :])
                    s_out[mp, :, cs] = s
                    mbs.append(jnp.max(s, axis=0, keepdims=True))
            alpha = jnp.exp2(m - mn)
            if nxt is not None:
                mb = jnp.concatenate(mbs, axis=1)
            out += [mn, alpha, mb]
        return tuple(out)

    krow = lax.broadcasted_iota(jnp.int32, (blk, blk), 0)
    qcol = lax.broadcasted_iota(jnp.int32, (blk, blk), 1)
    causal = krow <= qcol
    carry = []
    for mp in range(2):
        s = jnp.where(causal, scores(mp, qi), NEG_BIG)
        m = jnp.max(s, axis=0, keepdims=True)
        pa_ref[mp] = jnp.exp2(s - m).astype(BF16)
        acc_ref[mp] = jnp.zeros((HEAD_DIM + SUM_ROWS, blk), F32)
        s = scores(mp, 0)
        sa_ref[mp] = s
        carry += [m, jnp.ones_like(m), jnp.max(s, axis=0, keepdims=True)]
    carry = tuple(carry)

    def pair(u, carry):
        t = 2 * u
        carry = stage(jnp.where(t == 0, qi, t - 1), t + 1, sa_ref, sb_ref, pa_ref, pb_ref, carry)
        return stage(t, jnp.minimum(t + 2, qi - 1), sb_ref, sa_ref, pb_ref, pa_ref, carry)

    carry = lax.fori_loop(0, qi // 2, pair, carry)
    last = jnp.maximum(qi - 1, 0)

    def odd_tail(carry):
        t = qi - 1
        carry = stage(jnp.where(t == 0, qi, t - 1), None, sa_ref, sb_ref, pa_ref, pb_ref, carry)
        for mp in range(2):
            pv_update(mp, last, pb_ref, carry[3 * mp + 1])
        return carry

    def even_tail(carry):
        for mp in range(2):
            pv_update(mp, last, pa_ref, carry[3 * mp + 1])
        return carry

    lax.cond(qi % 2 == 1, odd_tail, even_tail, carry)

    lam = (jnp.exp(jnp.sum(lq1_ref[...] * lk1_ref[...], axis=-1, keepdims=True))
           - jnp.exp(jnp.sum(lq2_ref[...] * lk2_ref[...], axis=-1, keepdims=True)) + lambda_init)
    l1 = acc_ref[0, HEAD_DIM:HEAD_DIM + 1, :]
    l2 = acc_ref[1, HEAD_DIM:HEAD_DIM + 1, :]
    o_t = acc_ref[0, 0:HEAD_DIM, :] / l1 - lam * (acc_ref[1, 0:HEAD_DIM, :] / l2)
    o = o_t.T
    o = _rms(o, sg_ref[0]) * (1.0 - lambda_init)
    o_ref[...] = o.astype(o_ref.dtype)


def _diffattn(proj, lq1, lk1, lq2, lk2, subln_g, lambda_init, batch, seq, blk=512):
    m = proj.shape[0]
    nblk = seq // blk
    slopes = (2.0 ** (-8.0 * np.arange(1, DIFF_HEADS + 1, dtype=np.float32) / DIFF_HEADS)).astype(np.float32)
    vec = pl.BlockSpec((1, DIFF_MAP_DIM), lambda b, h, i: (0, 0))
    return pl.pallas_call(
        functools.partial(_diffattn_kernel, blk=blk, nblk=nblk, lambda_init=lambda_init),
        grid=(batch, DIFF_HEADS, nblk),
        in_specs=[pl.BlockSpec(memory_space=pltpu.SMEM),
                  pl.BlockSpec((blk, HEAD_DIM), lambda b, h, i: (b * nblk + i, COL_DQ + h)),
                  pl.BlockSpec((seq, HEAD_DIM), lambda b, h, i: (b, COL_DK + h)),
                  pl.BlockSpec((seq, HEAD_DIM), lambda b, h, i: (b, COL_DV + h)),
                  vec, vec, vec, vec,
                  pl.BlockSpec((1, 1, HEAD_DIM), lambda b, h, i: (h, 0, 0))],
        out_specs=pl.BlockSpec((blk, HEAD_DIM), lambda b, h, i: (b * nblk + i, h)),
        out_shape=jax.ShapeDtypeStruct((m, DIFF_W), BF16),
        scratch_shapes=[pltpu.VMEM((nblk, blk, HEAD_DIM), BF16),
                        pltpu.VMEM((nblk, blk, HEAD_DIM), BF16),
                        pltpu.VMEM((nblk, HEAD_DIM + SUM_ROWS, blk), BF16),
                        pltpu.VMEM((2, HEAD_DIM + SUM_ROWS, blk), F32),
                        pltpu.VMEM((2, blk, blk), F32),
                        pltpu.VMEM((2, blk, blk), F32),
                        pltpu.VMEM((2, blk, blk), BF16),
                        pltpu.VMEM((2, blk, blk), BF16)],
        compiler_params=_params("arbitrary", "arbitrary", "arbitrary"),
        name="diffattn",
    )(jnp.asarray(slopes), proj, proj, proj,
      lq1.reshape(1, -1), lk1.reshape(1, -1), lq2.reshape(1, -1), lk2.reshape(1, -1),
      subln_g.reshape(DIFF_HEADS, 1, HEAD_DIM))


def _sgu_kernel(u_ref, v_ref, lng_ref, lnb_ref, w_ref, b_ref, o_ref, *, tb):
    row = lax.broadcasted_iota(jnp.int32, (CHUNK, CHUNK), 0)
    col = lax.broadcasted_iota(jnp.int32, (CHUNK, CHUNK), 1)
    w = jnp.where(row >= col, w_ref[0], 0.0).astype(BF16)
    bias = b_ref[0]
    lng = lng_ref[0]
    lnb = lnb_ref[0]
    for c in range(tb // CHUNK):
        sl = slice(c * CHUNK, (c + 1) * CHUNK)
        u = jax.nn.gelu(u_ref[sl, :].astype(F32))
        v = jax.nn.gelu(v_ref[sl, :].astype(F32))
        mu = jnp.mean(v, axis=-1, keepdims=True)
        vc = v - mu
        var = jnp.mean(vc * vc, axis=-1, keepdims=True)
        vn = vc * lax.rsqrt(var + EPS) * lng + lnb
        mixed = _dot(w, vn.astype(BF16)) + bias
        o_ref[sl, :] = (u * mixed).astype(o_ref.dtype)


def _sgu(proj, ln_g, ln_b, w_s, b_s, tb=1024):
    m = proj.shape[0]
    g = SGU_GROUPS
    per_group = lambda i, j: (j, 0, 0)
    return pl.pallas_call(
        functools.partial(_sgu_kernel, tb=tb),
        grid=(m // tb, g),
        in_specs=[pl.BlockSpec((tb, HEAD_DIM), lambda i, j: (i, COL_SU + j)),
                  pl.BlockSpec((tb, HEAD_DIM), lambda i, j: (i, COL_SV + j)),
                  pl.BlockSpec((1, 1, HEAD_DIM), per_group),
                  pl.BlockSpec((1, 1, HEAD_DIM), per_group),
                  pl.BlockSpec((1, CHUNK, CHUNK), per_group),
                  pl.BlockSpec((1, CHUNK, 1), per_group)],
        out_specs=pl.BlockSpec((tb, HEAD_DIM), lambda i, j: (i, j)),
        out_shape=jax.ShapeDtypeStruct((m, SGU_W), BF16),
        compiler_params=_params("parallel", "arbitrary"),
        name="sgu",
    )(proj, proj, ln_g.reshape(g, 1, HEAD_DIM), ln_b.reshape(g, 1, HEAD_DIM), w_s, b_s.reshape(g, CHUNK, 1))


def _outproj_kernel(yr_ref, yd_ref, ys_ref, w_ref, x_ref, gpost_ref, gnext_ref, xo_ref, ho_ref):
    mix = (_dot(yr_ref[...], w_ref[0:RET_W, :])
           + _dot(yd_ref[...], w_ref[RET_W:RET_W + DIFF_W, :])
           + _dot(ys_ref[...], w_ref[RET_W + DIFF_W:D_MODEL, :]))
    xn = x_ref[...] + _rms(mix, gpost_ref[...])
    xo_ref[...] = xn
    ho_ref[...] = _rms(xn, gnext_ref[...]).astype(ho_ref.dtype)


def _outproj(y_ret, y_diff, y_sgu, w, x, g_post, g_next, tm=512):
    m, d = x.shape
    row = lambda i: (i, 0)
    const = lambda i: (0, 0)
    return pl.pallas_call(
        _outproj_kernel,
        grid=(m // tm,),
        in_specs=[pl.BlockSpec((tm, RET_W), row), pl.BlockSpec((tm, DIFF_W), row), pl.BlockSpec((tm, SGU_W), row),
                  pl.BlockSpec((d, d), const), pl.BlockSpec((tm, d), row),
                  pl.BlockSpec((1, d), const), pl.BlockSpec((1, d), const)],
        out_specs=[pl.BlockSpec((tm, d), row), pl.BlockSpec((tm, d), row)],
        out_shape=[jax.ShapeDtypeStruct((m, d), F32), jax.ShapeDtypeStruct((m, d), BF16)],
        compiler_params=_params("parallel"),
        name="outproj",
    )(y_ret, y_diff, y_sgu, w, x, g_post.reshape(1, d), g_next.reshape(1, d))


def _ffn_kernel(h_ref, x_ref, wg_ref, wu_ref, wd_ref, gpost_ref, gnext_ref, xo_ref, ho_ref, acc_ref, act_ref):
    j = pl.program_id(1)
    nh = pl.num_programs(1) - 1

    def activations(slot):
        h = h_ref[...]
        gate = _dot(h, wg_ref[...])
        up = _dot(h, wu_ref[...])
        act_ref[slot] = (gate * jax.nn.sigmoid(gate) * up).astype(BF16)

    @pl.when(j == 0)
    def _():
        acc_ref[...] = jnp.zeros_like(acc_ref)
        activations(0)

    for parity in range(2):
        @pl.when((j > 0) & (j < nh) & (j % 2 == parity))
        def _():
            acc_ref[...] += _dot(act_ref[1 - parity], wd_ref[...])
            activations(parity)

    @pl.when(j == nh)
    def _():
        f = acc_ref[...] + _dot(act_ref[(FFN_TILES - 1) % 2], wd_ref[...])
        xn = x_ref[...] + _rms(f, gpost_ref[...])
        xo_ref[...] = xn
        ho_ref[...] = _rms(xn, gnext_ref[...]).astype(ho_ref.dtype)


def _ffn(h, x, wg, wu, wd, g_post, g_next, tm=512):
    m, d = x.shape
    th = wg.shape[1] // FFN_TILES
    row = lambda i, j: (i, 0)
    const = lambda i, j: (0, 0)
    up_tile = lambda i, j: (0, jnp.minimum(j, FFN_TILES - 1))
    down_tile = lambda i, j: (jnp.maximum(j - 1, 0), 0)
    return pl.pallas_call(
        _ffn_kernel,
        grid=(m // tm, FFN_TILES + 1),
        in_specs=[pl.BlockSpec((tm, d), row), pl.BlockSpec((tm, d), row),
                  pl.BlockSpec((d, th), up_tile), pl.BlockSpec((d, th), up_tile),
                  pl.BlockSpec((th, d), down_tile),
                  pl.BlockSpec((1, d), const), pl.BlockSpec((1, d), const)],
        out_specs=[pl.BlockSpec((tm, d), row), pl.BlockSpec((tm, d), row)],
        out_shape=[jax.ShapeDtypeStruct((m, d), F32), jax.ShapeDtypeStruct((m, d), BF16)],
        scratch_shapes=[pltpu.VMEM((tm, d), F32), pltpu.VMEM((2, tm, th), BF16)],
        compiler_params=_params("parallel", "arbitrary"),
        name="ffn",
    )(h, x, wg, wu, wd, g_post.reshape(1, d), g_next.reshape(1, d))


def kernel(x, pre_mix_g, w_in, ret_gn_g, diff_lam_q1, diff_lam_k1, diff_lam_q2, diff_lam_k2, diff_subln_g,
           sgu_ln_g, sgu_ln_b, sgu_w, sgu_b, w_out, post_mix_g, pre_ffn_g, w_gate, w_up, w_down, post_ffn_g):
    batch, seq, d = x.shape
    depth = w_in.shape[0]
    xf = x.reshape(batch * seq, d)
    h = _rmsnorm(xf, pre_mix_g[0])
    for l in range(depth):
        lambda_init = 0.8 - 0.6 * math.exp(-0.3 * l)
        proj = _inproj(h, w_in[l].astype(BF16))
        y_ret = _retention(proj, ret_gn_g[l], batch, seq)
        y_diff = _diffattn(proj, diff_lam_q1[l], diff_lam_k1[l], diff_lam_q2[l], diff_lam_k2[l],
                           diff_subln_g[l], lambda_init, batch, seq)
        y_sgu = _sgu(proj, sgu_ln_g[l], sgu_ln_b[l], sgu_w[l], sgu_b[l])
        xf, h = _outproj(y_ret, y_diff, y_sgu, w_out[l].astype(BF16), xf, post_mix_g[l], pre_ffn_g[l])
        g_next = pre_mix_g[(l + 1) % depth]
        xf, h = _ffn(h, xf, w_gate[l].astype(BF16), w_up[l].astype(BF16), w_down[l].astype(BF16),
                     post_ffn_g[l], g_next)
    return xf.reshape(batch, seq, d)
```

```python
import functools
import math

import numpy as np
import jax
import jax.numpy as jnp
from jax import lax
from jax.experimental import pallas as pl
from jax.experimental.pallas import tpu as pltpu

D_MODEL = 2048
HEAD_DIM = 128
RET_HEADS = 6
DIFF_HEADS = 6
SGU_GROUPS = 4
DIFF_MAP_DIM = 64
RET_W = RET_HEADS * HEAD_DIM
DIFF_W = DIFF_HEADS * HEAD_DIM
SGU_W = SGU_GROUPS * HEAD_DIM
IN_W = 4 * RET_W + 3 * DIFF_W + 2 * SGU_W
CHUNK = 128
EPS = 1e-6
LOG2E = math.log2(math.e)
NEG_BIG = -1e30
SUM_ROWS = 16
FFN_TILES = 11
QCHUNK = 256
RING = 2

COL_RQ, COL_RK, COL_RV, COL_RG = 0, 6, 12, 18
COL_DQ, COL_DK, COL_DV = 24, 30, 36
COL_SU, COL_SV = 42, 46

V7X_VMEM_BYTES = 64 * 1024 * 1024
VMEM_LIMIT = 56 * 1024 * 1024

BF16 = jnp.bfloat16
F32 = jnp.float32


def _dot(a, b):
    return jnp.dot(a, b, preferred_element_type=F32)


def _dot_nt(a, b):
    return lax.dot_general(a, b, (((1,), (1,)), ((), ())), preferred_element_type=F32)


def _dot_tn(a, b):
    return lax.dot_general(a, b, (((0,), (0,)), ((), ())), preferred_element_type=F32)


def _rms(x, g):
    return x * lax.rsqrt(jnp.mean(x * x, axis=-1, keepdims=True) + EPS) * g


def _params(*sem):
    return pltpu.CompilerParams(dimension_semantics=sem, vmem_limit_bytes=VMEM_LIMIT)


def _rmsnorm_kernel(x_ref, g_ref, o_ref):
    o_ref[...] = _rms(x_ref[...], g_ref[...]).astype(o_ref.dtype)


def _rmsnorm(x, g, tm=512):
    m, d = x.shape
    return pl.pallas_call(
        _rmsnorm_kernel,
        grid=(m // tm,),
        in_specs=[pl.BlockSpec((tm, d), lambda i: (i, 0)), pl.BlockSpec((1, d), lambda i: (0, 0))],
        out_specs=pl.BlockSpec((tm, d), lambda i: (i, 0)),
        out_shape=jax.ShapeDtypeStruct((m, d), BF16),
        compiler_params=_params("parallel"),
        name="rmsnorm",
    )(x, g.reshape(1, d))


def _matmul_kernel(h_ref, w_ref, o_ref):
    o_ref[...] = _dot(h_ref[...], w_ref[0]).astype(o_ref.dtype)


def _inproj(h, w, tm=1024, tn=1280):
    m, k = h.shape
    n = w.shape[1]
    wt = w.astype(BF16).reshape(k, n // tn, tn).transpose(1, 0, 2)
    return pl.pallas_call(
        _matmul_kernel,
        grid=(m // tm, n // tn),
        in_specs=[pl.BlockSpec((tm, k), lambda i, j: (i, 0)), pl.BlockSpec((1, k, tn), lambda i, j: (j, 0, 0))],
        out_specs=pl.BlockSpec((tm, tn), lambda i, j: (i, j)),
        out_shape=jax.ShapeDtypeStruct((m, n), BF16),
        compiler_params=_params("parallel", "arbitrary"),
        name="inproj",
    )(h, wt)


def _retention_kernel(lg_ref, q_ref, k_ref, v_ref, g_ref, gn_ref, o_ref, r_ref, *, tb):
    h = pl.program_id(1)
    t = pl.program_id(2)

    @pl.when(t == 0)
    def _():
        r_ref[...] = jnp.zeros_like(r_ref)

    lg = lg_ref[h]
    scale = HEAD_DIM ** -0.5
    row = lax.broadcasted_iota(jnp.int32, (CHUNK, CHUNK), 0)
    col = lax.broadcasted_iota(jnp.int32, (CHUNK, CHUNK), 1)
    rel = (row - col).astype(F32)
    decay = jnp.where(rel >= 0, jnp.exp(lg * jnp.maximum(rel, 0.0)), 0.0) * scale
    pos = lax.broadcasted_iota(jnp.int32, (CHUNK, 1), 0).astype(F32)
    zeta = jnp.exp(lg * (CHUNK - 1.0 - pos)) * scale
    xi = jnp.exp(lg * (pos + 1.0))
    chunk_decay = jnp.exp(jnp.full((1, 1), lg * CHUNK, F32))
    gn = gn_ref[0]

    for c in range(tb // CHUNK):
        sl = slice(c * CHUNK, (c + 1) * CHUNK)
        q = q_ref[sl, :]
        k = k_ref[sl, :]
        v = v_ref[sl, :]
        scores = _dot_nt(q, k) * decay
        inner = _dot(scores.astype(BF16), v)
        r_prev = r_ref[...]
        cross = _dot(q, r_prev.astype(BF16)) * xi
        kz = (k.astype(F32) * zeta).astype(BF16)
        r_ref[...] = _dot_tn(kz, v) + chunk_decay * r_prev
        y = inner + cross
        mu = jnp.mean(y, axis=-1, keepdims=True)
        yc = y - mu
        var = jnp.mean(yc * yc, axis=-1, keepdims=True)
        y = yc * lax.rsqrt(var + EPS) * gn
        gate = g_ref[sl, :].astype(F32)
        o_ref[sl, :] = (gate * jax.nn.sigmoid(gate) * y).astype(o_ref.dtype)


def _retention(proj, gn_gain, batch, seq, tb=1024):
    m = proj.shape[0]
    nt = seq // tb
    log_gamma = np.log1p(-(2.0 ** (-5.0 - np.arange(RET_HEADS, dtype=np.float32)))).astype(np.float32)

    def blk(col):
        return pl.BlockSpec((tb, HEAD_DIM), lambda b, h, t: (b * nt + t, col + h))

    return pl.pallas_call(
        functools.partial(_retention_kernel, tb=tb),
        grid=(batch, RET_HEADS, nt),
        in_specs=[pl.BlockSpec(memory_space=pltpu.SMEM),
                  blk(COL_RQ), blk(COL_RK), blk(COL_RV), blk(COL_RG),
                  pl.BlockSpec((1, 1, HEAD_DIM), lambda b, h, t: (h, 0, 0))],
        out_specs=blk(0),
        out_shape=jax.ShapeDtypeStruct((m, RET_W), BF16),
        scratch_shapes=[pltpu.VMEM((HEAD_DIM, HEAD_DIM), F32)],
        compiler_params=_params("arbitrary", "arbitrary", "arbitrary"),
        name="retention",
    )(jnp.asarray(log_gamma), proj, proj, proj, proj, gn_gain.reshape(RET_HEADS, 1, HEAD_DIM))


def _diffattn_kernel(slope_ref, q_ref, k_ref, v_ref, lq1_ref, lk1_ref, lq2_ref, lk2_ref, sg_ref, o_ref,
                     k1_ref, k2_ref, vt_ref, acc_ref, s_ref, p_ref, *, blk, nblk, lambda_init):
    h = pl.program_id(1)
    qi = pl.program_id(2)
    lane = lax.broadcasted_iota(jnp.int32, (blk, HEAD_DIM), 1)

    @pl.when(qi == 0)
    def _():
        slope2 = slope_ref[h] * LOG2E

        def prep(j, carry):
            start = pl.multiple_of(j * blk, blk)
            kb = k_ref[pl.ds(start, blk), :].astype(F32)
            kpos = (lax.broadcasted_iota(jnp.int32, (blk, HEAD_DIM), 0) + start).astype(F32)
            bias = kpos * slope2
            hi = bias.astype(BF16).astype(F32)
            r1 = bias - hi
            mid = r1.astype(BF16).astype(F32)
            lo = r1 - mid
            zero = jnp.zeros_like(kb)
            k1 = jnp.where(lane < 64, kb,
                           jnp.where(lane == 64, hi, jnp.where(lane == 65, mid, jnp.where(lane == 66, lo, zero))))
            k2 = jnp.where(lane >= 64, kb,
                           jnp.where(lane == 0, hi, jnp.where(lane == 1, mid, jnp.where(lane == 2, lo, zero))))
            k1_ref[j] = k1.astype(BF16)
            k2_ref[j] = k2.astype(BF16)
            vt_ref[j, 0:HEAD_DIM, :] = v_ref[pl.ds(start, blk), :].astype(F32).T.astype(BF16)
            vt_ref[j, HEAD_DIM:, :] = jnp.ones((SUM_ROWS, blk), BF16)
            return carry

        lax.fori_loop(0, nblk, prep, 0)

    qs = q_ref[...].astype(F32) * (DIFF_MAP_DIM ** -0.5 * LOG2E)
    one = jnp.ones_like(qs)
    zero = jnp.zeros_like(qs)
    qa = (jnp.where(lane < 64, qs, jnp.where(lane < 67, one, zero)).astype(BF16),
          jnp.where(lane >= 64, qs, jnp.where(lane < 3, one, zero)).astype(BF16))
    ka = (k1_ref, k2_ref)

    def scores(mp, j):
        return _dot_nt(ka[mp][j], qa[mp])

    def pv_update(mp, j, p_ref, alpha):
        acc_ref[mp] = alpha * acc_ref[mp] + _dot(vt_ref[j], p_ref[mp])

    def stage(slot, t, nxt, carry):
        pj = jnp.where(t == 0, qi, t - 1)
        s_in, p_in = s_ref.at[slot], p_ref.at[slot]
        s_out, p_out = s_ref.at[(slot + 1) % RING], p_ref.at[(slot + 1) % RING]
        out = []
        for mp in range(2):
            m, alpha, mb = carry[3 * mp:3 * mp + 3]
            mn = jnp.maximum(m, mb)
            mbs = []
            for c in range(blk // QCHUNK):
                cs = slice(c * QCHUNK, (c + 1) * QCHUNK)
                acc_ref[mp, :, cs] = alpha[:, cs] * acc_ref[mp, :, cs] + _dot(vt_ref[pj], p_in[mp, :, cs])
                p_out[mp, :, cs] = jnp.exp2(s_in[mp, :, cs] - mn[:, cs]).astype(BF16)
                if nxt is not None:
                    s = _dot_nt(ka[mp][nxt], qa[mp][cs, :])
                    s_out[mp, :, cs] = s
                    mbs.append(jnp.max(s, axis=0, keepdims=True))
            alpha = jnp.exp2(m - mn)
            if nxt is not None:
                mb = jnp.concatenate(mbs, axis=1)
            out += [mn, alpha, mb]
        return tuple(out)

    krow = lax.broadcasted_iota(jnp.int32, (blk, blk), 0)
    qcol = lax.broadcasted_iota(jnp.int32, (blk, blk), 1)
    causal = krow <= qcol
    carry = []
    for mp in range(2):
        s = jnp.where(causal, scores(mp, qi), NEG_BIG)
        m = jnp.max(s, axis=0, keepdims=True)
        p_ref[0, mp] = jnp.exp2(s - m).astype(BF16)
        acc_ref[mp] = jnp.zeros((HEAD_DIM + SUM_ROWS, blk), F32)
        s = scores(mp, 0)
        s_ref[0, mp] = s
        carry += [m, jnp.ones_like(m), jnp.max(s, axis=0, keepdims=True)]
    carry = tuple(carry)

    def unrolled(u, carry):
        t = RING * u
        for k in range(RING - 1):
            carry = stage(k, t + k, t + k + 1, carry)
        return stage(RING - 1, t + RING - 1, jnp.minimum(t + RING, qi - 1), carry)

    carry = lax.fori_loop(0, qi // RING, unrolled, carry)

    def make_tail(rem):
        def tail(carry):
            t = qi - rem
            for k in range(rem):
                carry = stage(k, t + k, t + k + 1 if k < rem - 1 else None, carry)
            for mp in range(2):
                pv_update(mp, jnp.maximum(qi - 1, 0), p_ref.at[rem], carry[3 * mp + 1])
            return carry
        return tail

    lax.switch(qi % RING, [make_tail(rem) for rem in range(RING)], carry)

    lam = (jnp.exp(jnp.sum(lq1_ref[...] * lk1_ref[...], axis=-1, keepdims=True))
           - jnp.exp(jnp.sum(lq2_ref[...] * lk2_ref[...], axis=-1, keepdims=True)) + lambda_init)
    l1 = acc_ref[0, HEAD_DIM:HEAD_DIM + 1, :]
    l2 = acc_ref[1, HEAD_DIM:HEAD_DIM + 1, :]
    o_t = acc_ref[0, 0:HEAD_DIM, :] / l1 - lam * (acc_ref[1, 0:HEAD_DIM, :] / l2)
    o = o_t.T
    o = _rms(o, sg_ref[0]) * (1.0 - lambda_init)
    o_ref[...] = o.astype(o_ref.dtype)


def _diffattn(proj, lq1, lk1, lq2, lk2, subln_g, lambda_init, batch, seq, blk=512):
    m = proj.shape[0]
    nblk = seq // blk
    slopes = (2.0 ** (-8.0 * np.arange(1, DIFF_HEADS + 1, dtype=np.float32) / DIFF_HEADS)).astype(np.float32)
    vec = pl.BlockSpec((1, DIFF_MAP_DIM), lambda b, h, i: (0, 0))
    return pl.pallas_call(
        functools.partial(_diffattn_kernel, blk=blk, nblk=nblk, lambda_init=lambda_init),
        grid=(batch, DIFF_HEADS, nblk),
        in_specs=[pl.BlockSpec(memory_space=pltpu.SMEM),
                  pl.BlockSpec((blk, HEAD_DIM), lambda b, h, i: (b * nblk + i, COL_DQ + h)),
                  pl.BlockSpec((seq, HEAD_DIM), lambda b, h, i: (b, COL_DK + h)),
                  pl.BlockSpec((seq, HEAD_DIM), lambda b, h, i: (b, COL_DV + h)),
                  vec, vec, vec, vec,
                  pl.BlockSpec((1, 1, HEAD_DIM), lambda b, h, i: (h, 0, 0))],
        out_specs=pl.BlockSpec((blk, HEAD_DIM), lambda b, h, i: (b * nblk + i, h)),
        out_shape=jax.ShapeDtypeStruct((m, DIFF_W), BF16),
        scratch_shapes=[pltpu.VMEM((nblk, blk, HEAD_DIM), BF16),
                        pltpu.VMEM((nblk, blk, HEAD_DIM), BF16),
                        pltpu.VMEM((nblk, HEAD_DIM + SUM_ROWS, blk), BF16),
                        pltpu.VMEM((2, HEAD_DIM + SUM_ROWS, blk), F32),
                        pltpu.VMEM((RING, 2, blk, blk), F32),
                        pltpu.VMEM((RING, 2, blk, blk), BF16)],
        compiler_params=_params("arbitrary", "arbitrary", "arbitrary"),
        name="diffattn",
    )(jnp.asarray(slopes), proj, proj, proj,
      lq1.reshape(1, -1), lk1.reshape(1, -1), lq2.reshape(1, -1), lk2.reshape(1, -1),
      subln_g.reshape(DIFF_HEADS, 1, HEAD_DIM))


def _sgu_kernel(u_ref, v_ref, lng_ref, lnb_ref, w_ref, b_ref, o_ref, *, tb):
    row = lax.broadcasted_iota(jnp.int32, (CHUNK, CHUNK), 0)
    col = lax.broadcasted_iota(jnp.int32, (CHUNK, CHUNK), 1)
    w = jnp.where(row >= col, w_ref[0], 0.0).astype(BF16)
    bias = b_ref[0]
    lng = lng_ref[0]
    lnb = lnb_ref[0]
    for c in range(tb // CHUNK):
        sl = slice(c * CHUNK, (c + 1) * CHUNK)
        u = jax.nn.gelu(u_ref[sl, :].astype(F32))
        v = jax.nn.gelu(v_ref[sl, :].astype(F32))
        mu = jnp.mean(v, axis=-1, keepdims=True)
        vc = v - mu
        var = jnp.mean(vc * vc, axis=-1, keepdims=True)
        vn = vc * lax.rsqrt(var + EPS) * lng + lnb
        mixed = _dot(w, vn.astype(BF16)) + bias
        o_ref[sl, :] = (u * mixed).astype(o_ref.dtype)


def _sgu(proj, ln_g, ln_b, w_s, b_s, tb=1024):
    m = proj.shape[0]
    g = SGU_GROUPS
    per_group = lambda i, j: (j, 0, 0)
    return pl.pallas_call(
        functools.partial(_sgu_kernel, tb=tb),
        grid=(m // tb, g),
        in_specs=[pl.BlockSpec((tb, HEAD_DIM), lambda i, j: (i, COL_SU + j)),
                  pl.BlockSpec((tb, HEAD_DIM), lambda i, j: (i, COL_SV + j)),
                  pl.BlockSpec((1, 1, HEAD_DIM), per_group),
                  pl.BlockSpec((1, 1, HEAD_DIM), per_group),
                  pl.BlockSpec((1, CHUNK, CHUNK), per_group),
                  pl.BlockSpec((1, CHUNK, 1), per_group)],
        out_specs=pl.BlockSpec((tb, HEAD_DIM), lambda i, j: (i, j)),
        out_shape=jax.ShapeDtypeStruct((m, SGU_W), BF16),
        compiler_params=_params("parallel", "arbitrary"),
        name="sgu",
    )(proj, proj, ln_g.reshape(g, 1, HEAD_DIM), ln_b.reshape(g, 1, HEAD_DIM), w_s, b_s.reshape(g, CHUNK, 1))


def _outproj_kernel(yr_ref, yd_ref, ys_ref, w_ref, x_ref, gpost_ref, gnext_ref, xo_ref, ho_ref):
    mix = (_dot(yr_ref[...], w_ref[0:RET_W, :])
           + _dot(yd_ref[...], w_ref[RET_W:RET_W + DIFF_W, :])
           + _dot(ys_ref[...], w_ref[RET_W + DIFF_W:D_MODEL, :]))
    xn = x_ref[...] + _rms(mix, gpost_ref[...])
    xo_ref[...] = xn
    ho_ref[...] = _rms(xn, gnext_ref[...]).astype(ho_ref.dtype)


def _outproj(y_ret, y_diff, y_sgu, w, x, g_post, g_next, tm=512):
    m, d = x.shape
    row = lambda i: (i, 0)
    const = lambda i: (0, 0)
    return pl.pallas_call(
        _outproj_kernel,
        grid=(m // tm,),
        in_specs=[pl.BlockSpec((tm, RET_W), row), pl.BlockSpec((tm, DIFF_W), row), pl.BlockSpec((tm, SGU_W), row),
                  pl.BlockSpec((d, d), const), pl.BlockSpec((tm, d), row),
                  pl.BlockSpec((1, d), const), pl.BlockSpec((1, d), const)],
        out_specs=[pl.BlockSpec((tm, d), row), pl.BlockSpec((tm, d), row)],
        out_shape=[jax.ShapeDtypeStruct((m, d), F32), jax.ShapeDtypeStruct((m, d), BF16)],
        compiler_params=_params("parallel"),
        name="outproj",
    )(y_ret, y_diff, y_sgu, w, x, g_post.reshape(1, d), g_next.reshape(1, d))


def _ffn_kernel(h_ref, x_ref, wgu_ref, wd_ref, gpost_ref, gnext_ref, xo_ref, ho_ref, acc_ref, act_ref):
    j = pl.program_id(1)
    nh = pl.num_programs(1) - 1
    th = act_ref.shape[2]

    def activations(slot):
        gu = _dot(h_ref[...], wgu_ref[0])
        gate = gu[:, :th]
        act_ref[slot] = (gate * jax.nn.sigmoid(gate) * gu[:, th:]).astype(BF16)

    @pl.when(j == 0)
    def _():
        acc_ref[...] = jnp.zeros_like(acc_ref)
        activations(0)

    for parity in range(2):
        @pl.when((j > 0) & (j < nh) & (j % 2 == parity))
        def _():
            acc_ref[...] += _dot(act_ref[1 - parity], wd_ref[...])
            activations(parity)

    @pl.when(j == nh)
    def _():
        f = acc_ref[...] + _dot(act_ref[(FFN_TILES - 1) % 2], wd_ref[...])
        xn = x_ref[...] + _rms(f, gpost_ref[...])
        xo_ref[...] = xn
        ho_ref[...] = _rms(xn, gnext_ref[...]).astype(ho_ref.dtype)


def _ffn_weights(wg, wu):
    d, hidden = wg.shape
    th = hidden // FFN_TILES
    tiles = lambda w: w.astype(BF16).reshape(d, FFN_TILES, th).transpose(1, 0, 2)
    return jnp.concatenate([tiles(wg), tiles(wu)], axis=2)


def _ffn(h, x, wgu, wd, g_post, g_next, tm=512):
    m, d = x.shape
    th = wgu.shape[2] // 2
    row = lambda i, j: (i, 0)
    const = lambda i, j: (0, 0)
    up_tile = lambda i, j: (jnp.minimum(j, FFN_TILES - 1), 0, 0)
    down_tile = lambda i, j: (jnp.maximum(j - 1, 0), 0)
    return pl.pallas_call(
        _ffn_kernel,
        grid=(m // tm, FFN_TILES + 1),
        in_specs=[pl.BlockSpec((tm, d), row), pl.BlockSpec((tm, d), row),
                  pl.BlockSpec((1, d, 2 * th), up_tile),
                  pl.BlockSpec((th, d), down_tile),
                  pl.BlockSpec((1, d), const), pl.BlockSpec((1, d), const)],
        out_specs=[pl.BlockSpec((tm, d), row), pl.BlockSpec((tm, d), row)],
        out_shape=[jax.ShapeDtypeStruct((m, d), F32), jax.ShapeDtypeStruct((m, d), BF16)],
        scratch_shapes=[pltpu.VMEM((tm, d), F32), pltpu.VMEM((2, tm, th), BF16)],
        compiler_params=_params("parallel", "arbitrary"),
        name="ffn",
    )(h, x, wgu, wd, g_post.reshape(1, d), g_next.reshape(1, d))


def kernel(x, pre_mix_g, w_in, ret_gn_g, diff_lam_q1, diff_lam_k1, diff_lam_q2, diff_lam_k2, diff_subln_g,
           sgu_ln_g, sgu_ln_b, sgu_w, sgu_b, w_out, post_mix_g, pre_ffn_g, w_gate, w_up, w_down, post_ffn_g):
    batch, seq, d = x.shape
    depth = w_in.shape[0]
    xf = x.reshape(batch * seq, d)
    h = _rmsnorm(xf, pre_mix_g[0])
    for l in range(depth):
        lambda_init = 0.8 - 0.6 * math.exp(-0.3 * l)
        proj = _inproj(h, w_in[l])
        y_ret = _retention(proj, ret_gn_g[l], batch, seq)
        y_diff = _diffattn(proj, diff_lam_q1[l], diff_lam_k1[l], diff_lam_q2[l], diff_lam_k2[l],
                           diff_subln_g[l], lambda_init, batch, seq)
        y_sgu = _sgu(proj, sgu_ln_g[l], sgu_ln_b[l], sgu_w[l], sgu_b[l])
        xf, h = _outproj(y_ret, y_diff, y_sgu, w_out[l].astype(BF16), xf, post_mix_g[l], pre_ffn_g[l])
        g_next = pre_mix_g[(l + 1) % depth]
        xf, h = _ffn(h, xf, _ffn_weights(w_gate[l], w_up[l]), w_down[l].astype(BF16), post_ffn_g[l], g_next)
    return xf.reshape(batch, seq, d)
```

```python
import functools
import math

import numpy as np
import jax
import jax.numpy as jnp
from jax import lax
from jax.experimental import pallas as pl
from jax.experimental.pallas import tpu as pltpu

D_MODEL = 2048
HEAD_DIM = 128
RET_HEADS = 6
DIFF_HEADS = 6
SGU_GROUPS = 4
DIFF_MAP_DIM = 64
RET_W = RET_HEADS * HEAD_DIM
DIFF_W = DIFF_HEADS * HEAD_DIM
SGU_W = SGU_GROUPS * HEAD_DIM
IN_W = 4 * RET_W + 3 * DIFF_W + 2 * SGU_W
CHUNK = 128
EPS = 1e-6
LOG2E = math.log2(math.e)
NEG_BIG = -1e30
SUM_ROWS = 16
FFN_TILES = 11
QCHUNK = 256
RING = 2

COL_RQ, COL_RK, COL_RV, COL_RG = 0, 6, 12, 18
COL_DQ, COL_DK, COL_DV = 24, 30, 36
COL_SU, COL_SV = 42, 46

V7X_VMEM_BYTES = 64 * 1024 * 1024
VMEM_LIMIT = 56 * 1024 * 1024
VMEM_LIMIT_FFN = 60 * 1024 * 1024

BF16 = jnp.bfloat16
F32 = jnp.float32


def _dot(a, b):
    return jnp.dot(a, b, preferred_element_type=F32)


def _dot_nt(a, b):
    return lax.dot_general(a, b, (((1,), (1,)), ((), ())), preferred_element_type=F32)


def _dot_tn(a, b):
    return lax.dot_general(a, b, (((0,), (0,)), ((), ())), preferred_element_type=F32)


def _rms(x, g):
    return x * lax.rsqrt(jnp.mean(x * x, axis=-1, keepdims=True) + EPS) * g


def _params(*sem):
    return pltpu.CompilerParams(dimension_semantics=sem, vmem_limit_bytes=VMEM_LIMIT)


def _rmsnorm_kernel(x_ref, g_ref, o_ref):
    o_ref[...] = _rms(x_ref[...], g_ref[...]).astype(o_ref.dtype)


def _rmsnorm(x, g, tm=512):
    m, d = x.shape
    return pl.pallas_call(
        _rmsnorm_kernel,
        grid=(m // tm,),
        in_specs=[pl.BlockSpec((tm, d), lambda i: (i, 0)), pl.BlockSpec((1, d), lambda i: (0, 0))],
        out_specs=pl.BlockSpec((tm, d), lambda i: (i, 0)),
        out_shape=jax.ShapeDtypeStruct((m, d), BF16),
        compiler_params=_params("parallel"),
        name="rmsnorm",
    )(x, g.reshape(1, d))


def _matmul_kernel(h_ref, w_ref, o_ref):
    o_ref[...] = _dot(h_ref[...], w_ref[...]).astype(o_ref.dtype)


def _inproj(h, w, tm=1024, tn=1280):
    m, k = h.shape
    n = w.shape[1]
    return pl.pallas_call(
        _matmul_kernel,
        grid=(m // tm, n // tn),
        in_specs=[pl.BlockSpec((tm, k), lambda i, j: (i, 0)), pl.BlockSpec((k, tn), lambda i, j: (0, j))],
        out_specs=pl.BlockSpec((tm, tn), lambda i, j: (i, j)),
        out_shape=jax.ShapeDtypeStruct((m, n), BF16),
        compiler_params=_params("parallel", "arbitrary"),
        name="inproj",
    )(h, w)


def _retention_kernel(lg_ref, q_ref, k_ref, v_ref, g_ref, gn_ref, o_ref, r_ref, *, tb):
    h = pl.program_id(1)
    t = pl.program_id(2)

    @pl.when(t == 0)
    def _():
        r_ref[...] = jnp.zeros_like(r_ref)

    lg = lg_ref[h]
    scale = HEAD_DIM ** -0.5
    row = lax.broadcasted_iota(jnp.int32, (CHUNK, CHUNK), 0)
    col = lax.broadcasted_iota(jnp.int32, (CHUNK, CHUNK), 1)
    rel = (row - col).astype(F32)
    decay = jnp.where(rel >= 0, jnp.exp(lg * jnp.maximum(rel, 0.0)), 0.0) * scale
    pos = lax.broadcasted_iota(jnp.int32, (CHUNK, 1), 0).astype(F32)
    zeta = jnp.exp(lg * (CHUNK - 1.0 - pos)) * scale
    xi = jnp.exp(lg * (pos + 1.0))
    chunk_decay = jnp.exp(jnp.full((1, 1), lg * CHUNK, F32))
    gn = gn_ref[0]

    for c in range(tb // CHUNK):
        sl = slice(c * CHUNK, (c + 1) * CHUNK)
        q = q_ref[sl, :]
        k = k_ref[sl, :]
        v = v_ref[sl, :]
        scores = _dot_nt(q, k) * decay
        inner = _dot(scores.astype(BF16), v)
        r_prev = r_ref[...]
        cross = _dot(q, r_prev.astype(BF16)) * xi
        kz = (k.astype(F32) * zeta).astype(BF16)
        r_ref[...] = _dot_tn(kz, v) + chunk_decay * r_prev
        y = inner + cross
        mu = jnp.mean(y, axis=-1, keepdims=True)
        yc = y - mu
        var = jnp.mean(yc * yc, axis=-1, keepdims=True)
        y = yc * lax.rsqrt(var + EPS) * gn
        gate = g_ref[sl, :].astype(F32)
        o_ref[sl, :] = (gate * jax.nn.sigmoid(gate) * y).astype(o_ref.dtype)


def _retention(proj, gn_gain, batch, seq, tb=1024):
    m = proj.shape[0]
    nt = seq // tb
    log_gamma = np.log1p(-(2.0 ** (-5.0 - np.arange(RET_HEADS, dtype=np.float32)))).astype(np.float32)

    def blk(col):
        return pl.BlockSpec((tb, HEAD_DIM), lambda b, h, t: (b * nt + t, col + h))

    return pl.pallas_call(
        functools.partial(_retention_kernel, tb=tb),
        grid=(batch, RET_HEADS, nt),
        in_specs=[pl.BlockSpec(memory_space=pltpu.SMEM),
                  blk(COL_RQ), blk(COL_RK), blk(COL_RV), blk(COL_RG),
                  pl.BlockSpec((1, 1, HEAD_DIM), lambda b, h, t: (h, 0, 0))],
        out_specs=blk(0),
        out_shape=jax.ShapeDtypeStruct((m, RET_W), BF16),
        scratch_shapes=[pltpu.VMEM((HEAD_DIM, HEAD_DIM), F32)],
        compiler_params=_params("arbitrary", "arbitrary", "arbitrary"),
        name="retention",
    )(jnp.asarray(log_gamma), proj, proj, proj, proj, gn_gain.reshape(RET_HEADS, 1, HEAD_DIM))


def _diffattn_kernel(slope_ref, q_ref, k_ref, v_ref, lq1_ref, lk1_ref, lq2_ref, lk2_ref, sg_ref, o_ref,
                     k1_ref, k2_ref, vt_ref, *bufs, blk, nblk, lambda_init):
    acc = bufs[0:2]
    s_buf = [bufs[2 + 2 * k:4 + 2 * k] for k in range(RING)]
    p_buf = [bufs[2 + 2 * RING + 2 * k:4 + 2 * RING + 2 * k] for k in range(RING)]
    h = pl.program_id(1)
    qi = pl.program_id(2)
    lane = lax.broadcasted_iota(jnp.int32, (blk, HEAD_DIM), 1)

    @pl.when(qi == 0)
    def _():
        slope2 = slope_ref[h] * LOG2E

        def prep(j, carry):
            start = pl.multiple_of(j * blk, blk)
            kb = k_ref[pl.ds(start, blk), :].astype(F32)
            kpos = (lax.broadcasted_iota(jnp.int32, (blk, HEAD_DIM), 0) + start).astype(F32)
            bias = kpos * slope2
            hi = bias.astype(BF16).astype(F32)
            r1 = bias - hi
            mid = r1.astype(BF16).astype(F32)
            lo = r1 - mid
            zero = jnp.zeros_like(kb)
            k1 = jnp.where(lane < 64, kb,
                           jnp.where(lane == 64, hi, jnp.where(lane == 65, mid, jnp.where(lane == 66, lo, zero))))
            k2 = jnp.where(lane >= 64, kb,
                           jnp.where(lane == 0, hi, jnp.where(lane == 1, mid, jnp.where(lane == 2, lo, zero))))
            k1_ref[j] = k1.astype(BF16)
            k2_ref[j] = k2.astype(BF16)
            vt_ref[j, 0:HEAD_DIM, :] = v_ref[pl.ds(start, blk), :].astype(F32).T.astype(BF16)
            vt_ref[j, HEAD_DIM:, :] = jnp.ones((SUM_ROWS, blk), BF16)
            return carry

        lax.fori_loop(0, nblk, prep, 0)

    qs = q_ref[...].astype(F32) * (DIFF_MAP_DIM ** -0.5 * LOG2E)
    one = jnp.ones_like(qs)
    zero = jnp.zeros_like(qs)
    qa = (jnp.where(lane < 64, qs, jnp.where(lane < 67, one, zero)).astype(BF16),
          jnp.where(lane >= 64, qs, jnp.where(lane < 3, one, zero)).astype(BF16))
    ka = (k1_ref, k2_ref)

    def scores(mp, j):
        return _dot_nt(ka[mp][j], qa[mp])

    def pv_update(mp, j, p_in, alpha):
        acc[mp][...] = alpha * acc[mp][...] + _dot(vt_ref[j], p_in[mp][...])

    def stage(slot, t, nxt, carry):
        pj = jnp.where(t == 0, qi, t - 1)
        s_in, p_in = s_buf[slot], p_buf[slot]
        s_out, p_out = s_buf[(slot + 1) % RING], p_buf[(slot + 1) % RING]
        out = []
        for mp in range(2):
            m, alpha, mb = carry[3 * mp:3 * mp + 3]
            mn = jnp.maximum(m, mb)
            mbs = []
            for c in range(blk // QCHUNK):
                cs = slice(c * QCHUNK, (c + 1) * QCHUNK)
                acc[mp][:, cs] = alpha[:, cs] * acc[mp][:, cs] + _dot(vt_ref[pj], p_in[mp][:, cs])
                p_out[mp][:, cs] = jnp.exp2(s_in[mp][:, cs] - mn[:, cs]).astype(BF16)
                if nxt is not None:
                    s = _dot_nt(ka[mp][nxt], qa[mp][cs, :])
                    s_out[mp][:, cs] = s
                    mbs.append(jnp.max(s, axis=0, keepdims=True))
            alpha = jnp.exp2(m - mn)
            if nxt is not None:
                mb = jnp.concatenate(mbs, axis=1)
            out += [mn, alpha, mb]
        return tuple(out)

    krow = lax.broadcasted_iota(jnp.int32, (blk, blk), 0)
    qcol = lax.broadcasted_iota(jnp.int32, (blk, blk), 1)
    causal = krow <= qcol
    carry = []
    for mp in range(2):
        s = jnp.where(causal, scores(mp, qi), NEG_BIG)
        m = jnp.max(s, axis=0, keepdims=True)
        p_buf[0][mp][...] = jnp.exp2(s - m).astype(BF16)
        acc[mp][...] = jnp.zeros((HEAD_DIM + SUM_ROWS, blk), F32)
        s = scores(mp, 0)
        s_buf[0][mp][...] = s
        carry += [m, jnp.ones_like(m), jnp.max(s, axis=0, keepdims=True)]
    carry = tuple(carry)

    def unrolled(u, carry):
        t = RING * u
        for k in range(RING - 1):
            carry = stage(k, t + k, t + k + 1, carry)
        return stage(RING - 1, t + RING - 1, jnp.minimum(t + RING, qi - 1), carry)

    carry = lax.fori_loop(0, qi // RING, unrolled, carry)

    def make_tail(rem):
        def tail(carry):
            t = qi - rem
            for k in range(rem):
                carry = stage(k, t + k, t + k + 1 if k < rem - 1 else None, carry)
            for mp in range(2):
                pv_update(mp, jnp.maximum(qi - 1, 0), p_buf[rem % RING], carry[3 * mp + 1])
            return carry
        return tail

    lax.switch(qi % RING, [make_tail(rem) for rem in range(RING)], carry)

    lam = (jnp.exp(jnp.sum(lq1_ref[...] * lk1_ref[...], axis=-1, keepdims=True))
           - jnp.exp(jnp.sum(lq2_ref[...] * lk2_ref[...], axis=-1, keepdims=True)) + lambda_init)
    l1 = acc[0][HEAD_DIM:HEAD_DIM + 1, :]
    l2 = acc[1][HEAD_DIM:HEAD_DIM + 1, :]
    o_t = acc[0][0:HEAD_DIM, :] / l1 - lam * (acc[1][0:HEAD_DIM, :] / l2)
    o = o_t.T
    o = _rms(o, sg_ref[0]) * (1.0 - lambda_init)
    o_ref[...] = o.astype(o_ref.dtype)


def _diffattn(proj, lq1, lk1, lq2, lk2, subln_g, lambda_init, batch, seq, blk=512):
    m = proj.shape[0]
    nblk = seq // blk
    slopes = (2.0 ** (-8.0 * np.arange(1, DIFF_HEADS + 1, dtype=np.float32) / DIFF_HEADS)).astype(np.float32)
    vec = pl.BlockSpec((1, DIFF_MAP_DIM), lambda b, h, i: (0, 0))
    return pl.pallas_call(
        functools.partial(_diffattn_kernel, blk=blk, nblk=nblk, lambda_init=lambda_init),
        grid=(batch, DIFF_HEADS, nblk),
        in_specs=[pl.BlockSpec(memory_space=pltpu.SMEM),
                  pl.BlockSpec((blk, HEAD_DIM), lambda b, h, i: (b * nblk + i, COL_DQ + h)),
                  pl.BlockSpec((seq, HEAD_DIM), lambda b, h, i: (b, COL_DK + h)),
                  pl.BlockSpec((seq, HEAD_DIM), lambda b, h, i: (b, COL_DV + h)),
                  vec, vec, vec, vec,
                  pl.BlockSpec((1, 1, HEAD_DIM), lambda b, h, i: (h, 0, 0))],
        out_specs=pl.BlockSpec((blk, HEAD_DIM), lambda b, h, i: (b * nblk + i, h)),
        out_shape=jax.ShapeDtypeStruct((m, DIFF_W), BF16),
        scratch_shapes=[pltpu.VMEM((nblk, blk, HEAD_DIM), BF16),
                        pltpu.VMEM((nblk, blk, HEAD_DIM), BF16),
                        pltpu.VMEM((nblk, HEAD_DIM + SUM_ROWS, blk), BF16),
                        *[pltpu.VMEM((HEAD_DIM + SUM_ROWS, blk), F32)] * 2,
                        *[pltpu.VMEM((blk, blk), F32)] * (2 * RING),
                        *[pltpu.VMEM((blk, blk), BF16)] * (2 * RING)],
        compiler_params=_params("arbitrary", "arbitrary", "arbitrary"),
        name="diffattn",
    )(jnp.asarray(slopes), proj, proj, proj,
      lq1.reshape(1, -1), lk1.reshape(1, -1), lq2.reshape(1, -1), lk2.reshape(1, -1),
      subln_g.reshape(DIFF_HEADS, 1, HEAD_DIM))


def _sgu_kernel(u_ref, v_ref, lng_ref, lnb_ref, w_ref, b_ref, o_ref, *, tb):
    row = lax.broadcasted_iota(jnp.int32, (CHUNK, CHUNK), 0)
    col = lax.broadcasted_iota(jnp.int32, (CHUNK, CHUNK), 1)
    w = jnp.where(row >= col, w_ref[0], 0.0).astype(BF16)
    bias = b_ref[0]
    lng = lng_ref[0]
    lnb = lnb_ref[0]
    for c in range(tb // CHUNK):
        sl = slice(c * CHUNK, (c + 1) * CHUNK)
        u = jax.nn.gelu(u_ref[sl, :].astype(F32))
        v = jax.nn.gelu(v_ref[sl, :].astype(F32))
        mu = jnp.mean(v, axis=-1, keepdims=True)
        vc = v - mu
        var = jnp.mean(vc * vc, axis=-1, keepdims=True)
        vn = vc * lax.rsqrt(var + EPS) * lng + lnb
        mixed = _dot(w, vn.astype(BF16)) + bias
        o_ref[sl, :] = (u * mixed).astype(o_ref.dtype)


def _sgu(proj, ln_g, ln_b, w_s, b_s, tb=1024):
    m = proj.shape[0]
    g = SGU_GROUPS
    per_group = lambda i, j: (j, 0, 0)
    return pl.pallas_call(
        functools.partial(_sgu_kernel, tb=tb),
        grid=(m // tb, g),
        in_specs=[pl.BlockSpec((tb, HEAD_DIM), lambda i, j: (i, COL_SU + j)),
                  pl.BlockSpec((tb, HEAD_DIM), lambda i, j: (i, COL_SV + j)),
                  pl.BlockSpec((1, 1, HEAD_DIM), per_group),
                  pl.BlockSpec((1, 1, HEAD_DIM), per_group),
                  pl.BlockSpec((1, CHUNK, CHUNK), per_group),
                  pl.BlockSpec((1, CHUNK, 1), per_group)],
        out_specs=pl.BlockSpec((tb, HEAD_DIM), lambda i, j: (i, j)),
        out_shape=jax.ShapeDtypeStruct((m, SGU_W), BF16),
        compiler_params=_params("parallel", "arbitrary"),
        name="sgu",
    )(proj, proj, ln_g.reshape(g, 1, HEAD_DIM), ln_b.reshape(g, 1, HEAD_DIM), w_s, b_s.reshape(g, CHUNK, 1))


def _outproj_kernel(yr_ref, yd_ref, ys_ref, w_ref, x_ref, gpost_ref, gnext_ref, xo_ref, ho_ref):
    mix = (_dot(yr_ref[...], w_ref[0:RET_W, :])
           + _dot(yd_ref[...], w_ref[RET_W:RET_W + DIFF_W, :])
           + _dot(ys_ref[...], w_ref[RET_W + DIFF_W:D_MODEL, :]))
    xn = x_ref[...] + _rms(mix, gpost_ref[...])
    xo_ref[...] = xn
    ho_ref[...] = _rms(xn, gnext_ref[...]).astype(ho_ref.dtype)


def _outproj(y_ret, y_diff, y_sgu, w, x, g_post, g_next, tm=512):
    m, d = x.shape
    row = lambda i: (i, 0)
    const = lambda i: (0, 0)
    return pl.pallas_call(
        _outproj_kernel,
        grid=(m // tm,),
        in_specs=[pl.BlockSpec((tm, RET_W), row), pl.BlockSpec((tm, DIFF_W), row), pl.BlockSpec((tm, SGU_W), row),
                  pl.BlockSpec((d, d), const), pl.BlockSpec((tm, d), row),
                  pl.BlockSpec((1, d), const), pl.BlockSpec((1, d), const)],
        out_specs=[pl.BlockSpec((tm, d), row), pl.BlockSpec((tm, d), row)],
        out_shape=[jax.ShapeDtypeStruct((m, d), F32), jax.ShapeDtypeStruct((m, d), BF16)],
        compiler_params=_params("parallel"),
        name="outproj",
    )(y_ret, y_diff, y_sgu, w, x, g_post.reshape(1, d), g_next.reshape(1, d))


def _ffn_kernel(h_ref, x_ref, wg_ref, wu_ref, wd_ref, gpost_ref, gnext_ref, xo_ref, ho_ref, act_ref):
    j = pl.program_id(1)
    nh = pl.num_programs(1) - 1

    def activations(slot):
        h = h_ref[...]
        gate = _dot(h, wg_ref[...])
        up = _dot(h, wu_ref[...])
        act_ref[slot] = (gate * jax.nn.sigmoid(gate) * up).astype(BF16)

    @pl.when(j == 0)
    def _():
        xo_ref[...] = jnp.zeros_like(xo_ref)
        activations(0)

    for parity in range(2):
        @pl.when((j > 0) & (j < nh) & (j % 2 == parity))
        def _():
            xo_ref[...] += _dot(act_ref[1 - parity], wd_ref[...])
            activations(parity)

    @pl.when(j == nh)
    def _():
        f = xo_ref[...] + _dot(act_ref[(FFN_TILES - 1) % 2], wd_ref[...])
        xn = x_ref[...] + _rms(f, gpost_ref[...])
        xo_ref[...] = xn
        ho_ref[...] = _rms(xn, gnext_ref[...]).astype(ho_ref.dtype)


def _ffn(h, x, wg, wu, wd, g_post, g_next, tm=1024):
    m, d = x.shape
    th = wg.shape[1] // FFN_TILES
    row = lambda i, j: (i, 0)
    const = lambda i, j: (0, 0)
    up_tile = lambda i, j: (0, jnp.minimum(j, FFN_TILES - 1))
    down_tile = lambda i, j: (jnp.maximum(j - 1, 0), 0)
    once = pl.Buffered(1)
    return pl.pallas_call(
        _ffn_kernel,
        grid=(m // tm, FFN_TILES + 1),
        in_specs=[pl.BlockSpec((tm, d), row, pipeline_mode=once), pl.BlockSpec((tm, d), row, pipeline_mode=once),
                  pl.BlockSpec((d, th), up_tile), pl.BlockSpec((d, th), up_tile),
                  pl.BlockSpec((th, d), down_tile),
                  pl.BlockSpec((1, d), const), pl.BlockSpec((1, d), const)],
        out_specs=[pl.BlockSpec((tm, d), row), pl.BlockSpec((tm, d), row)],
        out_shape=[jax.ShapeDtypeStruct((m, d), F32), jax.ShapeDtypeStruct((m, d), BF16)],
        scratch_shapes=[pltpu.VMEM((2, tm, th), BF16)],
        compiler_params=pltpu.CompilerParams(dimension_semantics=("parallel", "arbitrary"),
                                             vmem_limit_bytes=VMEM_LIMIT_FFN),
        name="ffn",
    )(h, x, wg, wu, wd, g_post.reshape(1, d), g_next.reshape(1, d))


def kernel(x, pre_mix_g, w_in, ret_gn_g, diff_lam_q1, diff_lam_k1, diff_lam_q2, diff_lam_k2, diff_subln_g,
           sgu_ln_g, sgu_ln_b, sgu_w, sgu_b, w_out, post_mix_g, pre_ffn_g, w_gate, w_up, w_down, post_ffn_g):
    batch, seq, d = x.shape
    depth = w_in.shape[0]
    xf = x.reshape(batch * seq, d)
    h = _rmsnorm(xf, pre_mix_g[0])
    for l in range(depth):
        lambda_init = 0.8 - 0.6 * math.exp(-0.3 * l)
        proj = _inproj(h, w_in[l].astype(BF16))
        y_ret = _retention(proj, ret_gn_g[l], batch, seq)
        y_diff = _diffattn(proj, diff_lam_q1[l], diff_lam_k1[l], diff_lam_q2[l], diff_lam_k2[l],
                           diff_subln_g[l], lambda_init, batch, seq)
        y_sgu = _sgu(proj, sgu_ln_g[l], sgu_ln_b[l], sgu_w[l], sgu_b[l])
        xf, h = _outproj(y_ret, y_diff, y_sgu, w_out[l].astype(BF16), xf, post_mix_g[l], pre_ffn_g[l])
        g_next = pre_mix_g[(l + 1) % depth]
        xf, h = _ffn(h, xf, w_gate[l].astype(BF16), w_up[l].astype(BF16), w_down[l].astype(BF16),
                     post_ffn_g[l], g_next)
    return xf.reshape(batch, seq, d)
```

```python
import functools
import math

import numpy as np
import jax
import jax.numpy as jnp
from jax import lax
from jax.experimental import pallas as pl
from jax.experimental.pallas import tpu as pltpu

D_MODEL = 2048
HEAD_DIM = 128
RET_HEADS = 6
DIFF_HEADS = 6
SGU_GROUPS = 4
DIFF_MAP_DIM = 64
RET_W = RET_HEADS * HEAD_DIM
DIFF_W = DIFF_HEADS * HEAD_DIM
SGU_W = SGU_GROUPS * HEAD_DIM
IN_W = 4 * RET_W + 3 * DIFF_W + 2 * SGU_W
CHUNK = 128
EPS = 1e-6
LOG2E = math.log2(math.e)
NEG_BIG = -1e30
SUM_ROWS = 16
FFN_TILES = 11
QCHUNK = 256
RING = 2

COL_RQ, COL_RK, COL_RV, COL_RG = 0, 6, 12, 18
COL_DQ, COL_DK, COL_DV = 24, 30, 36
COL_SU, COL_SV = 42, 46

V7X_VMEM_BYTES = 64 * 1024 * 1024
VMEM_LIMIT = 56 * 1024 * 1024
VMEM_LIMIT_FFN = 60 * 1024 * 1024

BF16 = jnp.bfloat16
F32 = jnp.float32


def _dot(a, b):
    return jnp.dot(a, b, preferred_element_type=F32)


def _dot_nt(a, b):
    return lax.dot_general(a, b, (((1,), (1,)), ((), ())), preferred_element_type=F32)


def _dot_tn(a, b):
    return lax.dot_general(a, b, (((0,), (0,)), ((), ())), preferred_element_type=F32)


def _rms(x, g):
    return x * lax.rsqrt(jnp.mean(x * x, axis=-1, keepdims=True) + EPS) * g


def _params(*sem):
    return pltpu.CompilerParams(dimension_semantics=sem, vmem_limit_bytes=VMEM_LIMIT)


def _rmsnorm_kernel(x_ref, g_ref, o_ref):
    o_ref[...] = _rms(x_ref[...], g_ref[...]).astype(o_ref.dtype)


def _rmsnorm(x, g, tm=512):
    m, d = x.shape
    return pl.pallas_call(
        _rmsnorm_kernel,
        grid=(m // tm,),
        in_specs=[pl.BlockSpec((tm, d), lambda i: (i, 0)), pl.BlockSpec((1, d), lambda i: (0, 0))],
        out_specs=pl.BlockSpec((tm, d), lambda i: (i, 0)),
        out_shape=jax.ShapeDtypeStruct((m, d), BF16),
        compiler_params=_params("parallel"),
        name="rmsnorm",
    )(x, g.reshape(1, d))


def _matmul_kernel(h_ref, w_ref, o_ref):
    o_ref[...] = _dot(h_ref[...], w_ref[...]).astype(o_ref.dtype)


def _inproj(h, w, tm=1024, tn=1280):
    m, k = h.shape
    n = w.shape[1]
    return pl.pallas_call(
        _matmul_kernel,
        grid=(m // tm, n // tn),
        in_specs=[pl.BlockSpec((tm, k), lambda i, j: (i, 0)), pl.BlockSpec((k, tn), lambda i, j: (0, j))],
        out_specs=pl.BlockSpec((tm, tn), lambda i, j: (i, j)),
        out_shape=jax.ShapeDtypeStruct((m, n), BF16),
        compiler_params=_params("parallel", "arbitrary"),
        name="inproj",
    )(h, w)


def _retention_kernel(lg_ref, q_ref, k_ref, v_ref, g_ref, gn_ref, o_ref, r_ref, *, tb):
    h = pl.program_id(1)
    t = pl.program_id(2)

    @pl.when(t == 0)
    def _():
        r_ref[...] = jnp.zeros_like(r_ref)

    lg = lg_ref[h]
    scale = HEAD_DIM ** -0.5
    row = lax.broadcasted_iota(jnp.int32, (CHUNK, CHUNK), 0)
    col = lax.broadcasted_iota(jnp.int32, (CHUNK, CHUNK), 1)
    rel = (row - col).astype(F32)
    decay = jnp.where(rel >= 0, jnp.exp(lg * jnp.maximum(rel, 0.0)), 0.0) * scale
    pos = lax.broadcasted_iota(jnp.int32, (CHUNK, 1), 0).astype(F32)
    zeta = jnp.exp(lg * (CHUNK - 1.0 - pos)) * scale
    xi = jnp.exp(lg * (pos + 1.0))
    chunk_decay = jnp.exp(jnp.full((1, 1), lg * CHUNK, F32))
    gn = gn_ref[0]

    for c in range(tb // CHUNK):
        sl = slice(c * CHUNK, (c + 1) * CHUNK)
        q = q_ref[sl, :]
        k = k_ref[sl, :]
        v = v_ref[sl, :]
        scores = _dot_nt(q, k) * decay
        inner = _dot(scores.astype(BF16), v)
        r_prev = r_ref[...]
        cross = _dot(q, r_prev.astype(BF16)) * xi
        kz = (k.astype(F32) * zeta).astype(BF16)
        r_ref[...] = _dot_tn(kz, v) + chunk_decay * r_prev
        y = inner + cross
        mu = jnp.mean(y, axis=-1, keepdims=True)
        yc = y - mu
        var = jnp.mean(yc * yc, axis=-1, keepdims=True)
        y = yc * lax.rsqrt(var + EPS) * gn
        gate = g_ref[sl, :].astype(F32)
        o_ref[sl, :] = (gate * jax.nn.sigmoid(gate) * y).astype(o_ref.dtype)


def _retention(proj, gn_gain, batch, seq, tb=1024):
    m = proj.shape[0]
    nt = seq // tb
    log_gamma = np.log1p(-(2.0 ** (-5.0 - np.arange(RET_HEADS, dtype=np.float32)))).astype(np.float32)

    def blk(col):
        return pl.BlockSpec((tb, HEAD_DIM), lambda b, h, t: (b * nt + t, col + h))

    return pl.pallas_call(
        functools.partial(_retention_kernel, tb=tb),
        grid=(batch, RET_HEADS, nt),
        in_specs=[pl.BlockSpec(memory_space=pltpu.SMEM),
                  blk(COL_RQ), blk(COL_RK), blk(COL_RV), blk(COL_RG),
                  pl.BlockSpec((1, 1, HEAD_DIM), lambda b, h, t: (h, 0, 0))],
        out_specs=blk(0),
        out_shape=jax.ShapeDtypeStruct((m, RET_W), BF16),
        scratch_shapes=[pltpu.VMEM((HEAD_DIM, HEAD_DIM), F32)],
        compiler_params=_params("arbitrary", "arbitrary", "arbitrary"),
        name="retention",
    )(jnp.asarray(log_gamma), proj, proj, proj, proj, gn_gain.reshape(RET_HEADS, 1, HEAD_DIM))


def _diffattn_kernel(slope_ref, q_ref, k_ref, v_ref, lq1_ref, lk1_ref, lq2_ref, lk2_ref, sg_ref, o_ref,
                     k1_ref, k2_ref, vt_ref, *bufs, blk, nblk, lambda_init):
    acc = bufs[0:2]
    s_buf = [bufs[2 + 2 * k:4 + 2 * k] for k in range(RING)]
    p_buf = [bufs[2 + 2 * RING + 2 * k:4 + 2 * RING + 2 * k] for k in range(RING)]
    h = pl.program_id(1)
    qi = pl.program_id(2)
    lane = lax.broadcasted_iota(jnp.int32, (blk, HEAD_DIM), 1)

    @pl.when(qi == 0)
    def _():
        slope2 = slope_ref[h] * LOG2E

        def prep(j, carry):
            start = pl.multiple_of(j * blk, blk)
            kb = k_ref[pl.ds(start, blk), :].astype(F32)
            kpos = (lax.broadcasted_iota(jnp.int32, (blk, HEAD_DIM), 0) + start).astype(F32)
            bias = kpos * slope2
            hi = bias.astype(BF16).astype(F32)
            r1 = bias - hi
            mid = r1.astype(BF16).astype(F32)
            lo = r1 - mid
            zero = jnp.zeros_like(kb)
            k1 = jnp.where(lane < 64, kb,
                           jnp.where(lane == 64, hi, jnp.where(lane == 65, mid, jnp.where(lane == 66, lo, zero))))
            k2 = jnp.where(lane >= 64, kb,
                           jnp.where(lane == 0, hi, jnp.where(lane == 1, mid, jnp.where(lane == 2, lo, zero))))
            k1_ref[j] = k1.astype(BF16)
            k2_ref[j] = k2.astype(BF16)
            vt_ref[j, 0:HEAD_DIM, :] = v_ref[pl.ds(start, blk), :].astype(F32).T.astype(BF16)
            vt_ref[j, HEAD_DIM:, :] = jnp.ones((SUM_ROWS, blk), BF16)
            return carry

        lax.fori_loop(0, nblk, prep, 0)

    qs = q_ref[...].astype(F32) * (DIFF_MAP_DIM ** -0.5 * LOG2E)
    one = jnp.ones_like(qs)
    zero = jnp.zeros_like(qs)
    qa = (jnp.where(lane < 64, qs, jnp.where(lane < 67, one, zero)).astype(BF16),
          jnp.where(lane >= 64, qs, jnp.where(lane < 3, one, zero)).astype(BF16))
    ka = (k1_ref, k2_ref)

    def scores(mp, j):
        return _dot_nt(ka[mp][j], qa[mp])

    def pv_update(mp, j, p_in, alpha):
        acc[mp][...] = alpha * acc[mp][...] + _dot(vt_ref[j], p_in[mp][...])

    def stage(slot, t, nxt, carry):
        pj = jnp.where(t == 0, qi, t - 1)
        s_in, p_in = s_buf[slot], p_buf[slot]
        s_out, p_out = s_buf[(slot + 1) % RING], p_buf[(slot + 1) % RING]
        out = []
        for mp in range(2):
            m, alpha, mb = carry[3 * mp:3 * mp + 3]
            mn = jnp.maximum(m, mb)
            mbs = []
            for c in range(blk // QCHUNK):
                cs = slice(c * QCHUNK, (c + 1) * QCHUNK)
                acc[mp][:, cs] = alpha[:, cs] * acc[mp][:, cs] + _dot(vt_ref[pj], p_in[mp][:, cs])
                p_out[mp][:, cs] = jnp.exp2(s_in[mp][:, cs] - mn[:, cs]).astype(BF16)
                if nxt is not None:
                    s = _dot_nt(ka[mp][nxt], qa[mp][cs, :])
                    s_out[mp][:, cs] = s
                    mbs.append(jnp.max(s, axis=0, keepdims=True))
            alpha = jnp.exp2(m - mn)
            if nxt is not None:
                mb = jnp.concatenate(mbs, axis=1)
            out += [mn, alpha, mb]
        return tuple(out)

    krow = lax.broadcasted_iota(jnp.int32, (blk, blk), 0)
    qcol = lax.broadcasted_iota(jnp.int32, (blk, blk), 1)
    causal = krow <= qcol
    carry = []
    for mp in range(2):
        s = jnp.where(causal, scores(mp, qi), NEG_BIG)
        m = jnp.max(s, axis=0, keepdims=True)
        p_buf[0][mp][...] = jnp.exp2(s - m).astype(BF16)
        acc[mp][...] = jnp.zeros((HEAD_DIM + SUM_ROWS, blk), F32)
        s = scores(mp, 0)
        s_buf[0][mp][...] = s
        carry += [m, jnp.ones_like(m), jnp.max(s, axis=0, keepdims=True)]
    carry = tuple(carry)

    def unrolled(u, carry):
        t = RING * u
        for k in range(RING - 1):
            carry = stage(k, t + k, t + k + 1, carry)
        return stage(RING - 1, t + RING - 1, jnp.minimum(t + RING, qi - 1), carry)

    carry = lax.fori_loop(0, qi // RING, unrolled, carry)

    def make_tail(rem):
        def tail(carry):
            t = qi - rem
            for k in range(rem):
                carry = stage(k, t + k, t + k + 1 if k < rem - 1 else None, carry)
            for mp in range(2):
                pv_update(mp, jnp.maximum(qi - 1, 0), p_buf[rem % RING], carry[3 * mp + 1])
            return carry
        return tail

    lax.switch(qi % RING, [make_tail(rem) for rem in range(RING)], carry)

    lam = (jnp.exp(jnp.sum(lq1_ref[...] * lk1_ref[...], axis=-1, keepdims=True))
           - jnp.exp(jnp.sum(lq2_ref[...] * lk2_ref[...], axis=-1, keepdims=True)) + lambda_init)
    l1 = acc[0][HEAD_DIM:HEAD_DIM + 1, :]
    l2 = acc[1][HEAD_DIM:HEAD_DIM + 1, :]
    o_t = acc[0][0:HEAD_DIM, :] / l1 - lam * (acc[1][0:HEAD_DIM, :] / l2)
    o = o_t.T
    o = _rms(o, sg_ref[0]) * (1.0 - lambda_init)
    o_ref[...] = o.astype(o_ref.dtype)


def _diffattn(proj, lq1, lk1, lq2, lk2, subln_g, lambda_init, batch, seq, blk=512):
    m = proj.shape[0]
    nblk = seq // blk
    slopes = (2.0 ** (-8.0 * np.arange(1, DIFF_HEADS + 1, dtype=np.float32) / DIFF_HEADS)).astype(np.float32)
    vec = pl.BlockSpec((1, DIFF_MAP_DIM), lambda b, h, i: (0, 0))
    return pl.pallas_call(
        functools.partial(_diffattn_kernel, blk=blk, nblk=nblk, lambda_init=lambda_init),
        grid=(batch, DIFF_HEADS, nblk),
        in_specs=[pl.BlockSpec(memory_space=pltpu.SMEM),
                  pl.BlockSpec((blk, HEAD_DIM), lambda b, h, i: (b * nblk + i, COL_DQ + h)),
                  pl.BlockSpec((seq, HEAD_DIM), lambda b, h, i: (b, COL_DK + h)),
                  pl.BlockSpec((seq, HEAD_DIM), lambda b, h, i: (b, COL_DV + h)),
                  vec, vec, vec, vec,
                  pl.BlockSpec((1, 1, HEAD_DIM), lambda b, h, i: (h, 0, 0))],
        out_specs=pl.BlockSpec((blk, HEAD_DIM), lambda b, h, i: (b * nblk + i, h)),
        out_shape=jax.ShapeDtypeStruct((m, DIFF_W), BF16),
        scratch_shapes=[pltpu.VMEM((nblk, blk, HEAD_DIM), BF16),
                        pltpu.VMEM((nblk, blk, HEAD_DIM), BF16),
                        pltpu.VMEM((nblk, HEAD_DIM + SUM_ROWS, blk), BF16),
                        *[pltpu.VMEM((HEAD_DIM + SUM_ROWS, blk), F32)] * 2,
                        *[pltpu.VMEM((blk, blk), F32)] * (2 * RING),
                        *[pltpu.VMEM((blk, blk), BF16)] * (2 * RING)],
        compiler_params=_params("arbitrary", "arbitrary", "arbitrary"),
        name="diffattn",
    )(jnp.asarray(slopes), proj, proj, proj,
      lq1.reshape(1, -1), lk1.reshape(1, -1), lq2.reshape(1, -1), lk2.reshape(1, -1),
      subln_g.reshape(DIFF_HEADS, 1, HEAD_DIM))


def _sgu_kernel(u_ref, v_ref, lng_ref, lnb_ref, w_ref, b_ref, o_ref, *, tb):
    row = lax.broadcasted_iota(jnp.int32, (CHUNK, CHUNK), 0)
    col = lax.broadcasted_iota(jnp.int32, (CHUNK, CHUNK), 1)
    w = jnp.where(row >= col, w_ref[0], 0.0).astype(BF16)
    bias = b_ref[0]
    lng = lng_ref[0]
    lnb = lnb_ref[0]
    for c in range(tb // CHUNK):
        sl = slice(c * CHUNK, (c + 1) * CHUNK)
        u = jax.nn.gelu(u_ref[sl, :].astype(F32))
        v = jax.nn.gelu(v_ref[sl, :].astype(F32))
        mu = jnp.mean(v, axis=-1, keepdims=True)
        vc = v - mu
        var = jnp.mean(vc * vc, axis=-1, keepdims=True)
        vn = vc * lax.rsqrt(var + EPS) * lng + lnb
        mixed = _dot(w, vn.astype(BF16)) + bias
        o_ref[sl, :] = (u * mixed).astype(o_ref.dtype)


def _sgu(proj, ln_g, ln_b, w_s, b_s, tb=1024):
    m = proj.shape[0]
    g = SGU_GROUPS
    per_group = lambda i, j: (j, 0, 0)
    return pl.pallas_call(
        functools.partial(_sgu_kernel, tb=tb),
        grid=(m // tb, g),
        in_specs=[pl.BlockSpec((tb, HEAD_DIM), lambda i, j: (i, COL_SU + j)),
                  pl.BlockSpec((tb, HEAD_DIM), lambda i, j: (i, COL_SV + j)),
                  pl.BlockSpec((1, 1, HEAD_DIM), per_group),
                  pl.BlockSpec((1, 1, HEAD_DIM), per_group),
                  pl.BlockSpec((1, CHUNK, CHUNK), per_group),
                  pl.BlockSpec((1, CHUNK, 1), per_group)],
        out_specs=pl.BlockSpec((tb, HEAD_DIM), lambda i, j: (i, j)),
        out_shape=jax.ShapeDtypeStruct((m, SGU_W), BF16),
        compiler_params=_params("parallel", "arbitrary"),
        name="sgu",
    )(proj, proj, ln_g.reshape(g, 1, HEAD_DIM), ln_b.reshape(g, 1, HEAD_DIM), w_s, b_s.reshape(g, CHUNK, 1))


def _outproj_kernel(yr_ref, yd_ref, ys_ref, w_ref, x_ref, gpost_ref, gnext_ref, xo_ref, ho_ref):
    mix = (_dot(yr_ref[...], w_ref[0:RET_W, :])
           + _dot(yd_ref[...], w_ref[RET_W:RET_W + DIFF_W, :])
           + _dot(ys_ref[...], w_ref[RET_W + DIFF_W:D_MODEL, :]))
    xn = x_ref[...] + _rms(mix, gpost_ref[...])
    xo_ref[...] = xn
    ho_ref[...] = _rms(xn, gnext_ref[...]).astype(ho_ref.dtype)


def _outproj(y_ret, y_diff, y_sgu, w, x, g_post, g_next, tm=512):
    m, d = x.shape
    row = lambda i: (i, 0)
    const = lambda i: (0, 0)
    return pl.pallas_call(
        _outproj_kernel,
        grid=(m // tm,),
        in_specs=[pl.BlockSpec((tm, RET_W), row), pl.BlockSpec((tm, DIFF_W), row), pl.BlockSpec((tm, SGU_W), row),
                  pl.BlockSpec((d, d), const), pl.BlockSpec((tm, d), row),
                  pl.BlockSpec((1, d), const), pl.BlockSpec((1, d), const)],
        out_specs=[pl.BlockSpec((tm, d), row), pl.BlockSpec((tm, d), row)],
        out_shape=[jax.ShapeDtypeStruct((m, d), F32), jax.ShapeDtypeStruct((m, d), BF16)],
        compiler_params=_params("parallel"),
        name="outproj",
    )(y_ret, y_diff, y_sgu, w, x, g_post.reshape(1, d), g_next.reshape(1, d))


def _ffn_up_kernel(h_ref, wg_ref, wu_ref, a_ref, w_ref):
    th = wg_ref.shape[1]

    @pl.when(pl.program_id(1) == 0)
    def _():
        w_ref[:, 0:th] = wg_ref[...].astype(BF16)
        w_ref[:, th:] = wu_ref[...].astype(BF16)

    gu = _dot(h_ref[...], w_ref[...])
    gate = gu[:, 0:th]
    a_ref[...] = (gate * jax.nn.sigmoid(gate) * gu[:, th:]).astype(a_ref.dtype)


def _ffn_up(h, wg, wu, tm=1024):
    m, d = h.shape
    hidden = wg.shape[1]
    th = hidden // FFN_TILES
    w_tile = pl.BlockSpec((d, th), lambda j, i: (0, j))
    return pl.pallas_call(
        _ffn_up_kernel,
        grid=(FFN_TILES, m // tm),
        in_specs=[pl.BlockSpec((tm, d), lambda j, i: (i, 0)), w_tile, w_tile],
        out_specs=pl.BlockSpec((tm, th), lambda j, i: (i, j)),
        out_shape=jax.ShapeDtypeStruct((m, hidden), BF16),
        scratch_shapes=[pltpu.VMEM((d, 2 * th), BF16)],
        compiler_params=_params("arbitrary", "arbitrary"),
        name="ffn_up",
    )(h, wg, wu)


def _ffn_down_kernel(a_ref, w_ref, x_ref, gpost_ref, gnext_ref, xo_ref, ho_ref):
    f = _dot(a_ref[...], w_ref[...])
    xn = x_ref[...] + _rms(f, gpost_ref[...])
    xo_ref[...] = xn
    ho_ref[...] = _rms(xn, gnext_ref[...]).astype(ho_ref.dtype)


def _ffn_down(act, wd, x, g_post, g_next, tm=512):
    m, d = x.shape
    hidden = act.shape[1]
    row = lambda i: (i, 0)
    const = lambda i: (0, 0)
    return pl.pallas_call(
        _ffn_down_kernel,
        grid=(m // tm,),
        in_specs=[pl.BlockSpec((tm, hidden), row),
                  pl.BlockSpec((hidden, d), const, pipeline_mode=pl.Buffered(1)),
                  pl.BlockSpec((tm, d), row),
                  pl.BlockSpec((1, d), const), pl.BlockSpec((1, d), const)],
        out_specs=[pl.BlockSpec((tm, d), row), pl.BlockSpec((tm, d), row)],
        out_shape=[jax.ShapeDtypeStruct((m, d), F32), jax.ShapeDtypeStruct((m, d), BF16)],
        compiler_params=pltpu.CompilerParams(dimension_semantics=("parallel",), vmem_limit_bytes=VMEM_LIMIT_FFN),
        name="ffn_down",
    )(act, wd, x, g_post.reshape(1, d), g_next.reshape(1, d))


def kernel(x, pre_mix_g, w_in, ret_gn_g, diff_lam_q1, diff_lam_k1, diff_lam_q2, diff_lam_k2, diff_subln_g,
           sgu_ln_g, sgu_ln_b, sgu_w, sgu_b, w_out, post_mix_g, pre_ffn_g, w_gate, w_up, w_down, post_ffn_g):
    batch, seq, d = x.shape
    depth = w_in.shape[0]
    xf = x.reshape(batch * seq, d)
    h = _rmsnorm(xf, pre_mix_g[0])
    for l in range(depth):
        lambda_init = 0.8 - 0.6 * math.exp(-0.3 * l)
        proj = _inproj(h, w_in[l].astype(BF16))
        y_ret = _retention(proj, ret_gn_g[l], batch, seq)
        y_diff = _diffattn(proj, diff_lam_q1[l], diff_lam_k1[l], diff_lam_q2[l], diff_lam_k2[l],
                           diff_subln_g[l], lambda_init, batch, seq)
        y_sgu = _sgu(proj, sgu_ln_g[l], sgu_ln_b[l], sgu_w[l], sgu_b[l])
        xf, h = _outproj(y_ret, y_diff, y_sgu, w_out[l].astype(BF16), xf, post_mix_g[l], pre_ffn_g[l])
        g_next = pre_mix_g[(l + 1) % depth]
        act = _ffn_up(h, w_gate[l], w_up[l])
        xf, h = _ffn_down(act, w_down[l].astype(BF16), xf, post_ffn_g[l], g_next)
    return xf.reshape(batch, seq, d)
```

```python
import functools
import math

import numpy as np
import jax
import jax.numpy as jnp
from jax import lax
from jax.experimental import pallas as pl
from jax.experimental.pallas import tpu as pltpu

D_MODEL = 2048
HEAD_DIM = 128
RET_HEADS = 6
DIFF_HEADS = 6
SGU_GROUPS = 4
DIFF_MAP_DIM = 64
RET_W = RET_HEADS * HEAD_DIM
DIFF_W = DIFF_HEADS * HEAD_DIM
SGU_W = SGU_GROUPS * HEAD_DIM
IN_W = 4 * RET_W + 3 * DIFF_W + 2 * SGU_W
CHUNK = 128
EPS = 1e-6
LOG2E = math.log2(math.e)
NEG_BIG = -1e30
SUM_ROWS = 16
FFN_TILES = 11
QCHUNK = 256
RING = 2

COL_RQ, COL_RK, COL_RV, COL_RG = 0, 6, 12, 18
COL_DQ, COL_DK, COL_DV = 24, 30, 36
COL_SU, COL_SV = 42, 46

V7X_VMEM_BYTES = 64 * 1024 * 1024
VMEM_LIMIT = 56 * 1024 * 1024
VMEM_LIMIT_FFN = 60 * 1024 * 1024

BF16 = jnp.bfloat16
F32 = jnp.float32


def _dot(a, b):
    return jnp.dot(a, b, preferred_element_type=F32)


def _dot_nt(a, b):
    return lax.dot_general(a, b, (((1,), (1,)), ((), ())), preferred_element_type=F32)


def _dot_tn(a, b):
    return lax.dot_general(a, b, (((0,), (0,)), ((), ())), preferred_element_type=F32)


def _rms(x, g):
    return x * lax.rsqrt(jnp.mean(x * x, axis=-1, keepdims=True) + EPS) * g


def _params(*sem):
    return pltpu.CompilerParams(dimension_semantics=sem, vmem_limit_bytes=VMEM_LIMIT)


def _rmsnorm_kernel(x_ref, g_ref, o_ref):
    o_ref[...] = _rms(x_ref[...], g_ref[...]).astype(o_ref.dtype)


def _rmsnorm(x, g, tm=512):
    m, d = x.shape
    return pl.pallas_call(
        _rmsnorm_kernel,
        grid=(m // tm,),
        in_specs=[pl.BlockSpec((tm, d), lambda i: (i, 0)), pl.BlockSpec((1, d), lambda i: (0, 0))],
        out_specs=pl.BlockSpec((tm, d), lambda i: (i, 0)),
        out_shape=jax.ShapeDtypeStruct((m, d), BF16),
        compiler_params=_params("parallel"),
        name="rmsnorm",
    )(x, g.reshape(1, d))


def _inproj_kernel(h_ref, w_ref, o_ref, wb_ref):
    @pl.when(pl.program_id(1) == 0)
    def _():
        wb_ref[...] = w_ref[0].astype(BF16)

    o_ref[...] = _dot(h_ref[...], wb_ref[...]).astype(o_ref.dtype)


def _inproj(h, w_stack, layer, tm=1024, tn=1280):
    m, k = h.shape
    n = w_stack.shape[2]
    return pl.pallas_call(
        _inproj_kernel,
        grid=(n // tn, m // tm),
        in_specs=[pl.BlockSpec((tm, k), lambda j, i: (i, 0)),
                  pl.BlockSpec((1, k, tn), lambda j, i: (layer, 0, j))],
        out_specs=pl.BlockSpec((tm, tn), lambda j, i: (i, j)),
        out_shape=jax.ShapeDtypeStruct((m, n), BF16),
        scratch_shapes=[pltpu.VMEM((k, tn), BF16)],
        compiler_params=_params("arbitrary", "arbitrary"),
        name="inproj",
    )(h, w_stack)


def _retention_kernel(lg_ref, q_ref, k_ref, v_ref, g_ref, gn_ref, o_ref, r_ref, *, tb):
    h = pl.program_id(1)
    t = pl.program_id(2)

    @pl.when(t == 0)
    def _():
        r_ref[...] = jnp.zeros_like(r_ref)

    lg = lg_ref[h]
    scale = HEAD_DIM ** -0.5
    row = lax.broadcasted_iota(jnp.int32, (CHUNK, CHUNK), 0)
    col = lax.broadcasted_iota(jnp.int32, (CHUNK, CHUNK), 1)
    rel = (row - col).astype(F32)
    decay = jnp.where(rel >= 0, jnp.exp(lg * jnp.maximum(rel, 0.0)), 0.0) * scale
    pos = lax.broadcasted_iota(jnp.int32, (CHUNK, 1), 0).astype(F32)
    zeta = jnp.exp(lg * (CHUNK - 1.0 - pos)) * scale
    xi = jnp.exp(lg * (pos + 1.0))
    chunk_decay = jnp.exp(jnp.full((1, 1), lg * CHUNK, F32))
    gn = gn_ref[0]

    for c in range(tb // CHUNK):
        sl = slice(c * CHUNK, (c + 1) * CHUNK)
        q = q_ref[sl, :]
        k = k_ref[sl, :]
        v = v_ref[sl, :]
        scores = _dot_nt(q, k) * decay
        inner = _dot(scores.astype(BF16), v)
        r_prev = r_ref[...]
        cross = _dot(q, r_prev.astype(BF16)) * xi
        kz = (k.astype(F32) * zeta).astype(BF16)
        r_ref[...] = _dot_tn(kz, v) + chunk_decay * r_prev
        y = inner + cross
        mu = jnp.mean(y, axis=-1, keepdims=True)
        yc = y - mu
        var = jnp.mean(yc * yc, axis=-1, keepdims=True)
        y = yc * lax.rsqrt(var + EPS) * gn
        gate = g_ref[sl, :].astype(F32)
        o_ref[sl, :] = (gate * jax.nn.sigmoid(gate) * y).astype(o_ref.dtype)


def _retention(proj, gn_gain, batch, seq, tb=1024):
    m = proj.shape[0]
    nt = seq // tb
    log_gamma = np.log1p(-(2.0 ** (-5.0 - np.arange(RET_HEADS, dtype=np.float32)))).astype(np.float32)

    def blk(col):
        return pl.BlockSpec((tb, HEAD_DIM), lambda b, h, t: (b * nt + t, col + h))

    return pl.pallas_call(
        functools.partial(_retention_kernel, tb=tb),
        grid=(batch, RET_HEADS, nt),
        in_specs=[pl.BlockSpec(memory_space=pltpu.SMEM),
                  blk(COL_RQ), blk(COL_RK), blk(COL_RV), blk(COL_RG),
                  pl.BlockSpec((1, 1, HEAD_DIM), lambda b, h, t: (h, 0, 0))],
        out_specs=blk(0),
        out_shape=jax.ShapeDtypeStruct((m, RET_W), BF16),
        scratch_shapes=[pltpu.VMEM((HEAD_DIM, HEAD_DIM), F32)],
        compiler_params=_params("arbitrary", "arbitrary", "arbitrary"),
        name="retention",
    )(jnp.asarray(log_gamma), proj, proj, proj, proj, gn_gain.reshape(RET_HEADS, 1, HEAD_DIM))


def _diffattn_kernel(slope_ref, q_ref, k_ref, v_ref, lq1_ref, lk1_ref, lq2_ref, lk2_ref, sg_ref, o_ref,
                     k1_ref, k2_ref, vt_ref, *bufs, blk, nblk, lambda_init):
    acc = bufs[0:2]
    s_buf = [bufs[2 + 2 * k:4 + 2 * k] for k in range(RING)]
    p_buf = [bufs[2 + 2 * RING + 2 * k:4 + 2 * RING + 2 * k] for k in range(RING)]
    h = pl.program_id(1)
    qi = pl.program_id(2)
    lane = lax.broadcasted_iota(jnp.int32, (blk, HEAD_DIM), 1)

    @pl.when(qi == 0)
    def _():
        slope2 = slope_ref[h] * LOG2E

        def prep(j, carry):
            start = pl.multiple_of(j * blk, blk)
            kb = k_ref[pl.ds(start, blk), :].astype(F32)
            kpos = (lax.broadcasted_iota(jnp.int32, (blk, HEAD_DIM), 0) + start).astype(F32)
            bias = kpos * slope2
            hi = bias.astype(BF16).astype(F32)
            r1 = bias - hi
            mid = r1.astype(BF16).astype(F32)
            lo = r1 - mid
            zero = jnp.zeros_like(kb)
            k1 = jnp.where(lane < 64, kb,
                           jnp.where(lane == 64, hi, jnp.where(lane == 65, mid, jnp.where(lane == 66, lo, zero))))
            k2 = jnp.where(lane >= 64, kb,
                           jnp.where(lane == 0, hi, jnp.where(lane == 1, mid, jnp.where(lane == 2, lo, zero))))
            k1_ref[j] = k1.astype(BF16)
            k2_ref[j] = k2.astype(BF16)
            vt_ref[j, 0:HEAD_DIM, :] = v_ref[pl.ds(start, blk), :].astype(F32).T.astype(BF16)
            vt_ref[j, HEAD_DIM:, :] = jnp.ones((SUM_ROWS, blk), BF16)
            return carry

        lax.fori_loop(0, nblk, prep, 0)

    qs = q_ref[...].astype(F32) * (DIFF_MAP_DIM ** -0.5 * LOG2E)
    one = jnp.ones_like(qs)
    zero = jnp.zeros_like(qs)
    qa = (jnp.where(lane < 64, qs, jnp.where(lane < 67, one, zero)).astype(BF16),
          jnp.where(lane >= 64, qs, jnp.where(lane < 3, one, zero)).astype(BF16))
    ka = (k1_ref, k2_ref)

    def scores(mp, j):
        return _dot_nt(ka[mp][j], qa[mp])

    def pv_update(mp, j, p_in, alpha):
        acc[mp][...] = alpha * acc[mp][...] + _dot(vt_ref[j], p_in[mp][...])

    def stage(slot, t, nxt, carry):
        pj = jnp.where(t == 0, qi, t - 1)
        s_in, p_in = s_buf[slot], p_buf[slot]
        s_out, p_out = s_buf[(slot + 1) % RING], p_buf[(slot + 1) % RING]
        out = []
        for mp in range(2):
            m, alpha, mb = carry[3 * mp:3 * mp + 3]
            mn = jnp.maximum(m, mb)
            mbs = []
            for c in range(blk // QCHUNK):
                cs = slice(c * QCHUNK, (c + 1) * QCHUNK)
                acc[mp][:, cs] = alpha[:, cs] * acc[mp][:, cs] + _dot(vt_ref[pj], p_in[mp][:, cs])
                p_out[mp][:, cs] = jnp.exp2(s_in[mp][:, cs] - mn[:, cs]).astype(BF16)
                if nxt is not None:
                    s = _dot_nt(ka[mp][nxt], qa[mp][cs, :])
                    s_out[mp][:, cs] = s
                    mbs.append(jnp.max(s, axis=0, keepdims=True))
            alpha = jnp.exp2(m - mn)
            if nxt is not None:
                mb = jnp.concatenate(mbs, axis=1)
            out += [mn, alpha, mb]
        return tuple(out)

    krow = lax.broadcasted_iota(jnp.int32, (blk, blk), 0)
    qcol = lax.broadcasted_iota(jnp.int32, (blk, blk), 1)
    causal = krow <= qcol
    carry = []
    for mp in range(2):
        s = jnp.where(causal, scores(mp, qi), NEG_BIG)
        m = jnp.max(s, axis=0, keepdims=True)
        p_buf[0][mp][...] = jnp.exp2(s - m).astype(BF16)
        acc[mp][...] = jnp.zeros((HEAD_DIM + SUM_ROWS, blk), F32)
        s = scores(mp, 0)
        s_buf[0][mp][...] = s
        carry += [m, jnp.ones_like(m), jnp.max(s, axis=0, keepdims=True)]
    carry = tuple(carry)

    def unrolled(u, carry):
        t = RING * u
        for k in range(RING - 1):
            carry = stage(k, t + k, t + k + 1, carry)
        return stage(RING - 1, t + RING - 1, jnp.minimum(t + RING, qi - 1), carry)

    carry = lax.fori_loop(0, qi // RING, unrolled, carry)

    def make_tail(rem):
        def tail(carry):
            t = qi - rem
            for k in range(rem):
                carry = stage(k, t + k, t + k + 1 if k < rem - 1 else None, carry)
            for mp in range(2):
                pv_update(mp, jnp.maximum(qi - 1, 0), p_buf[rem % RING], carry[3 * mp + 1])
            return carry
        return tail

    lax.switch(qi % RING, [make_tail(rem) for rem in range(RING)], carry)

    lam = (jnp.exp(jnp.sum(lq1_ref[...] * lk1_ref[...], axis=-1, keepdims=True))
           - jnp.exp(jnp.sum(lq2_ref[...] * lk2_ref[...], axis=-1, keepdims=True)) + lambda_init)
    l1 = acc[0][HEAD_DIM:HEAD_DIM + 1, :]
    l2 = acc[1][HEAD_DIM:HEAD_DIM + 1, :]
    o_t = acc[0][0:HEAD_DIM, :] / l1 - lam * (acc[1][0:HEAD_DIM, :] / l2)
    o = o_t.T
    o = _rms(o, sg_ref[0]) * (1.0 - lambda_init)
    o_ref[...] = o.astype(o_ref.dtype)


def _diffattn(proj, lq1, lk1, lq2, lk2, subln_g, lambda_init, batch, seq, blk=512):
    m = proj.shape[0]
    nblk = seq // blk
    slopes = (2.0 ** (-8.0 * np.arange(1, DIFF_HEADS + 1, dtype=np.float32) / DIFF_HEADS)).astype(np.float32)
    vec = pl.BlockSpec((1, DIFF_MAP_DIM), lambda b, h, i: (0, 0))
    return pl.pallas_call(
        functools.partial(_diffattn_kernel, blk=blk, nblk=nblk, lambda_init=lambda_init),
        grid=(batch, DIFF_HEADS, nblk),
        in_specs=[pl.BlockSpec(memory_space=pltpu.SMEM),
                  pl.BlockSpec((blk, HEAD_DIM), lambda b, h, i: (b * nblk + i, COL_DQ + h)),
                  pl.BlockSpec((seq, HEAD_DIM), lambda b, h, i: (b, COL_DK + h)),
                  pl.BlockSpec((seq, HEAD_DIM), lambda b, h, i: (b, COL_DV + h)),
                  vec, vec, vec, vec,
                  pl.BlockSpec((1, 1, HEAD_DIM), lambda b, h, i: (h, 0, 0))],
        out_specs=pl.BlockSpec((blk, HEAD_DIM), lambda b, h, i: (b * nblk + i, h)),
        out_shape=jax.ShapeDtypeStruct((m, DIFF_W), BF16),
        scratch_shapes=[pltpu.VMEM((nblk, blk, HEAD_DIM), BF16),
                        pltpu.VMEM((nblk, blk, HEAD_DIM), BF16),
                        pltpu.VMEM((nblk, HEAD_DIM + SUM_ROWS, blk), BF16),
                        *[pltpu.VMEM((HEAD_DIM + SUM_ROWS, blk), F32)] * 2,
                        *[pltpu.VMEM((blk, blk), F32)] * (2 * RING),
                        *[pltpu.VMEM((blk, blk), BF16)] * (2 * RING)],
        compiler_params=_params("arbitrary", "arbitrary", "arbitrary"),
        name="diffattn",
    )(jnp.asarray(slopes), proj, proj, proj,
      lq1.reshape(1, -1), lk1.reshape(1, -1), lq2.reshape(1, -1), lk2.reshape(1, -1),
      subln_g.reshape(DIFF_HEADS, 1, HEAD_DIM))


def _sgu_kernel(u_ref, v_ref, lng_ref, lnb_ref, w_ref, b_ref, o_ref, *, tb):
    row = lax.broadcasted_iota(jnp.int32, (CHUNK, CHUNK), 0)
    col = lax.broadcasted_iota(jnp.int32, (CHUNK, CHUNK), 1)
    w = jnp.where(row >= col, w_ref[0], 0.0).astype(BF16)
    bias = b_ref[0]
    lng = lng_ref[0]
    lnb = lnb_ref[0]
    for c in range(tb // CHUNK):
        sl = slice(c * CHUNK, (c + 1) * CHUNK)
        u = jax.nn.gelu(u_ref[sl, :].astype(F32))
        v = jax.nn.gelu(v_ref[sl, :].astype(F32))
        mu = jnp.mean(v, axis=-1, keepdims=True)
        vc = v - mu
        var = jnp.mean(vc * vc, axis=-1, keepdims=True)
        vn = vc * lax.rsqrt(var + EPS) * lng + lnb
        mixed = _dot(w, vn.astype(BF16)) + bias
        o_ref[sl, :] = (u * mixed).astype(o_ref.dtype)


def _sgu(proj, ln_g, ln_b, w_s, b_s, tb=1024):
    m = proj.shape[0]
    g = SGU_GROUPS
    per_group = lambda i, j: (j, 0, 0)
    return pl.pallas_call(
        functools.partial(_sgu_kernel, tb=tb),
        grid=(m // tb, g),
        in_specs=[pl.BlockSpec((tb, HEAD_DIM), lambda i, j: (i, COL_SU + j)),
                  pl.BlockSpec((tb, HEAD_DIM), lambda i, j: (i, COL_SV + j)),
                  pl.BlockSpec((1, 1, HEAD_DIM), per_group),
                  pl.BlockSpec((1, 1, HEAD_DIM), per_group),
                  pl.BlockSpec((1, CHUNK, CHUNK), per_group),
                  pl.BlockSpec((1, CHUNK, 1), per_group)],
        out_specs=pl.BlockSpec((tb, HEAD_DIM), lambda i, j: (i, j)),
        out_shape=jax.ShapeDtypeStruct((m, SGU_W), BF16),
        compiler_params=_params("parallel", "arbitrary"),
        name="sgu",
    )(proj, proj, ln_g.reshape(g, 1, HEAD_DIM), ln_b.reshape(g, 1, HEAD_DIM), w_s, b_s.reshape(g, CHUNK, 1))


def _outproj_kernel(yr_ref, yd_ref, ys_ref, wf_ref, x_ref, gpost_ref, gnext_ref, xo_ref, ho_ref, w_ref):
    @pl.when(pl.program_id(0) == 0)
    def _():
        w_ref[...] = wf_ref[0].astype(BF16)

    mix = (_dot(yr_ref[...], w_ref[0:RET_W, :])
           + _dot(yd_ref[...], w_ref[RET_W:RET_W + DIFF_W, :])
           + _dot(ys_ref[...], w_ref[RET_W + DIFF_W:D_MODEL, :]))
    xn = x_ref[...] + _rms(mix, gpost_ref[...])
    xo_ref[...] = xn
    ho_ref[...] = _rms(xn, gnext_ref[...]).astype(ho_ref.dtype)


def _outproj(y_ret, y_diff, y_sgu, w_stack, layer, x, g_post, g_next, tm=512):
    m, d = x.shape
    row = lambda i: (i, 0)
    const = lambda i: (0, 0)
    return pl.pallas_call(
        _outproj_kernel,
        grid=(m // tm,),
        in_specs=[pl.BlockSpec((tm, RET_W), row), pl.BlockSpec((tm, DIFF_W), row), pl.BlockSpec((tm, SGU_W), row),
                  pl.BlockSpec((1, d, d), lambda i: (layer, 0, 0), pipeline_mode=pl.Buffered(1)),
                  pl.BlockSpec((tm, d), row),
                  pl.BlockSpec((1, d), const), pl.BlockSpec((1, d), const)],
        out_specs=[pl.BlockSpec((tm, d), row), pl.BlockSpec((tm, d), row)],
        out_shape=[jax.ShapeDtypeStruct((m, d), F32), jax.ShapeDtypeStruct((m, d), BF16)],
        scratch_shapes=[pltpu.VMEM((d, d), BF16)],
        compiler_params=_params("arbitrary"),
        name="outproj",
    )(y_ret, y_diff, y_sgu, w_stack, x, g_post.reshape(1, d), g_next.reshape(1, d))


def _ffn_up_kernel(h_ref, wg_ref, wu_ref, wd_ref, a_ref, wdb_ref, w_ref):
    th = wg_ref.shape[2]

    @pl.when(pl.program_id(1) == 0)
    def _():
        w_ref[:, 0:th] = wg_ref[0].astype(BF16)
        w_ref[:, th:] = wu_ref[0].astype(BF16)
        wdb_ref[...] = wd_ref[0].astype(BF16)

    gu = _dot(h_ref[...], w_ref[...])
    gate = gu[:, 0:th]
    a_ref[...] = (gate * jax.nn.sigmoid(gate) * gu[:, th:]).astype(a_ref.dtype)


def _ffn_up(h, wg_stack, wu_stack, wd_stack, layer, tm=1024):
    m, d = h.shape
    hidden = wg_stack.shape[2]
    th = hidden // FFN_TILES
    w_tile = pl.BlockSpec((1, d, th), lambda j, i: (layer, 0, j))
    return pl.pallas_call(
        _ffn_up_kernel,
        grid=(FFN_TILES, m // tm),
        in_specs=[pl.BlockSpec((tm, d), lambda j, i: (i, 0)), w_tile, w_tile,
                  pl.BlockSpec((1, th, d), lambda j, i: (layer, j, 0))],
        out_specs=[pl.BlockSpec((tm, th), lambda j, i: (i, j)), pl.BlockSpec((th, d), lambda j, i: (j, 0))],
        out_shape=[jax.ShapeDtypeStruct((m, hidden), BF16), jax.ShapeDtypeStruct((hidden, d), BF16)],
        scratch_shapes=[pltpu.VMEM((d, 2 * th), BF16)],
        compiler_params=_params("arbitrary", "arbitrary"),
        name="ffn_up",
    )(h, wg_stack, wu_stack, wd_stack)


def _ffn_down_kernel(a_ref, w_ref, x_ref, gpost_ref, gnext_ref, xo_ref, ho_ref):
    f = _dot(a_ref[...], w_ref[...])
    xn = x_ref[...] + _rms(f, gpost_ref[...])
    xo_ref[...] = xn
    ho_ref[...] = _rms(xn, gnext_ref[...]).astype(ho_ref.dtype)


def _ffn_down(act, wd, x, g_post, g_next, tm=512):
    m, d = x.shape
    hidden = act.shape[1]
    row = lambda i: (i, 0)
    const = lambda i: (0, 0)
    return pl.pallas_call(
        _ffn_down_kernel,
        grid=(m // tm,),
        in_specs=[pl.BlockSpec((tm, hidden), row),
                  pl.BlockSpec((hidden, d), const, pipeline_mode=pl.Buffered(1)),
                  pl.BlockSpec((tm, d), row),
                  pl.BlockSpec((1, d), const), pl.BlockSpec((1, d), const)],
        out_specs=[pl.BlockSpec((tm, d), row), pl.BlockSpec((tm, d), row)],
        out_shape=[jax.ShapeDtypeStruct((m, d), F32), jax.ShapeDtypeStruct((m, d), BF16)],
        compiler_params=pltpu.CompilerParams(dimension_semantics=("parallel",), vmem_limit_bytes=VMEM_LIMIT_FFN),
        name="ffn_down",
    )(act, wd, x, g_post.reshape(1, d), g_next.reshape(1, d))


def kernel(x, pre_mix_g, w_in, ret_gn_g, diff_lam_q1, diff_lam_k1, diff_lam_q2, diff_lam_k2, diff_subln_g,
           sgu_ln_g, sgu_ln_b, sgu_w, sgu_b, w_out, post_mix_g, pre_ffn_g, w_gate, w_up, w_down, post_ffn_g):
    batch, seq, d = x.shape
    depth = w_in.shape[0]
    xf = x.reshape(batch * seq, d)
    h = _rmsnorm(xf, pre_mix_g[0])
    for l in range(depth):
        lambda_init = 0.8 - 0.6 * math.exp(-0.3 * l)
        proj = _inproj(h, w_in, l)
        y_ret = _retention(proj, ret_gn_g[l], batch, seq)
        y_diff = _diffattn(proj, diff_lam_q1[l], diff_lam_k1[l], diff_lam_q2[l], diff_lam_k2[l],
                           diff_subln_g[l], lambda_init, batch, seq)
        y_sgu = _sgu(proj, sgu_ln_g[l], sgu_ln_b[l], sgu_w[l], sgu_b[l])
        xf, h = _outproj(y_ret, y_diff, y_sgu, w_out, l, xf, post_mix_g[l], pre_ffn_g[l])
        g_next = pre_mix_g[(l + 1) % depth]
        act, wd = _ffn_up(h, w_gate, w_up, w_down, l)
        xf, h = _ffn_down(act, wd, xf, post_ffn_g[l], g_next)
    return xf.reshape(batch, seq, d)
```

```python
import functools
import math

import numpy as np
import jax
import jax.numpy as jnp
from jax import lax
from jax.experimental import pallas as pl
from jax.experimental.pallas import tpu as pltpu

D_MODEL = 2048
HEAD_DIM = 128
RET_HEADS = 6
DIFF_HEADS = 6
SGU_GROUPS = 4
DIFF_MAP_DIM = 64
RET_W = RET_HEADS * HEAD_DIM
DIFF_W = DIFF_HEADS * HEAD_DIM
SGU_W = SGU_GROUPS * HEAD_DIM
IN_W = 4 * RET_W + 3 * DIFF_W + 2 * SGU_W
CHUNK = 128
EPS = 1e-6
LOG2E = math.log2(math.e)
NEG_BIG = -1e30
SUM_ROWS = 16
FFN_TILES = 11
QCHUNK = 256
RING = 2

COL_RQ, COL_RK, COL_RV, COL_RG = 0, 6, 12, 18
COL_DQ, COL_DK, COL_DV = 24, 30, 36
COL_SU, COL_SV = 42, 46

V7X_VMEM_BYTES = 64 * 1024 * 1024
VMEM_LIMIT = 56 * 1024 * 1024
VMEM_LIMIT_FFN = 60 * 1024 * 1024

BF16 = jnp.bfloat16
F32 = jnp.float32


def _dot(a, b):
    return jnp.dot(a, b, preferred_element_type=F32)


def _dot_nt(a, b):
    return lax.dot_general(a, b, (((1,), (1,)), ((), ())), preferred_element_type=F32)


def _dot_tn(a, b):
    return lax.dot_general(a, b, (((0,), (0,)), ((), ())), preferred_element_type=F32)


def _rms(x, g):
    return x * lax.rsqrt(jnp.mean(x * x, axis=-1, keepdims=True) + EPS) * g


def _params(*sem):
    return pltpu.CompilerParams(dimension_semantics=sem, vmem_limit_bytes=VMEM_LIMIT)


def _rmsnorm_kernel(x_ref, g_ref, o_ref):
    o_ref[...] = _rms(x_ref[...], g_ref[...]).astype(o_ref.dtype)


def _rmsnorm(x, g, tm=512):
    m, d = x.shape
    return pl.pallas_call(
        _rmsnorm_kernel,
        grid=(m // tm,),
        in_specs=[pl.BlockSpec((tm, d), lambda i: (i, 0)), pl.BlockSpec((1, d), lambda i: (0, 0))],
        out_specs=pl.BlockSpec((tm, d), lambda i: (i, 0)),
        out_shape=jax.ShapeDtypeStruct((m, d), BF16),
        compiler_params=_params("parallel"),
        name="rmsnorm",
    )(x, g.reshape(1, d))


def _inproj_kernel(h_ref, w_ref, o_ref, wb_ref):
    @pl.when(pl.program_id(1) == 0)
    def _():
        wb_ref[...] = w_ref[0].astype(BF16)

    o_ref[...] = _dot(h_ref[...], wb_ref[...]).astype(o_ref.dtype)


def _inproj(h, w_stack, layer, tm=1024, tn=1280):
    m, k = h.shape
    n = w_stack.shape[2]
    return pl.pallas_call(
        _inproj_kernel,
        grid=(n // tn, m // tm),
        in_specs=[pl.BlockSpec((tm, k), lambda j, i: (i, 0)),
                  pl.BlockSpec((1, k, tn), lambda j, i: (layer, 0, j))],
        out_specs=pl.BlockSpec((tm, tn), lambda j, i: (i, j)),
        out_shape=jax.ShapeDtypeStruct((m, n), BF16),
        scratch_shapes=[pltpu.VMEM((k, tn), BF16)],
        compiler_params=_params("arbitrary", "arbitrary"),
        name="inproj",
    )(h, w_stack)


def _retention_kernel(lg_ref, q_ref, k_ref, v_ref, g_ref, gn_ref, o_ref, r_ref, *, tb):
    h = pl.program_id(1)
    t = pl.program_id(2)

    @pl.when(t == 0)
    def _():
        r_ref[...] = jnp.zeros_like(r_ref)

    lg = lg_ref[h]
    scale = HEAD_DIM ** -0.5
    row = lax.broadcasted_iota(jnp.int32, (CHUNK, CHUNK), 0)
    col = lax.broadcasted_iota(jnp.int32, (CHUNK, CHUNK), 1)
    rel = (row - col).astype(F32)
    decay = jnp.where(rel >= 0, jnp.exp(lg * jnp.maximum(rel, 0.0)), 0.0) * scale
    pos = lax.broadcasted_iota(jnp.int32, (CHUNK, 1), 0).astype(F32)
    zeta = jnp.exp(lg * (CHUNK - 1.0 - pos)) * scale
    xi = jnp.exp(lg * (pos + 1.0))
    chunk_decay = jnp.exp(jnp.full((1, 1), lg * CHUNK, F32))
    gn = gn_ref[0]

    for c in range(tb // CHUNK):
        sl = slice(c * CHUNK, (c + 1) * CHUNK)
        q = q_ref[sl, :]
        k = k_ref[sl, :]
        v = v_ref[sl, :]
        scores = _dot_nt(q, k) * decay
        inner = _dot(scores.astype(BF16), v)
        r_prev = r_ref[...]
        cross = _dot(q, r_prev.astype(BF16)) * xi
        kz = (k.astype(F32) * zeta).astype(BF16)
        r_ref[...] = _dot_tn(kz, v) + chunk_decay * r_prev
        y = inner + cross
        mu = jnp.mean(y, axis=-1, keepdims=True)
        yc = y - mu
        var = jnp.mean(yc * yc, axis=-1, keepdims=True)
        y = yc * lax.rsqrt(var + EPS) * gn
        gate = g_ref[sl, :].astype(F32)
        o_ref[sl, :] = (gate * jax.nn.sigmoid(gate) * y).astype(o_ref.dtype)


def _retention(proj, gn_gain, batch, seq, tb=1024):
    m = proj.shape[0]
    nt = seq // tb
    log_gamma = np.log1p(-(2.0 ** (-5.0 - np.arange(RET_HEADS, dtype=np.float32)))).astype(np.float32)

    def blk(col):
        return pl.BlockSpec((tb, HEAD_DIM), lambda b, h, t: (b * nt + t, col + h))

    return pl.pallas_call(
        functools.partial(_retention_kernel, tb=tb),
        grid=(batch, RET_HEADS, nt),
        in_specs=[pl.BlockSpec(memory_space=pltpu.SMEM),
                  blk(COL_RQ), blk(COL_RK), blk(COL_RV), blk(COL_RG),
                  pl.BlockSpec((1, 1, HEAD_DIM), lambda b, h, t: (h, 0, 0))],
        out_specs=blk(0),
        out_shape=jax.ShapeDtypeStruct((m, RET_W), BF16),
        scratch_shapes=[pltpu.VMEM((HEAD_DIM, HEAD_DIM), F32)],
        compiler_params=_params("arbitrary", "arbitrary", "arbitrary"),
        name="retention",
    )(jnp.asarray(log_gamma), proj, proj, proj, proj, gn_gain.reshape(RET_HEADS, 1, HEAD_DIM))


def _diffattn_kernel(slope_ref, q_ref, k_ref, v_ref, lq1_ref, lk1_ref, lq2_ref, lk2_ref, sg_ref, o_ref,
                     k1_ref, k2_ref, vt_ref, *bufs, blk, nblk, lambda_init):
    acc = bufs[0:2]
    s_buf = [bufs[2 + 2 * k:4 + 2 * k] for k in range(RING)]
    p_buf = [bufs[2 + 2 * RING + 2 * k:4 + 2 * RING + 2 * k] for k in range(RING)]
    h = pl.program_id(1)
    qi = pl.program_id(2)
    lane = lax.broadcasted_iota(jnp.int32, (blk, HEAD_DIM), 1)

    @pl.when(qi == 0)
    def _():
        slope2 = slope_ref[h] * LOG2E

        def prep(j, carry):
            start = pl.multiple_of(j * blk, blk)
            kb = k_ref[pl.ds(start, blk), :].astype(F32)
            kpos = (lax.broadcasted_iota(jnp.int32, (blk, HEAD_DIM), 0) + start).astype(F32)
            bias = kpos * slope2
            hi = bias.astype(BF16).astype(F32)
            r1 = bias - hi
            mid = r1.astype(BF16).astype(F32)
            lo = r1 - mid
            zero = jnp.zeros_like(kb)
            k1 = jnp.where(lane < 64, kb,
                           jnp.where(lane == 64, hi, jnp.where(lane == 65, mid, jnp.where(lane == 66, lo, zero))))
            k2 = jnp.where(lane >= 64, kb,
                           jnp.where(lane == 0, hi, jnp.where(lane == 1, mid, jnp.where(lane == 2, lo, zero))))
            k1_ref[j] = k1.astype(BF16)
            k2_ref[j] = k2.astype(BF16)
            vt_ref[j, 0:HEAD_DIM, :] = v_ref[pl.ds(start, blk), :].astype(F32).T.astype(BF16)
            vt_ref[j, HEAD_DIM:, :] = jnp.ones((SUM_ROWS, blk), BF16)
            return carry

        lax.fori_loop(0, nblk, prep, 0)

    qs = q_ref[...].astype(F32) * (DIFF_MAP_DIM ** -0.5 * LOG2E)
    one = jnp.ones_like(qs)
    zero = jnp.zeros_like(qs)
    qa = (jnp.where(lane < 64, qs, jnp.where(lane < 67, one, zero)).astype(BF16),
          jnp.where(lane >= 64, qs, jnp.where(lane < 3, one, zero)).astype(BF16))
    ka = (k1_ref, k2_ref)

    def scores(mp, j):
        return _dot_nt(ka[mp][j], qa[mp])

    def pv_update(mp, j, p_in, alpha):
        acc[mp][...] = alpha * acc[mp][...] + _dot(vt_ref[j], p_in[mp][...])

    def stage(slot, t, nxt, carry):
        pj = jnp.where(t == 0, qi, t - 1)
        s_in, p_in = s_buf[slot], p_buf[slot]
        s_out, p_out = s_buf[(slot + 1) % RING], p_buf[(slot + 1) % RING]
        chunks = [slice(c * QCHUNK, (c + 1) * QCHUNK) for c in range(blk // QCHUNK)]
        ms, alphas, mbs, dep = carry[0:6:3], carry[1:6:3], carry[2:6:3], carry[6]
        mns = [jnp.maximum(ms[mp], mbs[mp]) for mp in range(2)]
        new_mb = [[], []]
        units = [(mp, cs) for mp in range(2) for cs in chunks]
        for mp, cs in units:
            if nxt is not None:
                s = _dot_nt(ka[mp][nxt], qa[mp][cs, :])
                s_out[mp][:, cs] = s
                new_mb[mp].append(jnp.max(s, axis=0, keepdims=True))
            p = jnp.exp2(s_in[mp][:, cs] - (mns[mp][:, cs] + dep))
            p_out[mp][:, cs] = p.astype(BF16)
        for mp, cs in units:
            acc[mp][:, cs] = alphas[mp][:, cs] * acc[mp][:, cs] + _dot(vt_ref[pj], p_in[mp][:, cs])
        out = []
        for mp in range(2):
            mb = jnp.concatenate(new_mb[mp], axis=1) if nxt is not None else mbs[mp]
            out += [mns[mp], jnp.exp2(ms[mp] - mns[mp]), mb]
        return tuple(out) + (p[0:1, :] * 0.0,)

    krow = lax.broadcasted_iota(jnp.int32, (blk, blk), 0)
    qcol = lax.broadcasted_iota(jnp.int32, (blk, blk), 1)
    causal = krow <= qcol
    carry = []
    for mp in range(2):
        s = jnp.where(causal, scores(mp, qi), NEG_BIG)
        m = jnp.max(s, axis=0, keepdims=True)
        p_buf[0][mp][...] = jnp.exp2(s - m).astype(BF16)
        acc[mp][...] = jnp.zeros((HEAD_DIM + SUM_ROWS, blk), F32)
        s = scores(mp, 0)
        s_buf[0][mp][...] = s
        carry += [m, jnp.ones_like(m), jnp.max(s, axis=0, keepdims=True)]
    carry = tuple(carry) + (jnp.zeros((1, QCHUNK), F32),)

    def unrolled(u, carry):
        t = RING * u
        for k in range(RING - 1):
            carry = stage(k, t + k, t + k + 1, carry)
        return stage(RING - 1, t + RING - 1, jnp.minimum(t + RING, qi - 1), carry)

    carry = lax.fori_loop(0, qi // RING, unrolled, carry)

    def make_tail(rem):
        def tail(carry):
            t = qi - rem
            for k in range(rem):
                carry = stage(k, t + k, t + k + 1 if k < rem - 1 else None, carry)
            for mp in range(2):
                pv_update(mp, jnp.maximum(qi - 1, 0), p_buf[rem % RING], carry[3 * mp + 1])
            return carry
        return tail

    lax.switch(qi % RING, [make_tail(rem) for rem in range(RING)], carry)

    lam = (jnp.exp(jnp.sum(lq1_ref[...] * lk1_ref[...], axis=-1, keepdims=True))
           - jnp.exp(jnp.sum(lq2_ref[...] * lk2_ref[...], axis=-1, keepdims=True)) + lambda_init)
    l1 = acc[0][HEAD_DIM:HEAD_DIM + 1, :]
    l2 = acc[1][HEAD_DIM:HEAD_DIM + 1, :]
    o_t = acc[0][0:HEAD_DIM, :] / l1 - lam * (acc[1][0:HEAD_DIM, :] / l2)
    o = o_t.T
    o = _rms(o, sg_ref[0]) * (1.0 - lambda_init)
    o_ref[...] = o.astype(o_ref.dtype)


def _diffattn(proj, lq1, lk1, lq2, lk2, subln_g, lambda_init, batch, seq, blk=512):
    m = proj.shape[0]
    nblk = seq // blk
    slopes = (2.0 ** (-8.0 * np.arange(1, DIFF_HEADS + 1, dtype=np.float32) / DIFF_HEADS)).astype(np.float32)
    vec = pl.BlockSpec((1, DIFF_MAP_DIM), lambda b, h, i: (0, 0))
    return pl.pallas_call(
        functools.partial(_diffattn_kernel, blk=blk, nblk=nblk, lambda_init=lambda_init),
        grid=(batch, DIFF_HEADS, nblk),
        in_specs=[pl.BlockSpec(memory_space=pltpu.SMEM),
                  pl.BlockSpec((blk, HEAD_DIM), lambda b, h, i: (b * nblk + i, COL_DQ + h)),
                  pl.BlockSpec((seq, HEAD_DIM), lambda b, h, i: (b, COL_DK + h)),
                  pl.BlockSpec((seq, HEAD_DIM), lambda b, h, i: (b, COL_DV + h)),
                  vec, vec, vec, vec,
                  pl.BlockSpec((1, 1, HEAD_DIM), lambda b, h, i: (h, 0, 0))],
        out_specs=pl.BlockSpec((blk, HEAD_DIM), lambda b, h, i: (b * nblk + i, h)),
        out_shape=jax.ShapeDtypeStruct((m, DIFF_W), BF16),
        scratch_shapes=[pltpu.VMEM((nblk, blk, HEAD_DIM), BF16),
                        pltpu.VMEM((nblk, blk, HEAD_DIM), BF16),
                        pltpu.VMEM((nblk, HEAD_DIM + SUM_ROWS, blk), BF16),
                        *[pltpu.VMEM((HEAD_DIM + SUM_ROWS, blk), F32)] * 2,
                        *[pltpu.VMEM((blk, blk), F32)] * (2 * RING),
                        *[pltpu.VMEM((blk, blk), BF16)] * (2 * RING)],
        compiler_params=_params("arbitrary", "arbitrary", "arbitrary"),
        name="diffattn",
    )(jnp.asarray(slopes), proj, proj, proj,
      lq1.reshape(1, -1), lk1.reshape(1, -1), lq2.reshape(1, -1), lk2.reshape(1, -1),
      subln_g.reshape(DIFF_HEADS, 1, HEAD_DIM))


def _sgu_kernel(u_ref, v_ref, lng_ref, lnb_ref, w_ref, b_ref, o_ref, *, tb):
    row = lax.broadcasted_iota(jnp.int32, (CHUNK, CHUNK), 0)
    col = lax.broadcasted_iota(jnp.int32, (CHUNK, CHUNK), 1)
    w = jnp.where(row >= col, w_ref[0], 0.0).astype(BF16)
    bias = b_ref[0]
    lng = lng_ref[0]
    lnb = lnb_ref[0]
    for c in range(tb // CHUNK):
        sl = slice(c * CHUNK, (c + 1) * CHUNK)
        u = jax.nn.gelu(u_ref[sl, :].astype(F32))
        v = jax.nn.gelu(v_ref[sl, :].astype(F32))
        mu = jnp.mean(v, axis=-1, keepdims=True)
        vc = v - mu
        var = jnp.mean(vc * vc, axis=-1, keepdims=True)
        vn = vc * lax.rsqrt(var + EPS) * lng + lnb
        mixed = _dot(w, vn.astype(BF16)) + bias
        o_ref[sl, :] = (u * mixed).astype(o_ref.dtype)


def _sgu(proj, ln_g, ln_b, w_s, b_s, tb=1024):
    m = proj.shape[0]
    g = SGU_GROUPS
    per_group = lambda i, j: (j, 0, 0)
    return pl.pallas_call(
        functools.partial(_sgu_kernel, tb=tb),
        grid=(m // tb, g),
        in_specs=[pl.BlockSpec((tb, HEAD_DIM), lambda i, j: (i, COL_SU + j)),
                  pl.BlockSpec((tb, HEAD_DIM), lambda i, j: (i, COL_SV + j)),
                  pl.BlockSpec((1, 1, HEAD_DIM), per_group),
                  pl.BlockSpec((1, 1, HEAD_DIM), per_group),
                  pl.BlockSpec((1, CHUNK, CHUNK), per_group),
                  pl.BlockSpec((1, CHUNK, 1), per_group)],
        out_specs=pl.BlockSpec((tb, HEAD_DIM), lambda i, j: (i, j)),
        out_shape=jax.ShapeDtypeStruct((m, SGU_W), BF16),
        compiler_params=_params("parallel", "arbitrary"),
        name="sgu",
    )(proj, proj, ln_g.reshape(g, 1, HEAD_DIM), ln_b.reshape(g, 1, HEAD_DIM), w_s, b_s.reshape(g, CHUNK, 1))


def _outproj_kernel(yr_ref, yd_ref, ys_ref, wf_ref, x_ref, gpost_ref, gnext_ref, xo_ref, ho_ref, w_ref):
    @pl.when(pl.program_id(0) == 0)
    def _():
        w_ref[...] = wf_ref[0].astype(BF16)

    mix = (_dot(yr_ref[...], w_ref[0:RET_W, :])
           + _dot(yd_ref[...], w_ref[RET_W:RET_W + DIFF_W, :])
           + _dot(ys_ref[...], w_ref[RET_W + DIFF_W:D_MODEL, :]))
    xn = x_ref[...] + _rms(mix, gpost_ref[...])
    xo_ref[...] = xn
    ho_ref[...] = _rms(xn, gnext_ref[...]).astype(ho_ref.dtype)


def _outproj(y_ret, y_diff, y_sgu, w_stack, layer, x, g_post, g_next, tm=512):
    m, d = x.shape
    row = lambda i: (i, 0)
    const = lambda i: (0, 0)
    return pl.pallas_call(
        _outproj_kernel,
        grid=(m // tm,),
        in_specs=[pl.BlockSpec((tm, RET_W), row), pl.BlockSpec((tm, DIFF_W), row), pl.BlockSpec((tm, SGU_W), row),
                  pl.BlockSpec((1, d, d), lambda i: (layer, 0, 0), pipeline_mode=pl.Buffered(1)),
                  pl.BlockSpec((tm, d), row),
                  pl.BlockSpec((1, d), const), pl.BlockSpec((1, d), const)],
        out_specs=[pl.BlockSpec((tm, d), row), pl.BlockSpec((tm, d), row)],
        out_shape=[jax.ShapeDtypeStruct((m, d), F32), jax.ShapeDtypeStruct((m, d), BF16)],
        scratch_shapes=[pltpu.VMEM((d, d), BF16)],
        compiler_params=_params("arbitrary"),
        name="outproj",
    )(y_ret, y_diff, y_sgu, w_stack, x, g_post.reshape(1, d), g_next.reshape(1, d))


def _ffn_up_kernel(h_ref, wg_ref, wu_ref, wd_ref, a_ref, wdb_ref, w_ref):
    th = wg_ref.shape[2]

    @pl.when(pl.program_id(1) == 0)
    def _():
        w_ref[:, 0:th] = wg_ref[0].astype(BF16)
        w_ref[:, th:] = wu_ref[0].astype(BF16)
        wdb_ref[...] = wd_ref[0].astype(BF16)

    gu = _dot(h_ref[...], w_ref[...])
    gate = gu[:, 0:th]
    a_ref[...] = (gate * jax.nn.sigmoid(gate) * gu[:, th:]).astype(a_ref.dtype)


def _ffn_up(h, wg_stack, wu_stack, wd_stack, layer, tm=1024):
    m, d = h.shape
    hidden = wg_stack.shape[2]
    th = hidden // FFN_TILES
    w_tile = pl.BlockSpec((1, d, th), lambda j, i: (layer, 0, j))
    return pl.pallas_call(
        _ffn_up_kernel,
        grid=(FFN_TILES, m // tm),
        in_specs=[pl.BlockSpec((tm, d), lambda j, i: (i, 0)), w_tile, w_tile,
                  pl.BlockSpec((1, th, d), lambda j, i: (layer, j, 0))],
        out_specs=[pl.BlockSpec((tm, th), lambda j, i: (i, j)), pl.BlockSpec((th, d), lambda j, i: (j, 0))],
        out_shape=[jax.ShapeDtypeStruct((m, hidden), BF16), jax.ShapeDtypeStruct((hidden, d), BF16)],
        scratch_shapes=[pltpu.VMEM((d, 2 * th), BF16)],
        compiler_params=_params("arbitrary", "arbitrary"),
        name="ffn_up",
    )(h, wg_stack, wu_stack, wd_stack)


def _ffn_down_kernel(a_ref, w_ref, x_ref, gpost_ref, gnext_ref, xo_ref, ho_ref):
    f = _dot(a_ref[...], w_ref[...])
    xn = x_ref[...] + _rms(f, gpost_ref[...])
    xo_ref[...] = xn
    ho_ref[...] = _rms(xn, gnext_ref[...]).astype(ho_ref.dtype)


def _ffn_down(act, wd, x, g_post, g_next, tm=512):
    m, d = x.shape
    hidden = act.shape[1]
    row = lambda i: (i, 0)
    const = lambda i: (0, 0)
    return pl.pallas_call(
        _ffn_down_kernel,
        grid=(m // tm,),
        in_specs=[pl.BlockSpec((tm, hidden), row),
                  pl.BlockSpec((hidden, d), const, pipeline_mode=pl.Buffered(1)),
                  pl.BlockSpec((tm, d), row),
                  pl.BlockSpec((1, d), const), pl.BlockSpec((1, d), const)],
        out_specs=[pl.BlockSpec((tm, d), row), pl.BlockSpec((tm, d), row)],
        out_shape=[jax.ShapeDtypeStruct((m, d), F32), jax.ShapeDtypeStruct((m, d), BF16)],
        compiler_params=pltpu.CompilerParams(dimension_semantics=("parallel",), vmem_limit_bytes=VMEM_LIMIT_FFN),
        name="ffn_down",
    )(act, wd, x, g_post.reshape(1, d), g_next.reshape(1, d))


def kernel(x, pre_mix_g, w_in, ret_gn_g, diff_lam_q1, diff_lam_k1, diff_lam_q2, diff_lam_k2, diff_subln_g,
           sgu_ln_g, sgu_ln_b, sgu_w, sgu_b, w_out, post_mix_g, pre_ffn_g, w_gate, w_up, w_down, post_ffn_g):
    batch, seq, d = x.shape
    depth = w_in.shape[0]
    xf = x.reshape(batch * seq, d)
    h = _rmsnorm(xf, pre_mix_g[0])
    for l in range(depth):
        lambda_init = 0.8 - 0.6 * math.exp(-0.3 * l)
        proj = _inproj(h, w_in, l)
        y_ret = _retention(proj, ret_gn_g[l], batch, seq)
        y_diff = _diffattn(proj, diff_lam_q1[l], diff_lam_k1[l], diff_lam_q2[l], diff_lam_k2[l],
                           diff_subln_g[l], lambda_init, batch, seq)
        y_sgu = _sgu(proj, sgu_ln_g[l], sgu_ln_b[l], sgu_w[l], sgu_b[l])
        xf, h = _outproj(y_ret, y_diff, y_sgu, w_out, l, xf, post_mix_g[l], pre_ffn_g[l])
        g_next = pre_mix_g[(l + 1) % depth]
        act, wd = _ffn_up(h, w_gate, w_up, w_down, l)
        xf, h = _ffn_down(act, wd, xf, post_ffn_g[l], g_next)
    return xf.reshape(batch, seq, d)
```

```python
import functools
import math

import numpy as np
import jax
import jax.numpy as jnp
from jax import lax
from jax.experimental import pallas as pl
from jax.experimental.pallas import tpu as pltpu

D_MODEL = 2048
HEAD_DIM = 128
RET_HEADS = 6
DIFF_HEADS = 6
SGU_GROUPS = 4
DIFF_MAP_DIM = 64
RET_W = RET_HEADS * HEAD_DIM
DIFF_W = DIFF_HEADS * HEAD_DIM
SGU_W = SGU_GROUPS * HEAD_DIM
IN_W = 4 * RET_W + 3 * DIFF_W + 2 * SGU_W
CHUNK = 128
EPS = 1e-6
LOG2E = math.log2(math.e)
NEG_BIG = -1e30
SUM_ROWS = 16
FFN_TILES = 11
QCHUNK = 256

COL_RQ, COL_RK, COL_RV, COL_RG = 0, 6, 12, 18
COL_DQ, COL_DK, COL_DV = 24, 30, 36
COL_SU, COL_SV = 42, 46

V7X_VMEM_BYTES = 64 * 1024 * 1024
VMEM_LIMIT = 56 * 1024 * 1024
VMEM_LIMIT_FFN = 60 * 1024 * 1024

BF16 = jnp.bfloat16
F32 = jnp.float32


def _dot(a, b):
    return jnp.dot(a, b, preferred_element_type=F32)


def _dot_nt(a, b):
    return lax.dot_general(a, b, (((1,), (1,)), ((), ())), preferred_element_type=F32)


def _dot_tn(a, b):
    return lax.dot_general(a, b, (((0,), (0,)), ((), ())), preferred_element_type=F32)


def _rms(x, g):
    return x * lax.rsqrt(jnp.mean(x * x, axis=-1, keepdims=True) + EPS) * g


def _params(*sem):
    return pltpu.CompilerParams(dimension_semantics=sem, vmem_limit_bytes=VMEM_LIMIT)


def _rmsnorm_kernel(x_ref, g_ref, o_ref):
    o_ref[...] = _rms(x_ref[...], g_ref[...]).astype(o_ref.dtype)


def _rmsnorm(x, g, tm=512):
    m, d = x.shape
    return pl.pallas_call(
        _rmsnorm_kernel,
        grid=(m // tm,),
        in_specs=[pl.BlockSpec((tm, d), lambda i: (i, 0)), pl.BlockSpec((1, d), lambda i: (0, 0))],
        out_specs=pl.BlockSpec((tm, d), lambda i: (i, 0)),
        out_shape=jax.ShapeDtypeStruct((m, d), BF16),
        compiler_params=_params("parallel"),
        name="rmsnorm",
    )(x, g.reshape(1, d))


def _inproj_kernel(h_ref, w_ref, o_ref, wb_ref):
    @pl.when(pl.program_id(1) == 0)
    def _():
        wb_ref[...] = w_ref[0].astype(BF16)

    o_ref[...] = _dot(h_ref[...], wb_ref[...]).astype(o_ref.dtype)


def _inproj(h, w_stack, layer, tm=1024, tn=1280):
    m, k = h.shape
    n = w_stack.shape[2]
    return pl.pallas_call(
        _inproj_kernel,
        grid=(n // tn, m // tm),
        in_specs=[pl.BlockSpec((tm, k), lambda j, i: (i, 0)),
                  pl.BlockSpec((1, k, tn), lambda j, i: (layer, 0, j))],
        out_specs=pl.BlockSpec((tm, tn), lambda j, i: (i, j)),
        out_shape=jax.ShapeDtypeStruct((m, n), BF16),
        scratch_shapes=[pltpu.VMEM((k, tn), BF16)],
        compiler_params=_params("arbitrary", "arbitrary"),
        name="inproj",
    )(h, w_stack)


def _retention_kernel(lg_ref, q_ref, k_ref, v_ref, g_ref, gn_ref, o_ref, r_ref, *, tb):
    h = pl.program_id(1)
    t = pl.program_id(2)

    @pl.when(t == 0)
    def _():
        r_ref[...] = jnp.zeros_like(r_ref)

    lg = lg_ref[h]
    scale = HEAD_DIM ** -0.5
    row = lax.broadcasted_iota(jnp.int32, (CHUNK, CHUNK), 0)
    col = lax.broadcasted_iota(jnp.int32, (CHUNK, CHUNK), 1)
    rel = (row - col).astype(F32)
    decay = jnp.where(rel >= 0, jnp.exp(lg * jnp.maximum(rel, 0.0)), 0.0) * scale
    pos = lax.broadcasted_iota(jnp.int32, (CHUNK, 1), 0).astype(F32)
    zeta = jnp.exp(lg * (CHUNK - 1.0 - pos)) * scale
    xi = jnp.exp(lg * (pos + 1.0))
    chunk_decay = jnp.exp(jnp.full((1, 1), lg * CHUNK, F32))
    gn = gn_ref[0]

    for c in range(tb // CHUNK):
        sl = slice(c * CHUNK, (c + 1) * CHUNK)
        q = q_ref[sl, :]
        k = k_ref[sl, :]
        v = v_ref[sl, :]
        scores = _dot_nt(q, k) * decay
        inner = _dot(scores.astype(BF16), v)
        r_prev = r_ref[...]
        cross = _dot(q, r_prev.astype(BF16)) * xi
        kz = (k.astype(F32) * zeta).astype(BF16)
        r_ref[...] = _dot_tn(kz, v) + chunk_decay * r_prev
        y = inner + cross
        mu = jnp.mean(y, axis=-1, keepdims=True)
        yc = y - mu
        var = jnp.mean(yc * yc, axis=-1, keepdims=True)
        y = yc * lax.rsqrt(var + EPS) * gn
        gate = g_ref[sl, :].astype(F32)
        o_ref[sl, :] = (gate * jax.nn.sigmoid(gate) * y).astype(o_ref.dtype)


def _retention(proj, gn_gain, batch, seq, tb=1024):
    m = proj.shape[0]
    nt = seq // tb
    log_gamma = np.log1p(-(2.0 ** (-5.0 - np.arange(RET_HEADS, dtype=np.float32)))).astype(np.float32)

    def blk(col):
        return pl.BlockSpec((tb, HEAD_DIM), lambda b, h, t: (b * nt + t, col + h))

    return pl.pallas_call(
        functools.partial(_retention_kernel, tb=tb),
        grid=(batch, RET_HEADS, nt),
        in_specs=[pl.BlockSpec(memory_space=pltpu.SMEM),
                  blk(COL_RQ), blk(COL_RK), blk(COL_RV), blk(COL_RG),
                  pl.BlockSpec((1, 1, HEAD_DIM), lambda b, h, t: (h, 0, 0))],
        out_specs=blk(0),
        out_shape=jax.ShapeDtypeStruct((m, RET_W), BF16),
        scratch_shapes=[pltpu.VMEM((HEAD_DIM, HEAD_DIM), F32)],
        compiler_params=_params("arbitrary", "arbitrary", "arbitrary"),
        name="retention",
    )(jnp.asarray(log_gamma), proj, proj, proj, proj, gn_gain.reshape(RET_HEADS, 1, HEAD_DIM))


def _diffattn_kernel(slope_ref, q_ref, k_ref, v_ref, lq1_ref, lk1_ref, lq2_ref, lk2_ref, sg_ref, o_ref,
                     k1_ref, k2_ref, vt_ref, *bufs, bq, bk, lambda_init):
    acc = bufs[0:2]
    s_buf = [bufs[2:4], bufs[4:6]]
    p_buf = [bufs[6:8], bufs[8:10]]
    h = pl.program_id(1)
    qi = pl.program_id(2)
    nkb = k1_ref.shape[0]
    per_q = bq // bk

    @pl.when(qi == 0)
    def _():
        slope2 = slope_ref[h] * LOG2E
        lane = lax.broadcasted_iota(jnp.int32, (bk, HEAD_DIM), 1)

        def prep(j, carry):
            start = pl.multiple_of(j * bk, bk)
            kb = k_ref[pl.ds(start, bk), :].astype(F32)
            kpos = (lax.broadcasted_iota(jnp.int32, (bk, HEAD_DIM), 0) + start).astype(F32)
            bias = kpos * slope2
            hi = bias.astype(BF16).astype(F32)
            r1 = bias - hi
            mid = r1.astype(BF16).astype(F32)
            lo = r1 - mid
            zero = jnp.zeros_like(kb)
            k1 = jnp.where(lane < 64, kb,
                           jnp.where(lane == 64, hi, jnp.where(lane == 65, mid, jnp.where(lane == 66, lo, zero))))
            k2 = jnp.where(lane >= 64, kb,
                           jnp.where(lane == 0, hi, jnp.where(lane == 1, mid, jnp.where(lane == 2, lo, zero))))
            k1_ref[j] = k1.astype(BF16)
            k2_ref[j] = k2.astype(BF16)
            vt_ref[j, 0:HEAD_DIM, :] = v_ref[pl.ds(start, bk), :].astype(F32).T.astype(BF16)
            vt_ref[j, HEAD_DIM:, :] = jnp.ones((SUM_ROWS, bk), BF16)
            return carry

        lax.fori_loop(0, nkb, prep, 0)

    lane = lax.broadcasted_iota(jnp.int32, (bq, HEAD_DIM), 1)
    qs = q_ref[...].astype(F32) * (DIFF_MAP_DIM ** -0.5 * LOG2E)
    one = jnp.ones_like(qs)
    zero = jnp.zeros_like(qs)
    qa = (jnp.where(lane < 64, qs, jnp.where(lane < 67, one, zero)).astype(BF16),
          jnp.where(lane >= 64, qs, jnp.where(lane < 3, one, zero)).astype(BF16))
    ka = (k1_ref, k2_ref)
    chunks = [slice(c * QCHUNK, (c + 1) * QCHUNK) for c in range(bq // QCHUNK)]
    krow = lax.broadcasted_iota(jnp.int32, (bk, QCHUNK), 0)
    qcol = lax.broadcasted_iota(jnp.int32, (bk, QCHUNK), 1)

    def issue_scores(mp, j, s_out, diag=None):
        mbs = []
        for cs in chunks:
            first_key = 0 if diag is None else diag * bk
            if first_key >= cs.stop:
                mbs.append(jnp.full((1, QCHUNK), 2 * NEG_BIG, F32))
                continue
            s = _dot_nt(ka[mp][j], qa[mp][cs, :])
            if diag is not None and first_key + bk - 1 > cs.start:
                s = jnp.where(krow + first_key <= qcol + cs.start, s, 2 * NEG_BIG)
            s_out[mp][:, cs] = s
            mbs.append(jnp.max(s, axis=0, keepdims=True))
        return jnp.concatenate(mbs, axis=1)

    def live(cs, diag):
        return diag is None or diag * bk < cs.stop

    def stage(slot, carry, nxt, pending, cur_diag=None, nxt_diag=None, pend_diag=None, first=False):
        s_in, p_in = s_buf[slot], p_buf[slot]
        s_out, p_out = s_buf[1 - slot], p_buf[1 - slot]
        ms, alphas, mbs, dep = carry[0:6:3], carry[1:6:3], carry[2:6:3], carry[6]
        mns = [jnp.maximum(ms[mp], mbs[mp]) for mp in range(2)]
        new_mb = []
        for mp in range(2):
            new_mb.append(issue_scores(mp, nxt, s_out, nxt_diag))
            for cs in chunks:
                if live(cs, cur_diag):
                    p = jnp.exp2(s_in[mp][:, cs] - (mns[mp][:, cs] + dep))
                    p_out[mp][:, cs] = p.astype(BF16)
                else:
                    p_out[mp][:, cs] = jnp.zeros((bk, QCHUNK), BF16)
        if pending is not None:
            for mp in range(2):
                for cs in chunks:
                    if live(cs, pend_diag):
                        pv = _dot(vt_ref[pending], p_in[mp][:, cs])
                        acc[mp][:, cs] = pv if first else alphas[mp][:, cs] * acc[mp][:, cs] + pv
        out = []
        for mp in range(2):
            out += [mns[mp], jnp.exp2(ms[mp] - mns[mp]), new_mb[mp]]
        return tuple(out) + (p[0:1, :] * 0.0,)

    d0 = per_q * qi
    nfull = per_q * qi
    carry = []
    for mp in range(2):
        carry += [jnp.full((1, bq), NEG_BIG, F32), jnp.ones((1, bq), F32), issue_scores(mp, d0, s_buf[0], diag=0)]
    carry = tuple(carry) + (jnp.zeros((1, QCHUNK), F32),)
    carry = stage(0, carry, d0 + 1, None, cur_diag=0, nxt_diag=1)
    carry = stage(1, carry, 0, d0, cur_diag=1, pend_diag=0, first=True)

    def unrolled(u, carry):
        t = 2 * u
        carry = stage(0, carry, t + 1, jnp.where(t == 0, d0 + 1, t - 1))
        return stage(1, carry, jnp.minimum(t + 2, nfull - 1), t)

    carry = lax.fori_loop(0, nfull // 2, unrolled, carry)
    last = jnp.where(qi == 0, 1, nfull - 1)
    for mp in range(2):
        for cs in chunks:
            acc[mp][:, cs] = carry[3 * mp + 1][:, cs] * acc[mp][:, cs] + _dot(vt_ref[last], p_buf[0][mp][:, cs])

    lam = (jnp.exp(jnp.sum(lq1_ref[...] * lk1_ref[...], axis=-1, keepdims=True))
           - jnp.exp(jnp.sum(lq2_ref[...] * lk2_ref[...], axis=-1, keepdims=True)) + lambda_init)
    l1 = acc[0][HEAD_DIM:HEAD_DIM + 1, :]
    l2 = acc[1][HEAD_DIM:HEAD_DIM + 1, :]
    o_t = acc[0][0:HEAD_DIM, :] / l1 - lam * (acc[1][0:HEAD_DIM, :] / l2)
    o = o_t.T
    o = _rms(o, sg_ref[0]) * (1.0 - lambda_init)
    o_ref[...] = o.astype(o_ref.dtype)


def _diffattn(proj, lq1, lk1, lq2, lk2, subln_g, lambda_init, batch, seq, bq=1024, bk=512):
    m = proj.shape[0]
    nq, nk = seq // bq, seq // bk
    slopes = (2.0 ** (-8.0 * np.arange(1, DIFF_HEADS + 1, dtype=np.float32) / DIFF_HEADS)).astype(np.float32)
    vec = pl.BlockSpec((1, DIFF_MAP_DIM), lambda b, h, i: (0, 0))
    return pl.pallas_call(
        functools.partial(_diffattn_kernel, bq=bq, bk=bk, lambda_init=lambda_init),
        grid=(batch, DIFF_HEADS, nq),
        in_specs=[pl.BlockSpec(memory_space=pltpu.SMEM),
                  pl.BlockSpec((bq, HEAD_DIM), lambda b, h, i: (b * nq + i, COL_DQ + h)),
                  pl.BlockSpec((seq, HEAD_DIM), lambda b, h, i: (b, COL_DK + h)),
                  pl.BlockSpec((seq, HEAD_DIM), lambda b, h, i: (b, COL_DV + h)),
                  vec, vec, vec, vec,
                  pl.BlockSpec((1, 1, HEAD_DIM), lambda b, h, i: (h, 0, 0))],
        out_specs=pl.BlockSpec((bq, HEAD_DIM), lambda b, h, i: (b * nq + i, h)),
        out_shape=jax.ShapeDtypeStruct((m, DIFF_W), BF16),
        scratch_shapes=[pltpu.VMEM((nk, bk, HEAD_DIM), BF16),
                        pltpu.VMEM((nk, bk, HEAD_DIM), BF16),
                        pltpu.VMEM((nk, HEAD_DIM + SUM_ROWS, bk), BF16),
                        *[pltpu.VMEM((HEAD_DIM + SUM_ROWS, bq), F32)] * 2,
                        *[pltpu.VMEM((bk, bq), F32)] * 4,
                        *[pltpu.VMEM((bk, bq), BF16)] * 4],
        compiler_params=_params("arbitrary", "arbitrary", "arbitrary"),
        name="diffattn",
    )(jnp.asarray(slopes), proj, proj, proj,
      lq1.reshape(1, -1), lk1.reshape(1, -1), lq2.reshape(1, -1), lk2.reshape(1, -1),
      subln_g.reshape(DIFF_HEADS, 1, HEAD_DIM))


def _sgu_kernel(u_ref, v_ref, lng_ref, lnb_ref, w_ref, b_ref, o_ref, *, tb):
    row = lax.broadcasted_iota(jnp.int32, (CHUNK, CHUNK), 0)
    col = lax.broadcasted_iota(jnp.int32, (CHUNK, CHUNK), 1)
    w = jnp.where(row >= col, w_ref[0], 0.0).astype(BF16)
    bias = b_ref[0]
    lng = lng_ref[0]
    lnb = lnb_ref[0]
    for c in range(tb // CHUNK):
        sl = slice(c * CHUNK, (c + 1) * CHUNK)
        u = jax.nn.gelu(u_ref[sl, :].astype(F32))
        v = jax.nn.gelu(v_ref[sl, :].astype(F32))
        mu = jnp.mean(v, axis=-1, keepdims=True)
        vc = v - mu
        var = jnp.mean(vc * vc, axis=-1, keepdims=True)
        vn = vc * lax.rsqrt(var + EPS) * lng + lnb
        mixed = _dot(w, vn.astype(BF16)) + bias
        o_ref[sl, :] = (u * mixed).astype(o_ref.dtype)


def _sgu(proj, ln_g, ln_b, w_s, b_s, tb=1024):
    m = proj.shape[0]
    g = SGU_GROUPS
    per_group = lambda i, j: (j, 0, 0)
    return pl.pallas_call(
        functools.partial(_sgu_kernel, tb=tb),
        grid=(m // tb, g),
        in_specs=[pl.BlockSpec((tb, HEAD_DIM), lambda i, j: (i, COL_SU + j)),
                  pl.BlockSpec((tb, HEAD_DIM), lambda i, j: (i, COL_SV + j)),
                  pl.BlockSpec((1, 1, HEAD_DIM), per_group),
                  pl.BlockSpec((1, 1, HEAD_DIM), per_group),
                  pl.BlockSpec((1, CHUNK, CHUNK), per_group),
                  pl.BlockSpec((1, CHUNK, 1), per_group)],
        out_specs=pl.BlockSpec((tb, HEAD_DIM), lambda i, j: (i, j)),
        out_shape=jax.ShapeDtypeStruct((m, SGU_W), BF16),
        compiler_params=_params("parallel", "arbitrary"),
        name="sgu",
    )(proj, proj, ln_g.reshape(g, 1, HEAD_DIM), ln_b.reshape(g, 1, HEAD_DIM), w_s, b_s.reshape(g, CHUNK, 1))


def _outproj_kernel(yr_ref, yd_ref, ys_ref, wf_ref, x_ref, gpost_ref, gnext_ref, xo_ref, ho_ref, w_ref):
    @pl.when(pl.program_id(0) == 0)
    def _():
        w_ref[...] = wf_ref[0].astype(BF16)

    mix = (_dot(yr_ref[...], w_ref[0:RET_W, :])
           + _dot(yd_ref[...], w_ref[RET_W:RET_W + DIFF_W, :])
           + _dot(ys_ref[...], w_ref[RET_W + DIFF_W:D_MODEL, :]))
    xn = x_ref[...] + _rms(mix, gpost_ref[...])
    xo_ref[...] = xn
    ho_ref[...] = _rms(xn, gnext_ref[...]).astype(ho_ref.dtype)


def _outproj(y_ret, y_diff, y_sgu, w_stack, layer, x, g_post, g_next, tm=512):
    m, d = x.shape
    row = lambda i: (i, 0)
    const = lambda i: (0, 0)
    return pl.pallas_call(
        _outproj_kernel,
        grid=(m // tm,),
        in_specs=[pl.BlockSpec((tm, RET_W), row), pl.BlockSpec((tm, DIFF_W), row), pl.BlockSpec((tm, SGU_W), row),
                  pl.BlockSpec((1, d, d), lambda i: (layer, 0, 0), pipeline_mode=pl.Buffered(1)),
                  pl.BlockSpec((tm, d), row),
                  pl.BlockSpec((1, d), const), pl.BlockSpec((1, d), const)],
        out_specs=[pl.BlockSpec((tm, d), row), pl.BlockSpec((tm, d), row)],
        out_shape=[jax.ShapeDtypeStruct((m, d), F32), jax.ShapeDtypeStruct((m, d), BF16)],
        scratch_shapes=[pltpu.VMEM((d, d), BF16)],
        compiler_params=_params("arbitrary"),
        name="outproj",
    )(y_ret, y_diff, y_sgu, w_stack, x, g_post.reshape(1, d), g_next.reshape(1, d))


def _ffn_up_kernel(h_ref, wg_ref, wu_ref, wd_ref, a_ref, wdb_ref, w_ref):
    th = wg_ref.shape[2]

    @pl.when(pl.program_id(1) == 0)
    def _():
        w_ref[:, 0:th] = wg_ref[0].astype(BF16)
        w_ref[:, th:] = wu_ref[0].astype(BF16)
        wdb_ref[...] = wd_ref[0].astype(BF16)

    gu = _dot(h_ref[...], w_ref[...])
    gate = gu[:, 0:th]
    a_ref[...] = (gate * jax.nn.sigmoid(gate) * gu[:, th:]).astype(a_ref.dtype)


def _ffn_up(h, wg_stack, wu_stack, wd_stack, layer, tm=1024):
    m, d = h.shape
    hidden = wg_stack.shape[2]
    th = hidden // FFN_TILES
    w_tile = pl.BlockSpec((1, d, th), lambda j, i: (layer, 0, j))
    return pl.pallas_call(
        _ffn_up_kernel,
        grid=(FFN_TILES, m // tm),
        in_specs=[pl.BlockSpec((tm, d), lambda j, i: (i, 0)), w_tile, w_tile,
                  pl.BlockSpec((1, th, d), lambda j, i: (layer, j, 0))],
        out_specs=[pl.BlockSpec((tm, th), lambda j, i: (i, j)), pl.BlockSpec((th, d), lambda j, i: (j, 0))],
        out_shape=[jax.ShapeDtypeStruct((m, hidden), BF16), jax.ShapeDtypeStruct((hidden, d), BF16)],
        scratch_shapes=[pltpu.VMEM((d, 2 * th), BF16)],
        compiler_params=_params("arbitrary", "arbitrary"),
        name="ffn_up",
    )(h, wg_stack, wu_stack, wd_stack)


def _ffn_down_kernel(a_ref, w_ref, x_ref, gpost_ref, gnext_ref, xo_ref, ho_ref):
    f = _dot(a_ref[...], w_ref[...])
    xn = x_ref[...] + _rms(f, gpost_ref[...])
    xo_ref[...] = xn
    ho_ref[...] = _rms(xn, gnext_ref[...]).astype(ho_ref.dtype)


def _ffn_down(act, wd, x, g_post, g_next, tm=512):
    m, d = x.shape
    hidden = act.shape[1]
    row = lambda i: (i, 0)
    const = lambda i: (0, 0)
    return pl.pallas_call(
        _ffn_down_kernel,
        grid=(m // tm,),
        in_specs=[pl.BlockSpec((tm, hidden), row),
                  pl.BlockSpec((hidden, d), const, pipeline_mode=pl.Buffered(1)),
                  pl.BlockSpec((tm, d), row),
                  pl.BlockSpec((1, d), const), pl.BlockSpec((1, d), const)],
        out_specs=[pl.BlockSpec((tm, d), row), pl.BlockSpec((tm, d), row)],
        out_shape=[jax.ShapeDtypeStruct((m, d), F32), jax.ShapeDtypeStruct((m, d), BF16)],
        compiler_params=pltpu.CompilerParams(dimension_semantics=("parallel",), vmem_limit_bytes=VMEM_LIMIT_FFN),
        name="ffn_down",
    )(act, wd, x, g_post.reshape(1, d), g_next.reshape(1, d))


def kernel(x, pre_mix_g, w_in, ret_gn_g, diff_lam_q1, diff_lam_k1, diff_lam_q2, diff_lam_k2, diff_subln_g,
           sgu_ln_g, sgu_ln_b, sgu_w, sgu_b, w_out, post_mix_g, pre_ffn_g, w_gate, w_up, w_down, post_ffn_g):
    batch, seq, d = x.shape
    depth = w_in.shape[0]
    xf = x.reshape(batch * seq, d)
    h = _rmsnorm(xf, pre_mix_g[0])
    for l in range(depth):
        lambda_init = 0.8 - 0.6 * math.exp(-0.3 * l)
        proj = _inproj(h, w_in, l)
        y_ret = _retention(proj, ret_gn_g[l], batch, seq)
        y_diff = _diffattn(proj, diff_lam_q1[l], diff_lam_k1[l], diff_lam_q2[l], diff_lam_k2[l],
                           diff_subln_g[l], lambda_init, batch, seq)
        y_sgu = _sgu(proj, sgu_ln_g[l], sgu_ln_b[l], sgu_w[l], sgu_b[l])
        xf, h = _outproj(y_ret, y_diff, y_sgu, w_out, l, xf, post_mix_g[l], pre_ffn_g[l])
        g_next = pre_mix_g[(l + 1) % depth]
        act, wd = _ffn_up(h, w_gate, w_up, w_down, l)
        xf, h = _ffn_down(act, wd, xf, post_ffn_g[l], g_next)
    return xf.reshape(batch, seq, d)
```

```python
import functools
import math

import numpy as np
import jax
import jax.numpy as jnp
from jax import lax
from jax.experimental import pallas as pl
from jax.experimental.pallas import tpu as pltpu

D_MODEL = 2048
HEAD_DIM = 128
RET_HEADS = 6
DIFF_HEADS = 6
SGU_GROUPS = 4
DIFF_MAP_DIM = 64
RET_W = RET_HEADS * HEAD_DIM
DIFF_W = DIFF_HEADS * HEAD_DIM
SGU_W = SGU_GROUPS * HEAD_DIM
IN_W = 4 * RET_W + 3 * DIFF_W + 2 * SGU_W
CHUNK = 128
EPS = 1e-6
LOG2E = math.log2(math.e)
NEG_BIG = -1e30
SUM_ROWS = 16
FFN_TILES = 11
QCHUNK = 256

COL_RQ, COL_RK, COL_RV, COL_RG = 0, 6, 12, 18
COL_DQ, COL_DK, COL_DV = 24, 30, 36
COL_SU, COL_SV = 42, 46

V7X_VMEM_BYTES = 64 * 1024 * 1024
VMEM_LIMIT = 56 * 1024 * 1024
VMEM_LIMIT_FFN = 60 * 1024 * 1024

BF16 = jnp.bfloat16
F32 = jnp.float32


def _dot(a, b):
    return jnp.dot(a, b, preferred_element_type=F32)


def _dot_nt(a, b):
    return lax.dot_general(a, b, (((1,), (1,)), ((), ())), preferred_element_type=F32)


def _dot_tn(a, b):
    return lax.dot_general(a, b, (((0,), (0,)), ((), ())), preferred_element_type=F32)


def _rms(x, g):
    return x * lax.rsqrt(jnp.mean(x * x, axis=-1, keepdims=True) + EPS) * g


def _params(*sem):
    return pltpu.CompilerParams(dimension_semantics=sem, vmem_limit_bytes=VMEM_LIMIT)


def _rmsnorm_kernel(x_ref, g_ref, o_ref):
    o_ref[...] = _rms(x_ref[...], g_ref[...]).astype(o_ref.dtype)


def _rmsnorm(x, g, tm=512):
    m, d = x.shape
    return pl.pallas_call(
        _rmsnorm_kernel,
        grid=(m // tm,),
        in_specs=[pl.BlockSpec((tm, d), lambda i: (i, 0)), pl.BlockSpec((1, d), lambda i: (0, 0))],
        out_specs=pl.BlockSpec((tm, d), lambda i: (i, 0)),
        out_shape=jax.ShapeDtypeStruct((m, d), BF16),
        compiler_params=_params("parallel"),
        name="rmsnorm",
    )(x, g.reshape(1, d))


def _inproj_kernel(h_ref, w_ref, o_ref, wb_ref):
    @pl.when(pl.program_id(1) == 0)
    def _():
        wb_ref[...] = w_ref[0].astype(BF16)

    o_ref[...] = _dot(h_ref[...], wb_ref[...]).astype(o_ref.dtype)


def _inproj(h, w_stack, layer, tm=1024, tn=1280):
    m, k = h.shape
    n = w_stack.shape[2]
    return pl.pallas_call(
        _inproj_kernel,
        grid=(n // tn, m // tm),
        in_specs=[pl.BlockSpec((tm, k), lambda j, i: (i, 0)),
                  pl.BlockSpec((1, k, tn), lambda j, i: (layer, 0, j))],
        out_specs=pl.BlockSpec((tm, tn), lambda j, i: (i, j)),
        out_shape=jax.ShapeDtypeStruct((m, n), BF16),
        scratch_shapes=[pltpu.VMEM((k, tn), BF16)],
        compiler_params=_params("arbitrary", "arbitrary"),
        name="inproj",
    )(h, w_stack)


def _retention_kernel(lg_ref, q_ref, k_ref, v_ref, g_ref, gn_ref, o_ref, r_ref, *, tb):
    h = pl.program_id(1)
    t = pl.program_id(2)

    @pl.when(t == 0)
    def _():
        r_ref[...] = jnp.zeros_like(r_ref)

    lg = lg_ref[h]
    scale = HEAD_DIM ** -0.5
    row = lax.broadcasted_iota(jnp.int32, (CHUNK, CHUNK), 0)
    col = lax.broadcasted_iota(jnp.int32, (CHUNK, CHUNK), 1)
    rel = (row - col).astype(F32)
    decay = jnp.where(rel >= 0, jnp.exp(lg * jnp.maximum(rel, 0.0)), 0.0) * scale
    pos = lax.broadcasted_iota(jnp.int32, (CHUNK, 1), 0).astype(F32)
    zeta = jnp.exp(lg * (CHUNK - 1.0 - pos)) * scale
    xi = jnp.exp(lg * (pos + 1.0))
    chunk_decay = jnp.exp(jnp.full((1, 1), lg * CHUNK, F32))
    gn = gn_ref[0]

    for c in range(tb // CHUNK):
        sl = slice(c * CHUNK, (c + 1) * CHUNK)
        q = q_ref[sl, :]
        k = k_ref[sl, :]
        v = v_ref[sl, :]
        scores = _dot_nt(q, k) * decay
        inner = _dot(scores.astype(BF16), v)
        r_prev = r_ref[...]
        cross = _dot(q, r_prev.astype(BF16)) * xi
        kz = (k.astype(F32) * zeta).astype(BF16)
        r_ref[...] = _dot_tn(kz, v) + chunk_decay * r_prev
        y = inner + cross
        mu = jnp.mean(y, axis=-1, keepdims=True)
        yc = y - mu
        var = jnp.mean(yc * yc, axis=-1, keepdims=True)
        y = yc * lax.rsqrt(var + EPS) * gn
        gate = g_ref[sl, :].astype(F32)
        o_ref[sl, :] = (gate * jax.nn.sigmoid(gate) * y).astype(o_ref.dtype)


def _retention(proj, gn_gain, batch, seq, tb=1024):
    m = proj.shape[0]
    nt = seq // tb
    log_gamma = np.log1p(-(2.0 ** (-5.0 - np.arange(RET_HEADS, dtype=np.float32)))).astype(np.float32)

    def blk(col):
        return pl.BlockSpec((tb, HEAD_DIM), lambda b, h, t: (b * nt + t, col + h))

    return pl.pallas_call(
        functools.partial(_retention_kernel, tb=tb),
        grid=(batch, RET_HEADS, nt),
        in_specs=[pl.BlockSpec(memory_space=pltpu.SMEM),
                  blk(COL_RQ), blk(COL_RK), blk(COL_RV), blk(COL_RG),
                  pl.BlockSpec((1, 1, HEAD_DIM), lambda b, h, t: (h, 0, 0))],
        out_specs=blk(0),
        out_shape=jax.ShapeDtypeStruct((m, RET_W), BF16),
        scratch_shapes=[pltpu.VMEM((HEAD_DIM, HEAD_DIM), F32)],
        compiler_params=_params("arbitrary", "arbitrary", "arbitrary"),
        name="retention",
    )(jnp.asarray(log_gamma), proj, proj, proj, proj, gn_gain.reshape(RET_HEADS, 1, HEAD_DIM))


def _diffattn_kernel(slope_ref, q_ref, k_ref, v_ref, lq1_ref, lk1_ref, lq2_ref, lk2_ref, sg_ref, o_ref,
                     k1_ref, k2_ref, vt_ref, *bufs, bq, bk, lambda_init):
    acc = bufs[0:2]
    s_buf = [bufs[2:4], bufs[4:6]]
    p_buf = [bufs[6:8], bufs[8:10]]
    h = pl.program_id(1)
    qi = pl.program_id(2)
    nkb = k1_ref.shape[0]
    per_q = bq // bk

    @pl.when(qi == 0)
    def _():
        slope2 = slope_ref[h] * LOG2E
        lane = lax.broadcasted_iota(jnp.int32, (bk, HEAD_DIM), 1)

        def prep(j, carry):
            start = pl.multiple_of(j * bk, bk)
            kb = k_ref[pl.ds(start, bk), :].astype(F32)
            kpos = (lax.broadcasted_iota(jnp.int32, (bk, HEAD_DIM), 0) + start).astype(F32)
            bias = kpos * slope2
            hi = bias.astype(BF16).astype(F32)
            r1 = bias - hi
            mid = r1.astype(BF16).astype(F32)
            lo = r1 - mid
            zero = jnp.zeros_like(kb)
            k1 = jnp.where(lane < 64, kb,
                           jnp.where(lane == 64, hi, jnp.where(lane == 65, mid, jnp.where(lane == 66, lo, zero))))
            k2 = jnp.where(lane >= 64, kb,
                           jnp.where(lane == 0, hi, jnp.where(lane == 1, mid, jnp.where(lane == 2, lo, zero))))
            k1_ref[j] = k1.astype(BF16)
            k2_ref[j] = k2.astype(BF16)
            vt_ref[j, 0:HEAD_DIM, :] = v_ref[pl.ds(start, bk), :].astype(F32).T.astype(BF16)
            vt_ref[j, HEAD_DIM:, :] = jnp.ones((SUM_ROWS, bk), BF16)
            return carry

        lax.fori_loop(0, nkb, prep, 0)

    lane = lax.broadcasted_iota(jnp.int32, (bq, HEAD_DIM), 1)
    qs = q_ref[...].astype(F32) * (DIFF_MAP_DIM ** -0.5 * LOG2E)
    one = jnp.ones_like(qs)
    zero = jnp.zeros_like(qs)
    qa = (jnp.where(lane < 64, qs, jnp.where(lane < 67, one, zero)).astype(BF16),
          jnp.where(lane >= 64, qs, jnp.where(lane < 3, one, zero)).astype(BF16))
    ka = (k1_ref, k2_ref)
    chunks = [slice(c * QCHUNK, (c + 1) * QCHUNK) for c in range(bq // QCHUNK)]
    krow = lax.broadcasted_iota(jnp.int32, (bk, QCHUNK), 0)
    qcol = lax.broadcasted_iota(jnp.int32, (bk, QCHUNK), 1)

    def issue_scores(mp, j, s_out, diag=None):
        mbs = []
        for cs in chunks:
            first_key = 0 if diag is None else diag * bk
            if first_key >= cs.stop:
                mbs.append(jnp.full((1, QCHUNK), 2 * NEG_BIG, F32))
                continue
            s = _dot_nt(ka[mp][j], qa[mp][cs, :])
            if diag is not None and first_key + bk - 1 > cs.start:
                s = jnp.where(krow + first_key <= qcol + cs.start, s, 2 * NEG_BIG)
            s_out[mp][:, cs] = s
            mbs.append(jnp.max(s, axis=0, keepdims=True))
        return jnp.concatenate(mbs, axis=1)

    def live(cs, diag):
        return diag is None or diag * bk < cs.stop

    def stage(slot, carry, nxt, pending, cur_diag=None, nxt_diag=None, pend_diag=None, first=False):
        s_in, p_in = s_buf[slot], p_buf[slot]
        s_out, p_out = s_buf[1 - slot], p_buf[1 - slot]
        ms, alphas, mbs, dep = carry[0:6:3], carry[1:6:3], carry[2:6:3], carry[6]
        mns = [jnp.maximum(ms[mp], mbs[mp]) for mp in range(2)]
        new_mb = []
        for mp in range(2):
            new_mb.append(issue_scores(mp, nxt, s_out, nxt_diag))
            for cs in chunks:
                if live(cs, cur_diag):
                    p = jnp.exp2(s_in[mp][:, cs] - (mns[mp][:, cs] + dep))
                    p_out[mp][:, cs] = p.astype(BF16)
                else:
                    p_out[mp][:, cs] = jnp.zeros((bk, QCHUNK), BF16)
        if pending is not None:
            for mp in range(2):
                for cs in chunks:
                    if live(cs, pend_diag):
                        pv = _dot(vt_ref[pending], p_in[mp][:, cs])
                        acc[mp][:, cs] = pv if first else alphas[mp][:, cs] * acc[mp][:, cs] + pv
        out = []
        for mp in range(2):
            out += [mns[mp], jnp.exp2(ms[mp] - mns[mp]), new_mb[mp]]
        return tuple(out) + (p[0:1, :] * 0.0,)

    d0 = per_q * qi
    nfull = per_q * qi
    carry = []
    for mp in range(2):
        carry += [jnp.full((1, bq), NEG_BIG, F32), jnp.ones((1, bq), F32), issue_scores(mp, d0, s_buf[0], diag=0)]
    carry = tuple(carry) + (jnp.zeros((1, QCHUNK), F32),)
    for d in range(per_q):
        more = d + 1 < per_q
        carry = stage(d % 2, carry, d0 + d + 1 if more else 0, d0 + d - 1 if d else None,
                      cur_diag=d, nxt_diag=d + 1 if more else None, pend_diag=d - 1 if d else None, first=d == 1)

    def unrolled(u, carry):
        t = 2 * u
        carry = stage(0, carry, t + 1, jnp.where(t == 0, d0 + per_q - 1, t - 1))
        return stage(1, carry, jnp.minimum(t + 2, nfull - 1), t)

    carry = lax.fori_loop(0, nfull // 2, unrolled, carry)
    last = jnp.where(qi == 0, per_q - 1, nfull - 1)
    for mp in range(2):
        for cs in chunks:
            acc[mp][:, cs] = carry[3 * mp + 1][:, cs] * acc[mp][:, cs] + _dot(vt_ref[last], p_buf[0][mp][:, cs])

    lam = (jnp.exp(jnp.sum(lq1_ref[...] * lk1_ref[...], axis=-1, keepdims=True))
           - jnp.exp(jnp.sum(lq2_ref[...] * lk2_ref[...], axis=-1, keepdims=True)) + lambda_init)
    l1 = acc[0][HEAD_DIM:HEAD_DIM + 1, :]
    l2 = acc[1][HEAD_DIM:HEAD_DIM + 1, :]
    o_t = acc[0][0:HEAD_DIM, :] / l1 - lam * (acc[1][0:HEAD_DIM, :] / l2)
    o = o_t.T
    o = _rms(o, sg_ref[0]) * (1.0 - lambda_init)
    o_ref[...] = o.astype(o_ref.dtype)


def _diffattn(proj, lq1, lk1, lq2, lk2, subln_g, lambda_init, batch, seq, bq=2048, bk=512):
    m = proj.shape[0]
    nq, nk = seq // bq, seq // bk
    slopes = (2.0 ** (-8.0 * np.arange(1, DIFF_HEADS + 1, dtype=np.float32) / DIFF_HEADS)).astype(np.float32)
    vec = pl.BlockSpec((1, DIFF_MAP_DIM), lambda b, h, i: (0, 0))
    return pl.pallas_call(
        functools.partial(_diffattn_kernel, bq=bq, bk=bk, lambda_init=lambda_init),
        grid=(batch, DIFF_HEADS, nq),
        in_specs=[pl.BlockSpec(memory_space=pltpu.SMEM),
                  pl.BlockSpec((bq, HEAD_DIM), lambda b, h, i: (b * nq + i, COL_DQ + h)),
                  pl.BlockSpec((seq, HEAD_DIM), lambda b, h, i: (b, COL_DK + h)),
                  pl.BlockSpec((seq, HEAD_DIM), lambda b, h, i: (b, COL_DV + h)),
                  vec, vec, vec, vec,
                  pl.BlockSpec((1, 1, HEAD_DIM), lambda b, h, i: (h, 0, 0))],
        out_specs=pl.BlockSpec((bq, HEAD_DIM), lambda b, h, i: (b * nq + i, h)),
        out_shape=jax.ShapeDtypeStruct((m, DIFF_W), BF16),
        scratch_shapes=[pltpu.VMEM((nk, bk, HEAD_DIM), BF16),
                        pltpu.VMEM((nk, bk, HEAD_DIM), BF16),
                        pltpu.VMEM((nk, HEAD_DIM + SUM_ROWS, bk), BF16),
                        *[pltpu.VMEM((HEAD_DIM + SUM_ROWS, bq), F32)] * 2,
                        *[pltpu.VMEM((bk, bq), F32)] * 4,
                        *[pltpu.VMEM((bk, bq), BF16)] * 4],
        compiler_params=_params("arbitrary", "arbitrary", "arbitrary"),
        name="diffattn",
    )(jnp.asarray(slopes), proj, proj, proj,
      lq1.reshape(1, -1), lk1.reshape(1, -1), lq2.reshape(1, -1), lk2.reshape(1, -1),
      subln_g.reshape(DIFF_HEADS, 1, HEAD_DIM))


def _sgu_kernel(u_ref, v_ref, lng_ref, lnb_ref, w_ref, b_ref, o_ref, *, tb):
    row = lax.broadcasted_iota(jnp.int32, (CHUNK, CHUNK), 0)
    col = lax.broadcasted_iota(jnp.int32, (CHUNK, CHUNK), 1)
    w = jnp.where(row >= col, w_ref[0], 0.0).astype(BF16)
    bias = b_ref[0]
    lng = lng_ref[0]
    lnb = lnb_ref[0]
    for c in range(tb // CHUNK):
        sl = slice(c * CHUNK, (c + 1) * CHUNK)
        u = jax.nn.gelu(u_ref[sl, :].astype(F32))
        v = jax.nn.gelu(v_ref[sl, :].astype(F32))
        mu = jnp.mean(v, axis=-1, keepdims=True)
        vc = v - mu
        var = jnp.mean(vc * vc, axis=-1, keepdims=True)
        vn = vc * lax.rsqrt(var + EPS) * lng + lnb
        mixed = _dot(w, vn.astype(BF16)) + bias
        o_ref[sl, :] = (u * mixed).astype(o_ref.dtype)


def _sgu(proj, ln_g, ln_b, w_s, b_s, tb=1024):
    m = proj.shape[0]
    g = SGU_GROUPS
    per_group = lambda i, j: (j, 0, 0)
    return pl.pallas_call(
        functools.partial(_sgu_kernel, tb=tb),
        grid=(m // tb, g),
        in_specs=[pl.BlockSpec((tb, HEAD_DIM), lambda i, j: (i, COL_SU + j)),
                  pl.BlockSpec((tb, HEAD_DIM), lambda i, j: (i, COL_SV + j)),
                  pl.BlockSpec((1, 1, HEAD_DIM), per_group),
                  pl.BlockSpec((1, 1, HEAD_DIM), per_group),
                  pl.BlockSpec((1, CHUNK, CHUNK), per_group),
                  pl.BlockSpec((1, CHUNK, 1), per_group)],
        out_specs=pl.BlockSpec((tb, HEAD_DIM), lambda i, j: (i, j)),
        out_shape=jax.ShapeDtypeStruct((m, SGU_W), BF16),
        compiler_params=_params("parallel", "arbitrary"),
        name="sgu",
    )(proj, proj, ln_g.reshape(g, 1, HEAD_DIM), ln_b.reshape(g, 1, HEAD_DIM), w_s, b_s.reshape(g, CHUNK, 1))


def _outproj_kernel(yr_ref, yd_ref, ys_ref, wf_ref, x_ref, gpost_ref, gnext_ref, xo_ref, ho_ref, w_ref):
    @pl.when(pl.program_id(0) == 0)
    def _():
        w_ref[...] = wf_ref[0].astype(BF16)

    mix = (_dot(yr_ref[...], w_ref[0:RET_W, :])
           + _dot(yd_ref[...], w_ref[RET_W:RET_W + DIFF_W, :])
           + _dot(ys_ref[...], w_ref[RET_W + DIFF_W:D_MODEL, :]))
    xn = x_ref[...] + _rms(mix, gpost_ref[...])
    xo_ref[...] = xn
    ho_ref[...] = _rms(xn, gnext_ref[...]).astype(ho_ref.dtype)


def _outproj(y_ret, y_diff, y_sgu, w_stack, layer, x, g_post, g_next, tm=512):
    m, d = x.shape
    row = lambda i: (i, 0)
    const = lambda i: (0, 0)
    return pl.pallas_call(
        _outproj_kernel,
        grid=(m // tm,),
        in_specs=[pl.BlockSpec((tm, RET_W), row), pl.BlockSpec((tm, DIFF_W), row), pl.BlockSpec((tm, SGU_W), row),
                  pl.BlockSpec((1, d, d), lambda i: (layer, 0, 0), pipeline_mode=pl.Buffered(1)),
                  pl.BlockSpec((tm, d), row),
                  pl.BlockSpec((1, d), const), pl.BlockSpec((1, d), const)],
        out_specs=[pl.BlockSpec((tm, d), row), pl.BlockSpec((tm, d), row)],
        out_shape=[jax.ShapeDtypeStruct((m, d), F32), jax.ShapeDtypeStruct((m, d), BF16)],
        scratch_shapes=[pltpu.VMEM((d, d), BF16)],
        compiler_params=_params("arbitrary"),
        name="outproj",
    )(y_ret, y_diff, y_sgu, w_stack, x, g_post.reshape(1, d), g_next.reshape(1, d))


def _ffn_up_kernel(h_ref, wg_ref, wu_ref, wd_ref, a_ref, wdb_ref, w_ref):
    th = wg_ref.shape[2]

    @pl.when(pl.program_id(1) == 0)
    def _():
        w_ref[:, 0:th] = wg_ref[0].astype(BF16)
        w_ref[:, th:] = wu_ref[0].astype(BF16)
        wdb_ref[...] = wd_ref[0].astype(BF16)

    gu = _dot(h_ref[...], w_ref[...])
    gate = gu[:, 0:th]
    a_ref[...] = (gate * jax.nn.sigmoid(gate) * gu[:, th:]).astype(a_ref.dtype)


def _ffn_up(h, wg_stack, wu_stack, wd_stack, layer, tm=1024):
    m, d = h.shape
    hidden = wg_stack.shape[2]
    th = hidden // FFN_TILES
    w_tile = pl.BlockSpec((1, d, th), lambda j, i: (layer, 0, j))
    return pl.pallas_call(
        _ffn_up_kernel,
        grid=(FFN_TILES, m // tm),
        in_specs=[pl.BlockSpec((tm, d), lambda j, i: (i, 0)), w_tile, w_tile,
                  pl.BlockSpec((1, th, d), lambda j, i: (layer, j, 0))],
        out_specs=[pl.BlockSpec((tm, th), lambda j, i: (i, j)), pl.BlockSpec((th, d), lambda j, i: (j, 0))],
        out_shape=[jax.ShapeDtypeStruct((m, hidden), BF16), jax.ShapeDtypeStruct((hidden, d), BF16)],
        scratch_shapes=[pltpu.VMEM((d, 2 * th), BF16)],
        compiler_params=_params("arbitrary", "arbitrary"),
        name="ffn_up",
    )(h, wg_stack, wu_stack, wd_stack)


def _ffn_down_kernel(a_ref, w_ref, x_ref, gpost_ref, gnext_ref, xo_ref, ho_ref):
    f = _dot(a_ref[...], w_ref[...])
    xn = x_ref[...] + _rms(f, gpost_ref[...])
    xo_ref[...] = xn
    ho_ref[...] = _rms(xn, gnext_ref[...]).astype(ho_ref.dtype)


def _ffn_down(act, wd, x, g_post, g_next, tm=512):
    m, d = x.shape
    hidden = act.shape[1]
    row = lambda i: (i, 0)
    const = lambda i: (0, 0)
    return pl.pallas_call(
        _ffn_down_kernel,
        grid=(m // tm,),
        in_specs=[pl.BlockSpec((tm, hidden), row),
                  pl.BlockSpec((hidden, d), const, pipeline_mode=pl.Buffered(1)),
                  pl.BlockSpec((tm, d), row),
                  pl.BlockSpec((1, d), const), pl.BlockSpec((1, d), const)],
        out_specs=[pl.BlockSpec((tm, d), row), pl.BlockSpec((tm, d), row)],
        out_shape=[jax.ShapeDtypeStruct((m, d), F32), jax.ShapeDtypeStruct((m, d), BF16)],
        compiler_params=pltpu.CompilerParams(dimension_semantics=("parallel",), vmem_limit_bytes=VMEM_LIMIT_FFN),
        name="ffn_down",
    )(act, wd, x, g_post.reshape(1, d), g_next.reshape(1, d))


def kernel(x, pre_mix_g, w_in, ret_gn_g, diff_lam_q1, diff_lam_k1, diff_lam_q2, diff_lam_k2, diff_subln_g,
           sgu_ln_g, sgu_ln_b, sgu_w, sgu_b, w_out, post_mix_g, pre_ffn_g, w_gate, w_up, w_down, post_ffn_g):
    batch, seq, d = x.shape
    depth = w_in.shape[0]
    xf = x.reshape(batch * seq, d)
    h = _rmsnorm(xf, pre_mix_g[0])
    for l in range(depth):
        lambda_init = 0.8 - 0.6 * math.exp(-0.3 * l)
        proj = _inproj(h, w_in, l)
        y_ret = _retention(proj, ret_gn_g[l], batch, seq)
        y_diff = _diffattn(proj, diff_lam_q1[l], diff_lam_k1[l], diff_lam_q2[l], diff_lam_k2[l],
                           diff_subln_g[l], lambda_init, batch, seq)
        y_sgu = _sgu(proj, sgu_ln_g[l], sgu_ln_b[l], sgu_w[l], sgu_b[l])
        xf, h = _outproj(y_ret, y_diff, y_sgu, w_out, l, xf, post_mix_g[l], pre_ffn_g[l])
        g_next = pre_mix_g[(l + 1) % depth]
        act, wd = _ffn_up(h, w_gate, w_up, w_down, l)
        xf, h = _ffn_down(act, wd, xf, post_ffn_g[l], g_next)
    return xf.reshape(batch, seq, d)
```

```python
import functools
import math

import numpy as np
import jax
import jax.numpy as jnp
from jax import lax
from jax.experimental import pallas as pl
from jax.experimental.pallas import tpu as pltpu

D_MODEL = 2048
HEAD_DIM = 128
RET_HEADS = 6
DIFF_HEADS = 6
SGU_GROUPS = 4
DIFF_MAP_DIM = 64
RET_W = RET_HEADS * HEAD_DIM
DIFF_W = DIFF_HEADS * HEAD_DIM
SGU_W = SGU_GROUPS * HEAD_DIM
IN_W = 4 * RET_W + 3 * DIFF_W + 2 * SGU_W
CHUNK = 128
RET_CHUNK = 256
EPS = 1e-6
LOG2E = math.log2(math.e)
NEG_BIG = -1e30
SUM_ROWS = 16
FFN_TILES = 11
OUTPROJ_TILES = 32
QCHUNK = 256

COL_RQ, COL_RK, COL_RV, COL_RG = 0, 6, 12, 18
COL_DQ, COL_DK, COL_DV = 24, 30, 36
COL_SU, COL_SV = 42, 46

V7X_VMEM_BYTES = 64 * 1024 * 1024
VMEM_LIMIT = 56 * 1024 * 1024
VMEM_LIMIT_FFN = 60 * 1024 * 1024

BF16 = jnp.bfloat16
F32 = jnp.float32


def _dot(a, b):
    return jnp.dot(a, b, preferred_element_type=F32)


def _dot_nt(a, b):
    return lax.dot_general(a, b, (((1,), (1,)), ((), ())), preferred_element_type=F32)


def _dot_tn(a, b):
    return lax.dot_general(a, b, (((0,), (0,)), ((), ())), preferred_element_type=F32)


def _rms(x, g):
    return x * lax.rsqrt(jnp.mean(x * x, axis=-1, keepdims=True) + EPS) * g


def _params(*sem):
    return pltpu.CompilerParams(dimension_semantics=sem, vmem_limit_bytes=VMEM_LIMIT)


def _rmsnorm_kernel(x_ref, g_ref, o_ref):
    o_ref[...] = _rms(x_ref[...], g_ref[...]).astype(o_ref.dtype)


def _rmsnorm(x, g, tm=512):
    m, d = x.shape
    return pl.pallas_call(
        _rmsnorm_kernel,
        grid=(m // tm,),
        in_specs=[pl.BlockSpec((tm, d), lambda i: (i, 0)), pl.BlockSpec((1, d), lambda i: (0, 0))],
        out_specs=pl.BlockSpec((tm, d), lambda i: (i, 0)),
        out_shape=jax.ShapeDtypeStruct((m, d), BF16),
        compiler_params=_params("parallel"),
        name="rmsnorm",
    )(x, g.reshape(1, d))


def _inproj_kernel(h_ref, w_ref, o_ref, wb_ref):
    @pl.when(pl.program_id(1) == 0)
    def _():
        wb_ref[...] = w_ref[0].astype(BF16)

    o_ref[...] = _dot(h_ref[...], wb_ref[...]).astype(o_ref.dtype)


def _inproj(h, w_stack, layer, tm=1024, tn=1280):
    m, k = h.shape
    n = w_stack.shape[2]
    return pl.pallas_call(
        _inproj_kernel,
        grid=(n // tn, m // tm),
        in_specs=[pl.BlockSpec((tm, k), lambda j, i: (i, 0)),
                  pl.BlockSpec((1, k, tn), lambda j, i: (layer, 0, j))],
        out_specs=pl.BlockSpec((tm, tn), lambda j, i: (i, j)),
        out_shape=jax.ShapeDtypeStruct((m, n), BF16),
        scratch_shapes=[pltpu.VMEM((k, tn), BF16)],
        compiler_params=_params("arbitrary", "arbitrary"),
        name="inproj",
    )(h, w_stack)


def _retention_kernel(lg_ref, q_ref, k_ref, v_ref, g_ref, gn_ref, o_ref, r_ref, *, tb):
    pair = pl.program_id(1)
    t = pl.program_id(2)

    @pl.when(t == 0)
    def _():
        r_ref[...] = jnp.zeros_like(r_ref)

    scale = HEAD_DIM ** -0.5
    row = lax.broadcasted_iota(jnp.int32, (RET_CHUNK, RET_CHUNK), 0)
    col = lax.broadcasted_iota(jnp.int32, (RET_CHUNK, RET_CHUNK), 1)
    rel = (row - col).astype(F32)
    pos = lax.broadcasted_iota(jnp.int32, (RET_CHUNK, 1), 0).astype(F32)
    consts = []
    for i in range(2):
        lg = lg_ref[2 * pair + i]
        decay = jnp.where(rel >= 0, jnp.exp(lg * jnp.maximum(rel, 0.0)), 0.0) * scale
        zeta = jnp.exp(lg * (RET_CHUNK - 1.0 - pos)) * scale
        xi = jnp.exp(lg * (pos + 1.0))
        chunk_decay = jnp.exp(jnp.full((1, 1), lg * RET_CHUNK, F32))
        consts.append((decay, zeta, xi, chunk_decay))

    for c in range(tb // RET_CHUNK):
        sl = slice(c * RET_CHUNK, (c + 1) * RET_CHUNK)
        for i in range(2):
            decay, zeta, xi, chunk_decay = consts[i]
            hs = slice(i * HEAD_DIM, (i + 1) * HEAD_DIM)
            q = q_ref[sl, hs]
            k = k_ref[sl, hs]
            v = v_ref[sl, hs]
            scores = _dot_nt(q, k) * decay
            inner = _dot(scores.astype(BF16), v)
            r_prev = r_ref[i]
            cross = _dot(q, r_prev.astype(BF16)) * xi
            kz = (k.astype(F32) * zeta).astype(BF16)
            r_ref[i] = _dot_tn(kz, v) + chunk_decay * r_prev
            y = inner + cross
            mu = jnp.mean(y, axis=-1, keepdims=True)
            yc = y - mu
            var = jnp.mean(yc * yc, axis=-1, keepdims=True)
            y = yc * lax.rsqrt(var + EPS) * gn_ref[0, :, hs]
            gate = g_ref[sl, hs].astype(F32)
            o_ref[sl, hs] = (gate * jax.nn.sigmoid(gate) * y).astype(o_ref.dtype)


def _retention(proj, gn_gain, batch, seq, tb=1024):
    m = proj.shape[0]
    nt = seq // tb
    log_gamma = np.log1p(-(2.0 ** (-5.0 - np.arange(RET_HEADS, dtype=np.float32)))).astype(np.float32)
    wide = 2 * HEAD_DIM

    def blk(col):
        return pl.BlockSpec((tb, wide), lambda b, h, t: (b * nt + t, col // 2 + h))

    return pl.pallas_call(
        functools.partial(_retention_kernel, tb=tb),
        grid=(batch, RET_HEADS // 2, nt),
        in_specs=[pl.BlockSpec(memory_space=pltpu.SMEM),
                  blk(COL_RQ), blk(COL_RK), blk(COL_RV), blk(COL_RG),
                  pl.BlockSpec((1, 1, wide), lambda b, h, t: (h, 0, 0))],
        out_specs=blk(0),
        out_shape=jax.ShapeDtypeStruct((m, RET_W), BF16),
        scratch_shapes=[pltpu.VMEM((2, HEAD_DIM, HEAD_DIM), F32)],
        compiler_params=_params("arbitrary", "arbitrary", "arbitrary"),
        name="retention",
    )(jnp.asarray(log_gamma), proj, proj, proj, proj, gn_gain.reshape(RET_HEADS // 2, 1, wide))


def _diffattn_kernel(slope_ref, q_ref, k_ref, v_ref, lq1_ref, lk1_ref, lq2_ref, lk2_ref, sg_ref, o_ref,
                     k1_ref, k2_ref, vt_ref, *bufs, bq, bk, lambda_init):
    acc = bufs[0:2]
    s_buf = [bufs[2:4], bufs[4:6]]
    p_buf = [bufs[6:8], bufs[8:10]]
    h = pl.program_id(1)
    qi = pl.program_id(2)
    nkb = k1_ref.shape[0]
    per_q = bq // bk

    @pl.when(qi == 0)
    def _():
        slope2 = slope_ref[h] * LOG2E
        lane = lax.broadcasted_iota(jnp.int32, (bk, HEAD_DIM), 1)

        def prep(j, carry):
            start = pl.multiple_of(j * bk, bk)
            kb = k_ref[pl.ds(start, bk), :].astype(F32)
            kpos = (lax.broadcasted_iota(jnp.int32, (bk, HEAD_DIM), 0) + start).astype(F32)
            bias = kpos * slope2
            hi = bias.astype(BF16).astype(F32)
            r1 = bias - hi
            mid = r1.astype(BF16).astype(F32)
            lo = r1 - mid
            zero = jnp.zeros_like(kb)
            k1 = jnp.where(lane < 64, kb,
                           jnp.where(lane == 64, hi, jnp.where(lane == 65, mid, jnp.where(lane == 66, lo, zero))))
            k2 = jnp.where(lane >= 64, kb,
                           jnp.where(lane == 0, hi, jnp.where(lane == 1, mid, jnp.where(lane == 2, lo, zero))))
            k1_ref[j] = k1.astype(BF16)
            k2_ref[j] = k2.astype(BF16)
            vt_ref[j, 0:HEAD_DIM, :] = v_ref[pl.ds(start, bk), :].astype(F32).T.astype(BF16)
            vt_ref[j, HEAD_DIM:, :] = jnp.ones((SUM_ROWS, bk), BF16)
            return carry

        lax.fori_loop(0, nkb, prep, 0)

    lane = lax.broadcasted_iota(jnp.int32, (bq, HEAD_DIM), 1)
    qs = q_ref[...].astype(F32) * (DIFF_MAP_DIM ** -0.5 * LOG2E)
    one = jnp.ones_like(qs)
    zero = jnp.zeros_like(qs)
    qa = (jnp.where(lane < 64, qs, jnp.where(lane < 67, one, zero)).astype(BF16),
          jnp.where(lane >= 64, qs, jnp.where(lane < 3, one, zero)).astype(BF16))
    ka = (k1_ref, k2_ref)
    chunks = [slice(c * QCHUNK, (c + 1) * QCHUNK) for c in range(bq // QCHUNK)]
    krow = lax.broadcasted_iota(jnp.int32, (bk, QCHUNK), 0)
    qcol = lax.broadcasted_iota(jnp.int32, (bk, QCHUNK), 1)

    def issue_scores(mp, j, s_out, diag=None):
        mbs = []
        for cs in chunks:
            first_key = 0 if diag is None else diag * bk
            if first_key >= cs.stop:
                mbs.append(jnp.full((1, QCHUNK), 2 * NEG_BIG, F32))
                continue
            s = _dot_nt(ka[mp][j], qa[mp][cs, :])
            if diag is not None and first_key + bk - 1 > cs.start:
                s = jnp.where(krow + first_key <= qcol + cs.start, s, 2 * NEG_BIG)
            s_out[mp][:, cs] = s
            mbs.append(jnp.max(s, axis=0, keepdims=True))
        return jnp.concatenate(mbs, axis=1)

    def live(cs, diag):
        return diag is None or diag * bk < cs.stop

    def stage(slot, carry, nxt, pending, cur_diag=None, nxt_diag=None, pend_diag=None, first=False):
        s_in, p_in = s_buf[slot], p_buf[slot]
        s_out, p_out = s_buf[1 - slot], p_buf[1 - slot]
        ms, alphas, mbs, dep = carry[0:6:3], carry[1:6:3], carry[2:6:3], carry[6]
        mns = [jnp.maximum(ms[mp], mbs[mp]) for mp in range(2)]
        new_mb = []
        for mp in range(2):
            new_mb.append(issue_scores(mp, nxt, s_out, nxt_diag))
            for cs in chunks:
                if live(cs, cur_diag):
                    p = jnp.exp2(s_in[mp][:, cs] - (mns[mp][:, cs] + dep))
                    p_out[mp][:, cs] = p.astype(BF16)
                else:
                    p_out[mp][:, cs] = jnp.zeros((bk, QCHUNK), BF16)
        if pending is not None:
            for mp in range(2):
                for cs in chunks:
                    if live(cs, pend_diag):
                        pv = _dot(vt_ref[pending], p_in[mp][:, cs])
                        acc[mp][:, cs] = pv if first else alphas[mp][:, cs] * acc[mp][:, cs] + pv
        out = []
        for mp in range(2):
            out += [mns[mp], jnp.exp2(ms[mp] - mns[mp]), new_mb[mp]]
        return tuple(out) + (p[0:1, :] * 0.0,)

    d0 = per_q * qi
    nfull = per_q * qi
    carry = []
    for mp in range(2):
        carry += [jnp.full((1, bq), NEG_BIG, F32), jnp.ones((1, bq), F32), issue_scores(mp, d0, s_buf[0], diag=0)]
    carry = tuple(carry) + (jnp.zeros((1, QCHUNK), F32),)
    for d in range(per_q):
        more = d + 1 < per_q
        carry = stage(d % 2, carry, d0 + d + 1 if more else 0, d0 + d - 1 if d else None,
                      cur_diag=d, nxt_diag=d + 1 if more else None, pend_diag=d - 1 if d else None, first=d == 1)

    def unrolled(u, carry):
        t = 2 * u
        carry = stage(0, carry, t + 1, jnp.where(t == 0, d0 + per_q - 1, t - 1))
        return stage(1, carry, jnp.minimum(t + 2, nfull - 1), t)

    carry = lax.fori_loop(0, nfull // 2, unrolled, carry)
    last = jnp.where(qi == 0, per_q - 1, nfull - 1)
    for mp in range(2):
        for cs in chunks:
            acc[mp][:, cs] = carry[3 * mp + 1][:, cs] * acc[mp][:, cs] + _dot(vt_ref[last], p_buf[0][mp][:, cs])

    lam = (jnp.exp(jnp.sum(lq1_ref[...] * lk1_ref[...], axis=-1, keepdims=True))
           - jnp.exp(jnp.sum(lq2_ref[...] * lk2_ref[...], axis=-1, keepdims=True)) + lambda_init)
    l1 = acc[0][HEAD_DIM:HEAD_DIM + 1, :]
    l2 = acc[1][HEAD_DIM:HEAD_DIM + 1, :]
    o_t = acc[0][0:HEAD_DIM, :] / l1 - lam * (acc[1][0:HEAD_DIM, :] / l2)
    o = o_t.T
    o = _rms(o, sg_ref[0]) * (1.0 - lambda_init)
    o_ref[...] = o.astype(o_ref.dtype)


def _diffattn(proj, lq1, lk1, lq2, lk2, subln_g, lambda_init, batch, seq, bq=2048, bk=512):
    m = proj.shape[0]
    nq, nk = seq // bq, seq // bk
    slopes = (2.0 ** (-8.0 * np.arange(1, DIFF_HEADS + 1, dtype=np.float32) / DIFF_HEADS)).astype(np.float32)
    vec = pl.BlockSpec((1, DIFF_MAP_DIM), lambda b, h, i: (0, 0))
    return pl.pallas_call(
        functools.partial(_diffattn_kernel, bq=bq, bk=bk, lambda_init=lambda_init),
        grid=(batch, DIFF_HEADS, nq),
        in_specs=[pl.BlockSpec(memory_space=pltpu.SMEM),
                  pl.BlockSpec((bq, HEAD_DIM), lambda b, h, i: (b * nq + i, COL_DQ + h)),
                  pl.BlockSpec((seq, HEAD_DIM), lambda b, h, i: (b, COL_DK + h)),
                  pl.BlockSpec((seq, HEAD_DIM), lambda b, h, i: (b, COL_DV + h)),
                  vec, vec, vec, vec,
                  pl.BlockSpec((1, 1, HEAD_DIM), lambda b, h, i: (h, 0, 0))],
        out_specs=pl.BlockSpec((bq, HEAD_DIM), lambda b, h, i: (b * nq + i, h)),
        out_shape=jax.ShapeDtypeStruct((m, DIFF_W), BF16),
        scratch_shapes=[pltpu.VMEM((nk, bk, HEAD_DIM), BF16),
                        pltpu.VMEM((nk, bk, HEAD_DIM), BF16),
                        pltpu.VMEM((nk, HEAD_DIM + SUM_ROWS, bk), BF16),
                        *[pltpu.VMEM((HEAD_DIM + SUM_ROWS, bq), F32)] * 2,
                        *[pltpu.VMEM((bk, bq), F32)] * 4,
                        *[pltpu.VMEM((bk, bq), BF16)] * 4],
        compiler_params=_params("arbitrary", "arbitrary", "arbitrary"),
        name="diffattn",
    )(jnp.asarray(slopes), proj, proj, proj,
      lq1.reshape(1, -1), lk1.reshape(1, -1), lq2.reshape(1, -1), lk2.reshape(1, -1),
      subln_g.reshape(DIFF_HEADS, 1, HEAD_DIM))


def _sgu_kernel(u_ref, v_ref, lng_ref, lnb_ref, w_ref, b_ref, o_ref, *, tb):
    row = lax.broadcasted_iota(jnp.int32, (CHUNK, CHUNK), 0)
    col = lax.broadcasted_iota(jnp.int32, (CHUNK, CHUNK), 1)
    w = jnp.where(row >= col, w_ref[0], 0.0).astype(BF16)
    bias = b_ref[0]
    lng = lng_ref[0]
    lnb = lnb_ref[0]
    for c in range(tb // CHUNK):
        sl = slice(c * CHUNK, (c + 1) * CHUNK)
        u = jax.nn.gelu(u_ref[sl, :].astype(F32))
        v = jax.nn.gelu(v_ref[sl, :].astype(F32))
        mu = jnp.mean(v, axis=-1, keepdims=True)
        vc = v - mu
        var = jnp.mean(vc * vc, axis=-1, keepdims=True)
        vn = vc * lax.rsqrt(var + EPS) * lng + lnb
        mixed = _dot(w, vn.astype(BF16)) + bias
        o_ref[sl, :] = (u * mixed).astype(o_ref.dtype)


def _sgu(proj, ln_g, ln_b, w_s, b_s, tb=1024):
    m = proj.shape[0]
    g = SGU_GROUPS
    per_group = lambda i, j: (j, 0, 0)
    return pl.pallas_call(
        functools.partial(_sgu_kernel, tb=tb),
        grid=(m // tb, g),
        in_specs=[pl.BlockSpec((tb, HEAD_DIM), lambda i, j: (i, COL_SU + j)),
                  pl.BlockSpec((tb, HEAD_DIM), lambda i, j: (i, COL_SV + j)),
                  pl.BlockSpec((1, 1, HEAD_DIM), per_group),
                  pl.BlockSpec((1, 1, HEAD_DIM), per_group),
                  pl.BlockSpec((1, CHUNK, CHUNK), per_group),
                  pl.BlockSpec((1, CHUNK, 1), per_group)],
        out_specs=pl.BlockSpec((tb, HEAD_DIM), lambda i, j: (i, j)),
        out_shape=jax.ShapeDtypeStruct((m, SGU_W), BF16),
        compiler_params=_params("parallel", "arbitrary"),
        name="sgu",
    )(proj, proj, ln_g.reshape(g, 1, HEAD_DIM), ln_b.reshape(g, 1, HEAD_DIM), w_s, b_s.reshape(g, CHUNK, 1))


def _outproj_kernel(yr_ref, yd_ref, ys_ref, wf_ref, x_ref, gpost_ref, gnext_ref, xo_ref, ho_ref, w_ref, *mix_ref):
    i = pl.program_id(0)
    n = pl.num_programs(0) - 1

    @pl.when(i == 0)
    def _():
        w_ref[...] = wf_ref[0].astype(BF16)
        mix_ref[1][...] = jnp.zeros_like(mix_ref[1])

    tm = x_ref.shape[0]
    pieces = [slice(r * (tm // 4), (r + 1) * (tm // 4)) for r in range(4)]

    def finish(mix, rows):
        xn = x_ref[rows, :] + _rms(mix[rows, :], gpost_ref[...])
        xo_ref[rows, :] = xn
        ho_ref[rows, :] = _rms(xn, gnext_ref[...]).astype(ho_ref.dtype)
        return xn[0:1, 0:RET_W] * 0.0

    for parity in range(2):
        @pl.when((i < n) & (i % 2 == parity))
        def _():
            link = jnp.zeros((1, RET_W), F32)
            for rows in pieces:
                yr = (yr_ref[rows, :] + link.astype(BF16))
                mix_ref[parity][rows, :] = (_dot(yr, w_ref[0:RET_W, :])
                                            + _dot(yd_ref[rows, :], w_ref[RET_W:RET_W + DIFF_W, :])
                                            + _dot(ys_ref[rows, :], w_ref[RET_W + DIFF_W:D_MODEL, :]))
                link = finish(mix_ref[1 - parity], rows)

    @pl.when(i == n)
    def _():
        for rows in pieces:
            finish(mix_ref[(OUTPROJ_TILES - 1) % 2], rows)


def _outproj(y_ret, y_diff, y_sgu, w_stack, layer, x, g_post, g_next):
    m, d = x.shape
    tm = m // OUTPROJ_TILES
    cur = lambda i: (jnp.minimum(i, OUTPROJ_TILES - 1), 0)
    prev = lambda i: (jnp.maximum(i - 1, 0), 0)
    const = lambda i: (0, 0)
    return pl.pallas_call(
        _outproj_kernel,
        grid=(OUTPROJ_TILES + 1,),
        in_specs=[pl.BlockSpec((tm, RET_W), cur), pl.BlockSpec((tm, DIFF_W), cur), pl.BlockSpec((tm, SGU_W), cur),
                  pl.BlockSpec((1, d, d), lambda i: (layer, 0, 0), pipeline_mode=pl.Buffered(1)),
                  pl.BlockSpec((tm, d), prev),
                  pl.BlockSpec((1, d), const), pl.BlockSpec((1, d), const)],
        out_specs=[pl.BlockSpec((tm, d), prev), pl.BlockSpec((tm, d), prev)],
        out_shape=[jax.ShapeDtypeStruct((m, d), F32), jax.ShapeDtypeStruct((m, d), BF16)],
        scratch_shapes=[pltpu.VMEM((d, d), BF16), pltpu.VMEM((tm, d), F32), pltpu.VMEM((tm, d), F32)],
        compiler_params=pltpu.CompilerParams(dimension_semantics=("arbitrary",), vmem_limit_bytes=VMEM_LIMIT_FFN),
        name="outproj",
    )(y_ret, y_diff, y_sgu, w_stack, x, g_post.reshape(1, d), g_next.reshape(1, d))


def _ffn_up_kernel(h_ref, wg_ref, wu_ref, wd_ref, a_ref, wdb_ref, w_ref):
    th = wg_ref.shape[2]

    @pl.when(pl.program_id(1) == 0)
    def _():
        w_ref[:, 0:th] = wg_ref[0].astype(BF16)
        w_ref[:, th:] = wu_ref[0].astype(BF16)
        wdb_ref[...] = wd_ref[0].astype(BF16)

    gu = _dot(h_ref[...], w_ref[...])
    gate = gu[:, 0:th]
    a_ref[...] = (gate * jax.nn.sigmoid(gate) * gu[:, th:]).astype(a_ref.dtype)


def _ffn_up(h, wg_stack, wu_stack, wd_stack, layer, tm=1024):
    m, d = h.shape
    hidden = wg_stack.shape[2]
    th = hidden // FFN_TILES
    w_tile = pl.BlockSpec((1, d, th), lambda j, i: (layer, 0, j))
    return pl.pallas_call(
        _ffn_up_kernel,
        grid=(FFN_TILES, m // tm),
        in_specs=[pl.BlockSpec((tm, d), lambda j, i: (i, 0)), w_tile, w_tile,
                  pl.BlockSpec((1, th, d), lambda j, i: (layer, j, 0))],
        out_specs=[pl.BlockSpec((tm, th), lambda j, i: (i, j)), pl.BlockSpec((th, d), lambda j, i: (j, 0))],
        out_shape=[jax.ShapeDtypeStruct((m, hidden), BF16), jax.ShapeDtypeStruct((hidden, d), BF16)],
        scratch_shapes=[pltpu.VMEM((d, 2 * th), BF16)],
        compiler_params=_params("arbitrary", "arbitrary"),
        name="ffn_up",
    )(h, wg_stack, wu_stack, wd_stack)


def _ffn_down_kernel(a_ref, w_ref, x_ref, gpost_ref, gnext_ref, xo_ref, ho_ref):
    f = _dot(a_ref[...], w_ref[...])
    xn = x_ref[...] + _rms(f, gpost_ref[...])
    xo_ref[...] = xn
    ho_ref[...] = _rms(xn, gnext_ref[...]).astype(ho_ref.dtype)


def _ffn_down(act, wd, x, g_post, g_next, tm=512):
    m, d = x.shape
    hidden = act.shape[1]
    row = lambda i: (i, 0)
    const = lambda i: (0, 0)
    return pl.pallas_call(
        _ffn_down_kernel,
        grid=(m // tm,),
        in_specs=[pl.BlockSpec((tm, hidden), row),
                  pl.BlockSpec((hidden, d), const, pipeline_mode=pl.Buffered(1)),
                  pl.BlockSpec((tm, d), row),
                  pl.BlockSpec((1, d), const), pl.BlockSpec((1, d), const)],
        out_specs=[pl.BlockSpec((tm, d), row), pl.BlockSpec((tm, d), row)],
        out_shape=[jax.ShapeDtypeStruct((m, d), F32), jax.ShapeDtypeStruct((m, d), BF16)],
        compiler_params=pltpu.CompilerParams(dimension_semantics=("parallel",), vmem_limit_bytes=VMEM_LIMIT_FFN),
        name="ffn_down",
    )(act, wd, x, g_post.reshape(1, d), g_next.reshape(1, d))


def kernel(x, pre_mix_g, w_in, ret_gn_g, diff_lam_q1, diff_lam_k1, diff_lam_q2, diff_lam_k2, diff_subln_g,
           sgu_ln_g, sgu_ln_b, sgu_w, sgu_b, w_out, post_mix_g, pre_ffn_g, w_gate, w_up, w_down, post_ffn_g):
    batch, seq, d = x.shape
    depth = w_in.shape[0]
    xf = x.reshape(batch * seq, d)
    h = _rmsnorm(xf, pre_mix_g[0])
    for l in range(depth):
        lambda_init = 0.8 - 0.6 * math.exp(-0.3 * l)
        proj = _inproj(h, w_in, l)
        y_ret = _retention(proj, ret_gn_g[l], batch, seq)
        y_diff = _diffattn(proj, diff_lam_q1[l], diff_lam_k1[l], diff_lam_q2[l], diff_lam_k2[l],
                           diff_subln_g[l], lambda_init, batch, seq)
        y_sgu = _sgu(proj, sgu_ln_g[l], sgu_ln_b[l], sgu_w[l], sgu_b[l])
        xf, h = _outproj(y_ret, y_diff, y_sgu, w_out, l, xf, post_mix_g[l], pre_ffn_g[l])
        g_next = pre_mix_g[(l + 1) % depth]
        act, wd = _ffn_up(h, w_gate, w_up, w_down, l)
        xf, h = _ffn_down(act, wd, xf, post_ffn_g[l], g_next)
    return xf.reshape(batch, seq, d)
```

```python
import functools
import math

import numpy as np
import jax
import jax.numpy as jnp
from jax import lax
from jax.experimental import pallas as pl
from jax.experimental.pallas import tpu as pltpu

D_MODEL = 2048
HEAD_DIM = 128
RET_HEADS = 6
DIFF_HEADS = 6
SGU_GROUPS = 4
DIFF_MAP_DIM = 64
RET_W = RET_HEADS * HEAD_DIM
DIFF_W = DIFF_HEADS * HEAD_DIM
SGU_W = SGU_GROUPS * HEAD_DIM
IN_W = 4 * RET_W + 3 * DIFF_W + 2 * SGU_W
CHUNK = 128
RET_CHUNK = 256
EPS = 1e-6
LOG2E = math.log2(math.e)
NEG_BIG = -1e30
SUM_ROWS = 16
FFN_TILES = 11
QCHUNK = 256

COL_RQ, COL_RK, COL_RV, COL_RG = 0, 6, 12, 18
COL_DQ, COL_DK, COL_DV = 24, 30, 36
COL_SU, COL_SV = 42, 46

V7X_VMEM_BYTES = 64 * 1024 * 1024
VMEM_LIMIT = 56 * 1024 * 1024
VMEM_LIMIT_FFN = 60 * 1024 * 1024

BF16 = jnp.bfloat16
F32 = jnp.float32


def _dot(a, b):
    return jnp.dot(a, b, preferred_element_type=F32)


def _dot_nt(a, b):
    return lax.dot_general(a, b, (((1,), (1,)), ((), ())), preferred_element_type=F32)


def _dot_tn(a, b):
    return lax.dot_general(a, b, (((0,), (0,)), ((), ())), preferred_element_type=F32)


def _rms(x, g):
    return x * lax.rsqrt(jnp.mean(x * x, axis=-1, keepdims=True) + EPS) * g


def _params(*sem):
    return pltpu.CompilerParams(dimension_semantics=sem, vmem_limit_bytes=VMEM_LIMIT)


def _rmsnorm_kernel(x_ref, g_ref, o_ref):
    o_ref[...] = _rms(x_ref[...], g_ref[...]).astype(o_ref.dtype)


def _rmsnorm(x, g, tm=512):
    m, d = x.shape
    return pl.pallas_call(
        _rmsnorm_kernel,
        grid=(m // tm,),
        in_specs=[pl.BlockSpec((tm, d), lambda i: (i, 0)), pl.BlockSpec((1, d), lambda i: (0, 0))],
        out_specs=pl.BlockSpec((tm, d), lambda i: (i, 0)),
        out_shape=jax.ShapeDtypeStruct((m, d), BF16),
        compiler_params=_params("parallel"),
        name="rmsnorm",
    )(x, g.reshape(1, d))


def _inproj_kernel(h_ref, w_ref, wd_ref, o_ref, wdb_ref, wb_ref):
    @pl.when(pl.program_id(1) == 0)
    def _():
        wb_ref[...] = w_ref[0].astype(BF16)

    @pl.when(pl.program_id(0) == 0)
    def _():
        wdb_ref[...] = wd_ref[0].astype(BF16)

    o_ref[...] = _dot(h_ref[...], wb_ref[...]).astype(o_ref.dtype)


def _inproj(h, w_stack, wd_stack, layer, tm=1024, tn=1280):
    m, k = h.shape
    n = w_stack.shape[2]
    nrow = m // tm
    hidden, d = wd_stack.shape[1:]
    slab = hidden // nrow
    assert slab * nrow == hidden and slab % 16 == 0, (hidden, nrow)
    slab_idx = lambda j, i: jnp.where(j == 0, i, nrow - 1)
    return pl.pallas_call(
        _inproj_kernel,
        grid=(n // tn, nrow),
        in_specs=[pl.BlockSpec((tm, k), lambda j, i: (i, 0)),
                  pl.BlockSpec((1, k, tn), lambda j, i: (layer, 0, j)),
                  pl.BlockSpec((1, slab, d), lambda j, i: (layer, slab_idx(j, i), 0))],
        out_specs=[pl.BlockSpec((tm, tn), lambda j, i: (i, j)),
                   pl.BlockSpec((slab, d), lambda j, i: (slab_idx(j, i), 0))],
        out_shape=[jax.ShapeDtypeStruct((m, n), BF16), jax.ShapeDtypeStruct((hidden, d), BF16)],
        scratch_shapes=[pltpu.VMEM((k, tn), BF16)],
        compiler_params=_params("arbitrary", "arbitrary"),
        name="inproj",
    )(h, w_stack, wd_stack)


def _retention_kernel(lg_ref, q_ref, k_ref, v_ref, g_ref, gn_ref, o_ref, r_ref, *, tb):
    pair = pl.program_id(1)
    t = pl.program_id(2)

    @pl.when(t == 0)
    def _():
        r_ref[...] = jnp.zeros_like(r_ref)

    scale = HEAD_DIM ** -0.5
    row = lax.broadcasted_iota(jnp.int32, (RET_CHUNK, RET_CHUNK), 0)
    col = lax.broadcasted_iota(jnp.int32, (RET_CHUNK, RET_CHUNK), 1)
    rel = (row - col).astype(F32)
    pos = lax.broadcasted_iota(jnp.int32, (RET_CHUNK, 1), 0).astype(F32)
    consts = []
    for i in range(2):
        lg = lg_ref[2 * pair + i]
        decay = jnp.where(rel >= 0, jnp.exp(lg * jnp.maximum(rel, 0.0)), 0.0) * scale
        zeta = jnp.exp(lg * (RET_CHUNK - 1.0 - pos)) * scale
        xi = jnp.exp(lg * (pos + 1.0))
        chunk_decay = jnp.exp(jnp.full((1, 1), lg * RET_CHUNK, F32))
        consts.append((decay, zeta, xi, chunk_decay))

    for c in range(tb // RET_CHUNK):
        sl = slice(c * RET_CHUNK, (c + 1) * RET_CHUNK)
        for i in range(2):
            decay, zeta, xi, chunk_decay = consts[i]
            hs = slice(i * HEAD_DIM, (i + 1) * HEAD_DIM)
            q = q_ref[sl, hs]
            k = k_ref[sl, hs]
            v = v_ref[sl, hs]
            scores = _dot_nt(q, k) * decay
            inner = _dot(scores.astype(BF16), v)
            r_prev = r_ref[i]
            cross = _dot(q, r_prev.astype(BF16)) * xi
            kz = (k.astype(F32) * zeta).astype(BF16)
            r_ref[i] = _dot_tn(kz, v) + chunk_decay * r_prev
            y = inner + cross
            mu = jnp.mean(y, axis=-1, keepdims=True)
            yc = y - mu
            var = jnp.mean(yc * yc, axis=-1, keepdims=True)
            y = yc * lax.rsqrt(var + EPS) * gn_ref[0, :, hs]
            gate = g_ref[sl, hs].astype(F32)
            o_ref[sl, hs] = (gate * jax.nn.sigmoid(gate) * y).astype(o_ref.dtype)


def _retention(proj, gn_gain, batch, seq, tb=1024):
    m = proj.shape[0]
    nt = seq // tb
    log_gamma = np.log1p(-(2.0 ** (-5.0 - np.arange(RET_HEADS, dtype=np.float32)))).astype(np.float32)
    wide = 2 * HEAD_DIM

    def blk(col):
        return pl.BlockSpec((tb, wide), lambda b, h, t: (b * nt + t, col // 2 + h))

    return pl.pallas_call(
        functools.partial(_retention_kernel, tb=tb),
        grid=(batch, RET_HEADS // 2, nt),
        in_specs=[pl.BlockSpec(memory_space=pltpu.SMEM),
                  blk(COL_RQ), blk(COL_RK), blk(COL_RV), blk(COL_RG),
                  pl.BlockSpec((1, 1, wide), lambda b, h, t: (h, 0, 0))],
        out_specs=blk(0),
        out_shape=jax.ShapeDtypeStruct((m, RET_W), BF16),
        scratch_shapes=[pltpu.VMEM((2, HEAD_DIM, HEAD_DIM), F32)],
        compiler_params=_params("arbitrary", "arbitrary", "arbitrary"),
        name="retention",
    )(jnp.asarray(log_gamma), proj, proj, proj, proj, gn_gain.reshape(RET_HEADS // 2, 1, wide))


def _diffattn_kernel(slope_ref, q_ref, k_ref, v_ref, lq1_ref, lk1_ref, lq2_ref, lk2_ref, sg_ref, o_ref,
                     k1_ref, k2_ref, vt_ref, *bufs, bq, bk, lambda_init):
    acc = bufs[0:2]
    s_buf = [bufs[2:4], bufs[4:6]]
    p_buf = [bufs[6:8], bufs[8:10]]
    h = pl.program_id(1)
    qi = pl.program_id(2)
    nkb = k1_ref.shape[0]
    per_q = bq // bk

    slope2 = slope_ref[h] * LOG2E

    def block_bias(j):
        return jnp.full((1, 1), j * bk, jnp.int32).astype(F32) * slope2

    @pl.when(qi == 0)
    def _():
        lane = lax.broadcasted_iota(jnp.int32, (bk, HEAD_DIM), 1)
        bias = lax.broadcasted_iota(jnp.int32, (bk, HEAD_DIM), 0).astype(F32) * slope2
        hi = bias.astype(BF16).astype(F32)
        r1 = bias - hi
        mid = r1.astype(BF16).astype(F32)
        lo = r1 - mid
        zero = jnp.zeros_like(bias)
        e1 = jnp.where(lane == 64, hi, jnp.where(lane == 65, mid, jnp.where(lane == 66, lo, zero)))
        e2 = jnp.where(lane == 0, hi, jnp.where(lane == 1, mid, jnp.where(lane == 2, lo, zero)))

        def prep(j, carry):
            start = pl.multiple_of(j * bk, bk)
            kb = k_ref[pl.ds(start, bk), :].astype(F32)
            k1_ref[j] = jnp.where(lane < 64, kb, e1).astype(BF16)
            k2_ref[j] = jnp.where(lane >= 64, kb, e2).astype(BF16)
            vt_ref[j, 0:HEAD_DIM, :] = v_ref[pl.ds(start, bk), :].astype(F32).T.astype(BF16)
            vt_ref[j, HEAD_DIM:, :] = jnp.ones((SUM_ROWS, bk), BF16)
            return carry

        lax.fori_loop(0, nkb, prep, 0)

    lane = lax.broadcasted_iota(jnp.int32, (bq, HEAD_DIM), 1)
    qs = q_ref[...].astype(F32) * (DIFF_MAP_DIM ** -0.5 * LOG2E)
    one = jnp.ones_like(qs)
    zero = jnp.zeros_like(qs)
    qa = (jnp.where(lane < 64, qs, jnp.where(lane < 67, one, zero)).astype(BF16),
          jnp.where(lane >= 64, qs, jnp.where(lane < 3, one, zero)).astype(BF16))
    ka = (k1_ref, k2_ref)
    chunks = [slice(c * QCHUNK, (c + 1) * QCHUNK) for c in range(bq // QCHUNK)]
    krow = lax.broadcasted_iota(jnp.int32, (bk, QCHUNK), 0)
    qcol = lax.broadcasted_iota(jnp.int32, (bk, QCHUNK), 1)

    def issue_scores(mp, j, s_out, diag=None):
        mbs = []
        for cs in chunks:
            first_key = 0 if diag is None else diag * bk
            if first_key >= cs.stop:
                mbs.append(jnp.full((1, QCHUNK), 2 * NEG_BIG, F32))
                continue
            s = _dot_nt(ka[mp][j], qa[mp][cs, :])
            if diag is not None and first_key + bk - 1 > cs.start:
                s = jnp.where(krow + first_key <= qcol + cs.start, s, 2 * NEG_BIG)
            s_out[mp][:, cs] = s
            mbs.append(jnp.max(s, axis=0, keepdims=True))
        return jnp.concatenate(mbs, axis=1) + block_bias(j)

    def live(cs, diag):
        return diag is None or diag * bk < cs.stop

    def stage(slot, carry, cur, nxt, pending, cur_diag=None, nxt_diag=None, pend_diag=None, first=False):
        s_in, p_in = s_buf[slot], p_buf[slot]
        s_out, p_out = s_buf[1 - slot], p_buf[1 - slot]
        ms, alphas, mbs, dep = carry[0:6:3], carry[1:6:3], carry[2:6:3], carry[6]
        mns = [jnp.maximum(ms[mp], mbs[mp]) for mp in range(2)]
        shift = dep - block_bias(cur)
        new_mb = []
        for mp in range(2):
            new_mb.append(mbs[mp] if nxt is None else issue_scores(mp, nxt, s_out, nxt_diag))
            for cs in chunks:
                if live(cs, cur_diag):
                    p = jnp.exp2(s_in[mp][:, cs] - (mns[mp][:, cs] + shift))
                    p_out[mp][:, cs] = p.astype(BF16)
                else:
                    p_out[mp][:, cs] = jnp.zeros((bk, QCHUNK), BF16)
        if pending is not None:
            for mp in range(2):
                for cs in chunks:
                    if live(cs, pend_diag):
                        pv = _dot(vt_ref[pending], p_in[mp][:, cs])
                        acc[mp][:, cs] = pv if first else alphas[mp][:, cs] * acc[mp][:, cs] + pv
        out = []
        for mp in range(2):
            out += [mns[mp], jnp.exp2(ms[mp] - mns[mp]), new_mb[mp]]
        return tuple(out) + (p[0:1, :] * 0.0,)

    d0 = per_q * qi
    nfull = per_q * qi
    carry = []
    for mp in range(2):
        carry += [jnp.full((1, bq), NEG_BIG, F32), jnp.ones((1, bq), F32), issue_scores(mp, d0, s_buf[0], diag=0)]
    carry = tuple(carry) + (jnp.zeros((1, QCHUNK), F32),)
    for d in range(per_q):
        more = d + 1 < per_q
        carry = stage(d % 2, carry, d0 + d, d0 + d + 1 if more else 0, d0 + d - 1 if d else None,
                      cur_diag=d, nxt_diag=d + 1 if more else None, pend_diag=d - 1 if d else None, first=d == 1)

    def pair(t, carry, issue_last):
        carry = stage(0, carry, t, t + 1, jnp.where(t == 0, d0 + per_q - 1, t - 1))
        return stage(1, carry, t + 1, t + 2 if issue_last else None, t)

    carry = lax.fori_loop(0, jnp.maximum(nfull // 2 - 1, 0), lambda u, c: pair(2 * u, c, True), carry)
    carry = lax.cond(nfull > 0, lambda c: pair(nfull - 2, c, False), lambda c: c, carry)
    last = jnp.where(qi == 0, per_q - 1, nfull - 1)
    for mp in range(2):
        for cs in chunks:
            acc[mp][:, cs] = carry[3 * mp + 1][:, cs] * acc[mp][:, cs] + _dot(vt_ref[last], p_buf[0][mp][:, cs])

    lam = (jnp.exp(jnp.sum(lq1_ref[...] * lk1_ref[...], axis=-1, keepdims=True))
           - jnp.exp(jnp.sum(lq2_ref[...] * lk2_ref[...], axis=-1, keepdims=True)) + lambda_init)
    l1 = acc[0][HEAD_DIM:HEAD_DIM + 1, :]
    l2 = acc[1][HEAD_DIM:HEAD_DIM + 1, :]
    o_t = acc[0][0:HEAD_DIM, :] / l1 - lam * (acc[1][0:HEAD_DIM, :] / l2)
    o = o_t.T
    o = _rms(o, sg_ref[0]) * (1.0 - lambda_init)
    o_ref[...] = o.astype(o_ref.dtype)


def _diffattn(proj, lq1, lk1, lq2, lk2, subln_g, lambda_init, batch, seq, bq=2048, bk=512):
    m = proj.shape[0]
    nq, nk = seq // bq, seq // bk
    slopes = (2.0 ** (-8.0 * np.arange(1, DIFF_HEADS + 1, dtype=np.float32) / DIFF_HEADS)).astype(np.float32)
    vec = pl.BlockSpec((1, DIFF_MAP_DIM), lambda b, h, i: (0, 0))
    return pl.pallas_call(
        functools.partial(_diffattn_kernel, bq=bq, bk=bk, lambda_init=lambda_init),
        grid=(batch, DIFF_HEADS, nq),
        in_specs=[pl.BlockSpec(memory_space=pltpu.SMEM),
                  pl.BlockSpec((bq, HEAD_DIM), lambda b, h, i: (b * nq + i, COL_DQ + h)),
                  pl.BlockSpec((seq, HEAD_DIM), lambda b, h, i: (b, COL_DK + h)),
                  pl.BlockSpec((seq, HEAD_DIM), lambda b, h, i: (b, COL_DV + h)),
                  vec, vec, vec, vec,
                  pl.BlockSpec((1, 1, HEAD_DIM), lambda b, h, i: (h, 0, 0))],
        out_specs=pl.BlockSpec((bq, HEAD_DIM), lambda b, h, i: (b * nq + i, h)),
        out_shape=jax.ShapeDtypeStruct((m, DIFF_W), BF16),
        scratch_shapes=[pltpu.VMEM((nk, bk, HEAD_DIM), BF16),
                        pltpu.VMEM((nk, bk, HEAD_DIM), BF16),
                        pltpu.VMEM((nk, HEAD_DIM + SUM_ROWS, bk), BF16),
                        *[pltpu.VMEM((HEAD_DIM + SUM_ROWS, bq), F32)] * 2,
                        *[pltpu.VMEM((bk, bq), F32)] * 4,
                        *[pltpu.VMEM((bk, bq), BF16)] * 4],
        compiler_params=_params("arbitrary", "arbitrary", "arbitrary"),
        name="diffattn",
    )(jnp.asarray(slopes), proj, proj, proj,
      lq1.reshape(1, -1), lk1.reshape(1, -1), lq2.reshape(1, -1), lk2.reshape(1, -1),
      subln_g.reshape(DIFF_HEADS, 1, HEAD_DIM))


def _sgu_kernel(u_ref, v_ref, lng_ref, lnb_ref, w_ref, b_ref, o_ref, *, tb):
    row = lax.broadcasted_iota(jnp.int32, (CHUNK, CHUNK), 0)
    col = lax.broadcasted_iota(jnp.int32, (CHUNK, CHUNK), 1)
    for i in range(2):
        gs = slice(i * HEAD_DIM, (i + 1) * HEAD_DIM)
        w = jnp.where(row >= col, w_ref[i], 0.0).astype(BF16)
        bias = b_ref[i]
        lng = lng_ref[0, :, gs]
        lnb = lnb_ref[0, :, gs]
        for c in range(tb // CHUNK):
            sl = slice(c * CHUNK, (c + 1) * CHUNK)
            u = jax.nn.gelu(u_ref[sl, gs].astype(F32))
            v = jax.nn.gelu(v_ref[sl, gs].astype(F32))
            mu = jnp.mean(v, axis=-1, keepdims=True)
            vc = v - mu
            var = jnp.mean(vc * vc, axis=-1, keepdims=True)
            vn = vc * lax.rsqrt(var + EPS) * lng + lnb
            mixed = _dot(w, vn.astype(BF16)) + bias
            o_ref[sl, gs] = (u * mixed).astype(o_ref.dtype)


def _sgu(proj, ln_g, ln_b, w_s, b_s, tb=1024):
    m = proj.shape[0]
    pairs = SGU_GROUPS // 2
    wide = 2 * HEAD_DIM
    per_pair = lambda i, j: (j, 0, 0)
    return pl.pallas_call(
        functools.partial(_sgu_kernel, tb=tb),
        grid=(m // tb, pairs),
        in_specs=[pl.BlockSpec((tb, wide), lambda i, j: (i, COL_SU // 2 + j)),
                  pl.BlockSpec((tb, wide), lambda i, j: (i, COL_SV // 2 + j)),
                  pl.BlockSpec((1, 1, wide), per_pair),
                  pl.BlockSpec((1, 1, wide), per_pair),
                  pl.BlockSpec((2, CHUNK, CHUNK), per_pair),
                  pl.BlockSpec((2, CHUNK, 1), per_pair)],
        out_specs=pl.BlockSpec((tb, wide), lambda i, j: (i, j)),
        out_shape=jax.ShapeDtypeStruct((m, SGU_W), BF16),
        compiler_params=_params("parallel", "arbitrary"),
        name="sgu",
    )(proj, proj, ln_g.reshape(pairs, 1, wide), ln_b.reshape(pairs, 1, wide), w_s,
      b_s.reshape(SGU_GROUPS, CHUNK, 1))


def _outproj_kernel(yr_ref, yd_ref, ys_ref, wf_ref, x_ref, gpost_ref, gnext_ref, xo_ref, ho_ref, w_ref):
    @pl.when(pl.program_id(0) == 0)
    def _():
        w_ref[...] = wf_ref[0].astype(BF16)

    mix = (_dot(yr_ref[...], w_ref[0:RET_W, :])
           + _dot(yd_ref[...], w_ref[RET_W:RET_W + DIFF_W, :])
           + _dot(ys_ref[...], w_ref[RET_W + DIFF_W:D_MODEL, :]))
    xn = x_ref[...] + _rms(mix, gpost_ref[...])
    xo_ref[...] = xn
    ho_ref[...] = _rms(xn, gnext_ref[...]).astype(ho_ref.dtype)


def _outproj(y_ret, y_diff, y_sgu, w_stack, layer, x, g_post, g_next, tm=512):
    m, d = x.shape
    row = lambda i: (i, 0)
    const = lambda i: (0, 0)
    return pl.pallas_call(
        _outproj_kernel,
        grid=(m // tm,),
        in_specs=[pl.BlockSpec((tm, RET_W), row), pl.BlockSpec((tm, DIFF_W), row), pl.BlockSpec((tm, SGU_W), row),
                  pl.BlockSpec((1, d, d), lambda i: (layer, 0, 0), pipeline_mode=pl.Buffered(1)),
                  pl.BlockSpec((tm, d), row),
                  pl.BlockSpec((1, d), const), pl.BlockSpec((1, d), const)],
        out_specs=[pl.BlockSpec((tm, d), row), pl.BlockSpec((tm, d), row)],
        out_shape=[jax.ShapeDtypeStruct((m, d), F32), jax.ShapeDtypeStruct((m, d), BF16)],
        scratch_shapes=[pltpu.VMEM((d, d), BF16)],
        compiler_params=_params("arbitrary"),
        name="outproj",
    )(y_ret, y_diff, y_sgu, w_stack, x, g_post.reshape(1, d), g_next.reshape(1, d))


def _ffn_up_kernel(h_ref, wg_ref, wu_ref, a_ref, w_ref):
    th = wg_ref.shape[2]

    @pl.when(pl.program_id(1) == 0)
    def _():
        w_ref[:, 0:th] = wg_ref[0].astype(BF16)
        w_ref[:, th:] = wu_ref[0].astype(BF16)

    gu = _dot(h_ref[...], w_ref[...])
    gate = gu[:, 0:th]
    a_ref[...] = (gate * jax.nn.sigmoid(gate) * gu[:, th:]).astype(a_ref.dtype)


def _ffn_up(h, wg_stack, wu_stack, layer, tm=2048):
    m, d = h.shape
    hidden = wg_stack.shape[2]
    th = hidden // FFN_TILES
    w_tile = pl.BlockSpec((1, d, th), lambda j, i: (layer, 0, j))
    return pl.pallas_call(
        _ffn_up_kernel,
        grid=(FFN_TILES, m // tm),
        in_specs=[pl.BlockSpec((tm, d), lambda j, i: (i, 0)), w_tile, w_tile],
        out_specs=pl.BlockSpec((tm, th), lambda j, i: (i, j)),
        out_shape=jax.ShapeDtypeStruct((m, hidden), BF16),
        scratch_shapes=[pltpu.VMEM((d, 2 * th), BF16)],
        compiler_params=_params("arbitrary", "arbitrary"),
        name="ffn_up",
    )(h, wg_stack, wu_stack)


def _ffn_down_kernel(a_ref, w_ref, x_ref, gpost_ref, gnext_ref, xo_ref, *maybe_ho_ref):
    f = _dot(a_ref[...], w_ref[...])
    xn = x_ref[...] + _rms(f, gpost_ref[...])
    xo_ref[...] = xn
    for ho_ref in maybe_ho_ref:
        ho_ref[...] = _rms(xn, gnext_ref[...]).astype(ho_ref.dtype)


def _ffn_down(act, wd, x, g_post, g_next, emit_h, tm=512):
    m, d = x.shape
    hidden = act.shape[1]
    row = lambda i: (i, 0)
    const = lambda i: (0, 0)
    out = pl.pallas_call(
        _ffn_down_kernel,
        grid=(m // tm,),
        in_specs=[pl.BlockSpec((tm, hidden), row),
                  pl.BlockSpec((hidden, d), const, pipeline_mode=pl.Buffered(1)),
                  pl.BlockSpec((tm, d), row),
                  pl.BlockSpec((1, d), const), pl.BlockSpec((1, d), const)],
        out_specs=[pl.BlockSpec((tm, d), row)] + [pl.BlockSpec((tm, d), row)] * emit_h,
        out_shape=[jax.ShapeDtypeStruct((m, d), F32)] + [jax.ShapeDtypeStruct((m, d), BF16)] * emit_h,
        compiler_params=pltpu.CompilerParams(dimension_semantics=("parallel",), vmem_limit_bytes=VMEM_LIMIT_FFN),
        name="ffn_down",
    )(act, wd, x, g_post.reshape(1, d), g_next.reshape(1, d))
    return (out[0], out[1]) if emit_h else (out[0], None)


def kernel(x, pre_mix_g, w_in, ret_gn_g, diff_lam_q1, diff_lam_k1, diff_lam_q2, diff_lam_k2, diff_subln_g,
           sgu_ln_g, sgu_ln_b, sgu_w, sgu_b, w_out, post_mix_g, pre_ffn_g, w_gate, w_up, w_down, post_ffn_g):
    batch, seq, d = x.shape
    depth = w_in.shape[0]
    xf = x.reshape(batch * seq, d)
    h = _rmsnorm(xf, pre_mix_g[0])
    for l in range(depth):
        lambda_init = 0.8 - 0.6 * math.exp(-0.3 * l)
        proj, wd = _inproj(h, w_in, w_down, l)
        y_ret = _retention(proj, ret_gn_g[l], batch, seq)
        y_diff = _diffattn(proj, diff_lam_q1[l], diff_lam_k1[l], diff_lam_q2[l], diff_lam_k2[l],
                           diff_subln_g[l], lambda_init, batch, seq)
        y_sgu = _sgu(proj, sgu_ln_g[l], sgu_ln_b[l], sgu_w[l], sgu_b[l])
        xf, h = _outproj(y_ret, y_diff, y_sgu, w_out, l, xf, post_mix_g[l], pre_ffn_g[l])
        g_next = pre_mix_g[(l + 1) % depth]
        act = _ffn_up(h, w_gate, w_up, l)
        xf, h = _ffn_down(act, wd, xf, post_ffn_g[l], g_next, emit_h=l + 1 < depth)
    return xf.reshape(batch, seq, d)
```

```python
import functools
import math

import numpy as np
import jax
import jax.numpy as jnp
from jax import lax
from jax.experimental import pallas as pl
from jax.experimental.pallas import tpu as pltpu

D_MODEL = 2048
HEAD_DIM = 128
RET_HEADS = 6
DIFF_HEADS = 6
SGU_GROUPS = 4
DIFF_MAP_DIM = 64
RET_W = RET_HEADS * HEAD_DIM
DIFF_W = DIFF_HEADS * HEAD_DIM
SGU_W = SGU_GROUPS * HEAD_DIM
IN_W = 4 * RET_W + 3 * DIFF_W + 2 * SGU_W
CHUNK = 128
RET_CHUNK = 256
EPS = 1e-6
LOG2E = math.log2(math.e)
NEG_BIG = -1e30
SUM_ROWS = 16
FFN_TILES = 11
QCHUNK = 256

COL_RQ, COL_RK, COL_RV, COL_RG = 0, 6, 12, 18
COL_DQ, COL_DK, COL_DV = 24, 30, 36
COL_SU, COL_SV = 42, 46

V7X_VMEM_BYTES = 64 * 1024 * 1024
VMEM_LIMIT = 56 * 1024 * 1024
VMEM_LIMIT_FFN = 60 * 1024 * 1024
assert VMEM_LIMIT < VMEM_LIMIT_FFN < V7X_VMEM_BYTES

BF16 = jnp.bfloat16
F32 = jnp.float32


def _dot(a, b):
    return jnp.dot(a, b, preferred_element_type=F32)


def _dot_nt(a, b):
    return lax.dot_general(a, b, (((1,), (1,)), ((), ())), preferred_element_type=F32)


def _dot_tn(a, b):
    return lax.dot_general(a, b, (((0,), (0,)), ((), ())), preferred_element_type=F32)


def _rms(x, g):
    return x * lax.rsqrt(jnp.mean(x * x, axis=-1, keepdims=True) + EPS) * g


def _params(*sem):
    return pltpu.CompilerParams(dimension_semantics=sem, vmem_limit_bytes=VMEM_LIMIT)


def _rmsnorm_kernel(x_ref, g_ref, o_ref):
    o_ref[...] = _rms(x_ref[...], g_ref[...]).astype(o_ref.dtype)


def _rmsnorm(x, g, tm=512):
    m, d = x.shape
    return pl.pallas_call(
        _rmsnorm_kernel,
        grid=(m // tm,),
        in_specs=[pl.BlockSpec((tm, d), lambda i: (i, 0)), pl.BlockSpec((1, d), lambda i: (0, 0))],
        out_specs=pl.BlockSpec((tm, d), lambda i: (i, 0)),
        out_shape=jax.ShapeDtypeStruct((m, d), BF16),
        compiler_params=_params("parallel"),
        name="rmsnorm",
    )(x, g.reshape(1, d))


def _inproj_kernel(h_ref, w_ref, wd_ref, o_ref, wdb_ref, wb_ref):
    @pl.when(pl.program_id(1) == 0)
    def _():
        wb_ref[...] = w_ref[0].astype(BF16)

    @pl.when(pl.program_id(0) == 0)
    def _():
        wdb_ref[...] = wd_ref[0].astype(BF16)

    o_ref[...] = _dot(h_ref[...], wb_ref[...]).astype(o_ref.dtype)


def _inproj(h, w_stack, wd_stack, layer, tm=1024, tn=1280):
    m, k = h.shape
    n = w_stack.shape[2]
    nrow = m // tm
    hidden, d = wd_stack.shape[1:]
    slab = hidden // nrow
    assert slab * nrow == hidden and slab % 16 == 0, (hidden, nrow)
    slab_idx = lambda j, i: jnp.where(j == 0, i, nrow - 1)
    return pl.pallas_call(
        _inproj_kernel,
        grid=(n // tn, nrow),
        in_specs=[pl.BlockSpec((tm, k), lambda j, i: (i, 0)),
                  pl.BlockSpec((1, k, tn), lambda j, i: (layer, 0, j)),
                  pl.BlockSpec((1, slab, d), lambda j, i: (layer, slab_idx(j, i), 0))],
        out_specs=[pl.BlockSpec((tm, tn), lambda j, i: (i, j)),
                   pl.BlockSpec((slab, d), lambda j, i: (slab_idx(j, i), 0))],
        out_shape=[jax.ShapeDtypeStruct((m, n), BF16), jax.ShapeDtypeStruct((hidden, d), BF16)],
        scratch_shapes=[pltpu.VMEM((k, tn), BF16)],
        compiler_params=_params("arbitrary", "arbitrary"),
        name="inproj",
    )(h, w_stack, wd_stack)


def _retention_kernel(lg_ref, q_ref, k_ref, v_ref, g_ref, gn_ref, o_ref, r_ref, *, tb):
    pair = pl.program_id(1)
    t = pl.program_id(2)

    @pl.when(t == 0)
    def _():
        r_ref[...] = jnp.zeros_like(r_ref)

    scale = HEAD_DIM ** -0.5
    row = lax.broadcasted_iota(jnp.int32, (RET_CHUNK, RET_CHUNK), 0)
    col = lax.broadcasted_iota(jnp.int32, (RET_CHUNK, RET_CHUNK), 1)
    rel = (row - col).astype(F32)
    pos = lax.broadcasted_iota(jnp.int32, (RET_CHUNK, 1), 0).astype(F32)
    consts = []
    for i in range(2):
        lg = lg_ref[2 * pair + i]
        decay = jnp.where(rel >= 0, jnp.exp(lg * jnp.maximum(rel, 0.0)), 0.0) * scale
        zeta = jnp.exp(lg * (RET_CHUNK - 1.0 - pos)) * scale
        xi = jnp.exp(lg * (pos + 1.0))
        chunk_decay = jnp.exp(jnp.full((1, 1), lg * RET_CHUNK, F32))
        consts.append((decay, zeta, xi, chunk_decay))

    for c in range(tb // RET_CHUNK):
        sl = slice(c * RET_CHUNK, (c + 1) * RET_CHUNK)
        for i in range(2):
            decay, zeta, xi, chunk_decay = consts[i]
            hs = slice(i * HEAD_DIM, (i + 1) * HEAD_DIM)
            q = q_ref[sl, hs]
            k = k_ref[sl, hs]
            v = v_ref[sl, hs]
            scores = _dot_nt(q, k) * decay
            inner = _dot(scores.astype(BF16), v)
            r_prev = r_ref[i]
            cross = _dot(q, r_prev.astype(BF16)) * xi
            kz = (k.astype(F32) * zeta).astype(BF16)
            r_ref[i] = _dot_tn(kz, v) + chunk_decay * r_prev
            y = inner + cross
            mu = jnp.mean(y, axis=-1, keepdims=True)
            yc = y - mu
            var = jnp.mean(yc * yc, axis=-1, keepdims=True)
            y = yc * lax.rsqrt(var + EPS) * gn_ref[0, :, hs]
            gate = g_ref[sl, hs].astype(F32)
            o_ref[sl, hs] = (gate * jax.nn.sigmoid(gate) * y).astype(o_ref.dtype)


def _retention(proj, gn_gain, batch, seq, tb=1024):
    m = proj.shape[0]
    nt = seq // tb
    log_gamma = np.log1p(-(2.0 ** (-5.0 - np.arange(RET_HEADS, dtype=np.float32)))).astype(np.float32)
    wide = 2 * HEAD_DIM

    def blk(col):
        return pl.BlockSpec((tb, wide), lambda b, h, t: (b * nt + t, col // 2 + h))

    return pl.pallas_call(
        functools.partial(_retention_kernel, tb=tb),
        grid=(batch, RET_HEADS // 2, nt),
        in_specs=[pl.BlockSpec(memory_space=pltpu.SMEM),
                  blk(COL_RQ), blk(COL_RK), blk(COL_RV), blk(COL_RG),
                  pl.BlockSpec((1, 1, wide), lambda b, h, t: (h, 0, 0))],
        out_specs=blk(0),
        out_shape=jax.ShapeDtypeStruct((m, RET_W), BF16),
        scratch_shapes=[pltpu.VMEM((2, HEAD_DIM, HEAD_DIM), F32)],
        compiler_params=_params("arbitrary", "arbitrary", "arbitrary"),
        name="retention",
    )(jnp.asarray(log_gamma), proj, proj, proj, proj, gn_gain.reshape(RET_HEADS // 2, 1, wide))


def _diffattn_kernel(slope_ref, q_ref, k_ref, v_ref, lq1_ref, lk1_ref, lq2_ref, lk2_ref, sg_ref, o_ref,
                     k1_ref, k2_ref, vt_ref, *bufs, bq, bk, lambda_init):
    acc = bufs[0:2]
    s_buf = [bufs[2:4], bufs[4:6]]
    p_buf = [bufs[6:8], bufs[8:10]]
    h = pl.program_id(1)
    qi = pl.program_id(2)
    nkb = k1_ref.shape[0]
    per_q = bq // bk

    slope2 = slope_ref[h] * LOG2E

    def block_bias(j):
        return jnp.full((1, 1), j * bk, jnp.int32).astype(F32) * slope2

    @pl.when(qi == 0)
    def _():
        lane = lax.broadcasted_iota(jnp.int32, (bk, HEAD_DIM), 1)
        bias = lax.broadcasted_iota(jnp.int32, (bk, HEAD_DIM), 0).astype(F32) * slope2
        hi = bias.astype(BF16).astype(F32)
        r1 = bias - hi
        mid = r1.astype(BF16).astype(F32)
        lo = r1 - mid
        zero = jnp.zeros_like(bias)
        e1 = jnp.where(lane == 64, hi, jnp.where(lane == 65, mid, jnp.where(lane == 66, lo, zero)))
        e2 = jnp.where(lane == 0, hi, jnp.where(lane == 1, mid, jnp.where(lane == 2, lo, zero)))

        def prep(j, carry):
            start = pl.multiple_of(j * bk, bk)
            kb = k_ref[pl.ds(start, bk), :].astype(F32)
            k1_ref[j] = jnp.where(lane < 64, kb, e1).astype(BF16)
            k2_ref[j] = jnp.where(lane >= 64, kb, e2).astype(BF16)
            vt_ref[j, 0:HEAD_DIM, :] = v_ref[pl.ds(start, bk), :].astype(F32).T.astype(BF16)
            vt_ref[j, HEAD_DIM:, :] = jnp.ones((SUM_ROWS, bk), BF16)
            return carry

        lax.fori_loop(0, nkb, prep, 0)

    lane = lax.broadcasted_iota(jnp.int32, (bq, HEAD_DIM), 1)
    qs = q_ref[...].astype(F32) * (DIFF_MAP_DIM ** -0.5 * LOG2E)
    one = jnp.ones_like(qs)
    zero = jnp.zeros_like(qs)
    qa = (jnp.where(lane < 64, qs, jnp.where(lane < 67, one, zero)).astype(BF16),
          jnp.where(lane >= 64, qs, jnp.where(lane < 3, one, zero)).astype(BF16))
    ka = (k1_ref, k2_ref)
    chunks = [slice(c * QCHUNK, (c + 1) * QCHUNK) for c in range(bq // QCHUNK)]
    krow = lax.broadcasted_iota(jnp.int32, (bk, QCHUNK), 0)
    qcol = lax.broadcasted_iota(jnp.int32, (bk, QCHUNK), 1)

    def issue_scores(mp, j, s_out, diag=None):
        mbs = []
        for cs in chunks:
            first_key = 0 if diag is None else diag * bk
            if first_key >= cs.stop:
                mbs.append(jnp.full((1, QCHUNK), 2 * NEG_BIG, F32))
                continue
            s = _dot_nt(ka[mp][j], qa[mp][cs, :])
            if diag is not None and first_key + bk - 1 > cs.start:
                s = jnp.where(krow + first_key <= qcol + cs.start, s, 2 * NEG_BIG)
            s_out[mp][:, cs] = s
            mbs.append(jnp.max(s, axis=0, keepdims=True))
        return jnp.concatenate(mbs, axis=1) + block_bias(j)

    def live(cs, diag):
        return diag is None or diag * bk < cs.stop

    def stage(slot, carry, cur, nxt, pending, cur_diag=None, nxt_diag=None, pend_diag=None, first=False):
        s_in, p_in = s_buf[slot], p_buf[slot]
        s_out, p_out = s_buf[1 - slot], p_buf[1 - slot]
        ms, alphas, mbs, dep = carry[0:6:3], carry[1:6:3], carry[2:6:3], carry[6]
        mns = [jnp.maximum(ms[mp], mbs[mp]) for mp in range(2)]
        shift = dep - block_bias(cur)
        new_mb = []
        for mp in range(2):
            new_mb.append(issue_scores(mp, nxt, s_out, nxt_diag))
            for cs in chunks:
                if live(cs, cur_diag):
                    p = jnp.exp2(s_in[mp][:, cs] - (mns[mp][:, cs] + shift))
                    p_out[mp][:, cs] = p.astype(BF16)
                else:
                    p_out[mp][:, cs] = jnp.zeros((bk, QCHUNK), BF16)
        if pending is not None:
            for mp in range(2):
                for cs in chunks:
                    if live(cs, pend_diag):
                        pv = _dot(vt_ref[pending], p_in[mp][:, cs])
                        acc[mp][:, cs] = pv if first else alphas[mp][:, cs] * acc[mp][:, cs] + pv
        out = []
        for mp in range(2):
            out += [mns[mp], jnp.exp2(ms[mp] - mns[mp]), new_mb[mp]]
        return tuple(out) + (p[0:1, :] * 0.0,)

    d0 = per_q * qi
    nfull = per_q * qi
    carry = []
    for mp in range(2):
        carry += [jnp.full((1, bq), NEG_BIG, F32), jnp.ones((1, bq), F32), issue_scores(mp, d0, s_buf[0], diag=0)]
    carry = tuple(carry) + (jnp.zeros((1, QCHUNK), F32),)
    for d in range(per_q):
        more = d + 1 < per_q
        carry = stage(d % 2, carry, d0 + d, d0 + d + 1 if more else 0, d0 + d - 1 if d else None,
                      cur_diag=d, nxt_diag=d + 1 if more else None, pend_diag=d - 1 if d else None, first=d == 1)

    def pair(u, carry):
        t = 2 * u
        carry = stage(0, carry, t, t + 1, jnp.where(t == 0, d0 + per_q - 1, t - 1))
        return stage(1, carry, t + 1, jnp.minimum(t + 2, nfull - 1), t)

    carry = lax.fori_loop(0, nfull // 2, pair, carry)
    last = jnp.where(qi == 0, per_q - 1, nfull - 1)
    for mp in range(2):
        for cs in chunks:
            acc[mp][:, cs] = carry[3 * mp + 1][:, cs] * acc[mp][:, cs] + _dot(vt_ref[last], p_buf[0][mp][:, cs])

    lam = (jnp.exp(jnp.sum(lq1_ref[...] * lk1_ref[...], axis=-1, keepdims=True))
           - jnp.exp(jnp.sum(lq2_ref[...] * lk2_ref[...], axis=-1, keepdims=True)) + lambda_init)
    l1 = acc[0][HEAD_DIM:HEAD_DIM + 1, :]
    l2 = acc[1][HEAD_DIM:HEAD_DIM + 1, :]
    o_t = acc[0][0:HEAD_DIM, :] * (1.0 / l1) - acc[1][0:HEAD_DIM, :] * (lam / l2)
    o = o_t.T
    o = _rms(o, sg_ref[0]) * (1.0 - lambda_init)
    o_ref[...] = o.astype(o_ref.dtype)


def _diffattn(proj, lq1, lk1, lq2, lk2, subln_g, lambda_init, batch, seq, bq=2048, bk=512):
    m = proj.shape[0]
    nq, nk = seq // bq, seq // bk
    assert nq * bq == seq and nk * bk == seq and (bq // bk) % 2 == 0 and bq % QCHUNK == 0, (seq, bq, bk)
    slopes = (2.0 ** (-8.0 * np.arange(1, DIFF_HEADS + 1, dtype=np.float32) / DIFF_HEADS)).astype(np.float32)
    vec = pl.BlockSpec((1, DIFF_MAP_DIM), lambda b, h, i: (0, 0))
    return pl.pallas_call(
        functools.partial(_diffattn_kernel, bq=bq, bk=bk, lambda_init=lambda_init),
        grid=(batch, DIFF_HEADS, nq),
        in_specs=[pl.BlockSpec(memory_space=pltpu.SMEM),
                  pl.BlockSpec((bq, HEAD_DIM), lambda b, h, i: (b * nq + i, COL_DQ + h)),
                  pl.BlockSpec((seq, HEAD_DIM), lambda b, h, i: (b, COL_DK + h)),
                  pl.BlockSpec((seq, HEAD_DIM), lambda b, h, i: (b, COL_DV + h)),
                  vec, vec, vec, vec,
                  pl.BlockSpec((1, 1, HEAD_DIM), lambda b, h, i: (h, 0, 0))],
        out_specs=pl.BlockSpec((bq, HEAD_DIM), lambda b, h, i: (b * nq + i, h)),
        out_shape=jax.ShapeDtypeStruct((m, DIFF_W), BF16),
        scratch_shapes=[pltpu.VMEM((nk, bk, HEAD_DIM), BF16),
                        pltpu.VMEM((nk, bk, HEAD_DIM), BF16),
                        pltpu.VMEM((nk, HEAD_DIM + SUM_ROWS, bk), BF16),
                        *[pltpu.VMEM((HEAD_DIM + SUM_ROWS, bq), F32)] * 2,
                        *[pltpu.VMEM((bk, bq), F32)] * 4,
                        *[pltpu.VMEM((bk, bq), BF16)] * 4],
        compiler_params=_params("arbitrary", "arbitrary", "arbitrary"),
        name="diffattn",
    )(jnp.asarray(slopes), proj, proj, proj,
      lq1.reshape(1, -1), lk1.reshape(1, -1), lq2.reshape(1, -1), lk2.reshape(1, -1),
      subln_g.reshape(DIFF_HEADS, 1, HEAD_DIM))


def _sgu_kernel(u_ref, v_ref, lng_ref, lnb_ref, w_ref, b_ref, o_ref, *, tb):
    row = lax.broadcasted_iota(jnp.int32, (CHUNK, CHUNK), 0)
    col = lax.broadcasted_iota(jnp.int32, (CHUNK, CHUNK), 1)
    for i in range(2):
        gs = slice(i * HEAD_DIM, (i + 1) * HEAD_DIM)
        w = jnp.where(row >= col, w_ref[i], 0.0).astype(BF16)
        bias = b_ref[i]
        lng = lng_ref[0, :, gs]
        lnb = lnb_ref[0, :, gs]
        for c in range(tb // CHUNK):
            sl = slice(c * CHUNK, (c + 1) * CHUNK)
            u = jax.nn.gelu(u_ref[sl, gs].astype(F32))
            v = jax.nn.gelu(v_ref[sl, gs].astype(F32))
            mu = jnp.mean(v, axis=-1, keepdims=True)
            vc = v - mu
            var = jnp.mean(vc * vc, axis=-1, keepdims=True)
            vn = vc * lax.rsqrt(var + EPS) * lng + lnb
            mixed = _dot(w, vn.astype(BF16)) + bias
            o_ref[sl, gs] = (u * mixed).astype(o_ref.dtype)


def _sgu(proj, ln_g, ln_b, w_s, b_s, tb=1024):
    m = proj.shape[0]
    pairs = SGU_GROUPS // 2
    wide = 2 * HEAD_DIM
    per_pair = lambda i, j: (j, 0, 0)
    return pl.pallas_call(
        functools.partial(_sgu_kernel, tb=tb),
        grid=(m // tb, pairs),
        in_specs=[pl.BlockSpec((tb, wide), lambda i, j: (i, COL_SU // 2 + j)),
                  pl.BlockSpec((tb, wide), lambda i, j: (i, COL_SV // 2 + j)),
                  pl.BlockSpec((1, 1, wide), per_pair),
                  pl.BlockSpec((1, 1, wide), per_pair),
                  pl.BlockSpec((2, CHUNK, CHUNK), per_pair),
                  pl.BlockSpec((2, CHUNK, 1), per_pair)],
        out_specs=pl.BlockSpec((tb, wide), lambda i, j: (i, j)),
        out_shape=jax.ShapeDtypeStruct((m, SGU_W), BF16),
        compiler_params=_params("parallel", "arbitrary"),
        name="sgu",
    )(proj, proj, ln_g.reshape(pairs, 1, wide), ln_b.reshape(pairs, 1, wide), w_s,
      b_s.reshape(SGU_GROUPS, CHUNK, 1))


def _outproj_kernel(yr_ref, yd_ref, ys_ref, wf_ref, x_ref, gpost_ref, gnext_ref, xo_ref, ho_ref, w_ref):
    @pl.when(pl.program_id(0) == 0)
    def _():
        w_ref[...] = wf_ref[0].astype(BF16)

    mix = (_dot(yr_ref[...], w_ref[0:RET_W, :])
           + _dot(yd_ref[...], w_ref[RET_W:RET_W + DIFF_W, :])
           + _dot(ys_ref[...], w_ref[RET_W + DIFF_W:D_MODEL, :]))
    xn = x_ref[...] + _rms(mix, gpost_ref[...])
    xo_ref[...] = xn
    ho_ref[...] = _rms(xn, gnext_ref[...]).astype(ho_ref.dtype)


def _outproj(y_ret, y_diff, y_sgu, w_stack, layer, x, g_post, g_next, tm=512):
    m, d = x.shape
    row = lambda i: (i, 0)
    const = lambda i: (0, 0)
    return pl.pallas_call(
        _outproj_kernel,
        grid=(m // tm,),
        in_specs=[pl.BlockSpec((tm, RET_W), row), pl.BlockSpec((tm, DIFF_W), row), pl.BlockSpec((tm, SGU_W), row),
                  pl.BlockSpec((1, d, d), lambda i: (layer, 0, 0), pipeline_mode=pl.Buffered(1)),
                  pl.BlockSpec((tm, d), row),
                  pl.BlockSpec((1, d), const), pl.BlockSpec((1, d), const)],
        out_specs=[pl.BlockSpec((tm, d), row), pl.BlockSpec((tm, d), row)],
        out_shape=[jax.ShapeDtypeStruct((m, d), F32), jax.ShapeDtypeStruct((m, d), BF16)],
        scratch_shapes=[pltpu.VMEM((d, d), BF16)],
        compiler_params=_params("arbitrary"),
        name="outproj",
    )(y_ret, y_diff, y_sgu, w_stack, x, g_post.reshape(1, d), g_next.reshape(1, d))


def _ffn_up_kernel(h_ref, wg_ref, wu_ref, a_ref, w_ref):
    th = wg_ref.shape[2]

    @pl.when(pl.program_id(1) == 0)
    def _():
        w_ref[:, 0:th] = wg_ref[0].astype(BF16)
        w_ref[:, th:] = wu_ref[0].astype(BF16)

    gu = _dot(h_ref[...], w_ref[...])
    gate = gu[:, 0:th]
    a_ref[...] = (gate * jax.nn.sigmoid(gate) * gu[:, th:]).astype(a_ref.dtype)


def _ffn_up(h, wg_stack, wu_stack, layer, tm=2048):
    m, d = h.shape
    hidden = wg_stack.shape[2]
    th = hidden // FFN_TILES
    w_tile = pl.BlockSpec((1, d, th), lambda j, i: (layer, 0, j))
    return pl.pallas_call(
        _ffn_up_kernel,
        grid=(FFN_TILES, m // tm),
        in_specs=[pl.BlockSpec((tm, d), lambda j, i: (i, 0)), w_tile, w_tile],
        out_specs=pl.BlockSpec((tm, th), lambda j, i: (i, j)),
        out_shape=jax.ShapeDtypeStruct((m, hidden), BF16),
        scratch_shapes=[pltpu.VMEM((d, 2 * th), BF16)],
        compiler_params=_params("arbitrary", "arbitrary"),
        name="ffn_up",
    )(h, wg_stack, wu_stack)


def _ffn_down_kernel(a_ref, w_ref, x_ref, gpost_ref, gnext_ref, xo_ref, *maybe_ho_ref):
    f = _dot(a_ref[...], w_ref[...])
    xn = x_ref[...] + _rms(f, gpost_ref[...])
    xo_ref[...] = xn
    for ho_ref in maybe_ho_ref:
        ho_ref[...] = _rms(xn, gnext_ref[...]).astype(ho_ref.dtype)


def _ffn_down(act, wd, x, g_post, g_next, emit_h, tm=512):
    m, d = x.shape
    hidden = act.shape[1]
    row = lambda i: (i, 0)
    const = lambda i: (0, 0)
    out = pl.pallas_call(
        _ffn_down_kernel,
        grid=(m // tm,),
        in_specs=[pl.BlockSpec((tm, hidden), row),
                  pl.BlockSpec((hidden, d), const, pipeline_mode=pl.Buffered(1)),
                  pl.BlockSpec((tm, d), row),
                  pl.BlockSpec((1, d), const), pl.BlockSpec((1, d), const)],
        out_specs=[pl.BlockSpec((tm, d), row)] + [pl.BlockSpec((tm, d), row)] * emit_h,
        out_shape=[jax.ShapeDtypeStruct((m, d), F32)] + [jax.ShapeDtypeStruct((m, d), BF16)] * emit_h,
        compiler_params=pltpu.CompilerParams(dimension_semantics=("parallel",), vmem_limit_bytes=VMEM_LIMIT_FFN),
        name="ffn_down",
    )(act, wd, x, g_post.reshape(1, d), g_next.reshape(1, d))
    return (out[0], out[1]) if emit_h else (out[0], None)


def kernel(x, pre_mix_g, w_in, ret_gn_g, diff_lam_q1, diff_lam_k1, diff_lam_q2, diff_lam_k2, diff_subln_g,
           sgu_ln_g, sgu_ln_b, sgu_w, sgu_b, w_out, post_mix_g, pre_ffn_g, w_gate, w_up, w_down, post_ffn_g):
    batch, seq, d = x.shape
    depth = w_in.shape[0]
    xf = x.reshape(batch * seq, d)
    h = _rmsnorm(xf, pre_mix_g[0])
    for l in range(depth):
        lambda_init = 0.8 - 0.6 * math.exp(-0.3 * l)
        proj, wd = _inproj(h, w_in, w_down, l)
        y_ret = _retention(proj, ret_gn_g[l], batch, seq)
        y_diff = _diffattn(proj, diff_lam_q1[l], diff_lam_k1[l], diff_lam_q2[l], diff_lam_k2[l],
                           diff_subln_g[l], lambda_init, batch, seq)
        y_sgu = _sgu(proj, sgu_ln_g[l], sgu_ln_b[l], sgu_w[l], sgu_b[l])
        xf, h = _outproj(y_ret, y_diff, y_sgu, w_out, l, xf, post_mix_g[l], pre_ffn_g[l])
        g_next = pre_mix_g[(l + 1) % depth]
        act = _ffn_up(h, w_gate, w_up, l)
        xf, h = _ffn_down(act, wd, xf, post_ffn_g[l], g_next, emit_h=l + 1 < depth)
    return xf.reshape(batch, seq, d)
```

```python
import functools
import math

import numpy as np
import jax
import jax.numpy as jnp
from jax import lax
from jax.experimental import pallas as pl
from jax.experimental.pallas import tpu as pltpu

D_MODEL = 2048
HEAD_DIM = 128
RET_HEADS = 6
DIFF_HEADS = 6
SGU_GROUPS = 4
DIFF_MAP_DIM = 64
RET_W = RET_HEADS * HEAD_DIM
DIFF_W = DIFF_HEADS * HEAD_DIM
SGU_W = SGU_GROUPS * HEAD_DIM
IN_W = 4 * RET_W + 3 * DIFF_W + 2 * SGU_W
CHUNK = 128
RET_CHUNK = 256
EPS = 1e-6
LOG2E = math.log2(math.e)
NEG_BIG = -1e30
SUM_ROWS = 16
FFN_TILES = 11
QCHUNK = 256

COL_RQ, COL_RK, COL_RV, COL_RG = 0, 6, 12, 18
COL_DQ, COL_DK, COL_DV = 24, 30, 36
COL_SU, COL_SV = 42, 46

V7X_VMEM_BYTES = 64 * 1024 * 1024
VMEM_LIMIT = 56 * 1024 * 1024
VMEM_LIMIT_FFN = 60 * 1024 * 1024
assert VMEM_LIMIT < VMEM_LIMIT_FFN < V7X_VMEM_BYTES

BF16 = jnp.bfloat16
F32 = jnp.float32


def _dot(a, b):
    return jnp.dot(a, b, preferred_element_type=F32)


def _dot_nt(a, b):
    return lax.dot_general(a, b, (((1,), (1,)), ((), ())), preferred_element_type=F32)


def _dot_tn(a, b):
    return lax.dot_general(a, b, (((0,), (0,)), ((), ())), preferred_element_type=F32)


def _rms(x, g):
    return x * lax.rsqrt(jnp.mean(x * x, axis=-1, keepdims=True) + EPS) * g


def _params(*sem):
    return pltpu.CompilerParams(dimension_semantics=sem, vmem_limit_bytes=VMEM_LIMIT)


def _rmsnorm_kernel(x_ref, g_ref, o_ref):
    o_ref[...] = _rms(x_ref[...], g_ref[...]).astype(o_ref.dtype)


def _rmsnorm(x, g, tm=512):
    m, d = x.shape
    return pl.pallas_call(
        _rmsnorm_kernel,
        grid=(m // tm,),
        in_specs=[pl.BlockSpec((tm, d), lambda i: (i, 0)), pl.BlockSpec((1, d), lambda i: (0, 0))],
        out_specs=pl.BlockSpec((tm, d), lambda i: (i, 0)),
        out_shape=jax.ShapeDtypeStruct((m, d), BF16),
        compiler_params=_params("parallel"),
        name="rmsnorm",
    )(x, g.reshape(1, d))


def _inproj_kernel(h_ref, w_ref, wd_ref, o_ref, wdb_ref, wb_ref):
    @pl.when(pl.program_id(1) == 0)
    def _():
        wb_ref[...] = w_ref[0].astype(BF16)

    @pl.when(pl.program_id(0) == 0)
    def _():
        wdb_ref[...] = wd_ref[0].astype(BF16)

    o_ref[...] = _dot(h_ref[...], wb_ref[...]).astype(o_ref.dtype)


def _inproj(h, w_stack, wd_stack, layer, tm=1024, tn=1280):
    m, k = h.shape
    n = w_stack.shape[2]
    nrow = m // tm
    hidden, d = wd_stack.shape[1:]
    slab = hidden // nrow
    assert slab * nrow == hidden and slab % 16 == 0, (hidden, nrow)
    slab_idx = lambda j, i: jnp.where(j == 0, i, nrow - 1)
    return pl.pallas_call(
        _inproj_kernel,
        grid=(n // tn, nrow),
        in_specs=[pl.BlockSpec((tm, k), lambda j, i: (i, 0)),
                  pl.BlockSpec((1, k, tn), lambda j, i: (layer, 0, j)),
                  pl.BlockSpec((1, slab, d), lambda j, i: (layer, slab_idx(j, i), 0))],
        out_specs=[pl.BlockSpec((tm, tn), lambda j, i: (i, j)),
                   pl.BlockSpec((slab, d), lambda j, i: (slab_idx(j, i), 0))],
        out_shape=[jax.ShapeDtypeStruct((m, n), BF16), jax.ShapeDtypeStruct((hidden, d), BF16)],
        scratch_shapes=[pltpu.VMEM((k, tn), BF16)],
        compiler_params=_params("arbitrary", "arbitrary"),
        name="inproj",
    )(h, w_stack, wd_stack)


def _retention_kernel(lg_ref, q_ref, k_ref, v_ref, g_ref, gn_ref, o_ref, r_ref, *, tb):
    pair = pl.program_id(1)
    t = pl.program_id(2)

    @pl.when(t == 0)
    def _():
        r_ref[...] = jnp.zeros_like(r_ref)

    scale = HEAD_DIM ** -0.5
    row = lax.broadcasted_iota(jnp.int32, (RET_CHUNK, RET_CHUNK), 0)
    col = lax.broadcasted_iota(jnp.int32, (RET_CHUNK, RET_CHUNK), 1)
    rel = (row - col).astype(F32)
    pos = lax.broadcasted_iota(jnp.int32, (RET_CHUNK, 1), 0).astype(F32)
    consts = []
    for i in range(2):
        lg = lg_ref[2 * pair + i]
        decay = jnp.where(rel >= 0, jnp.exp(lg * jnp.maximum(rel, 0.0)), 0.0) * scale
        zeta = jnp.exp(lg * (RET_CHUNK - 1.0 - pos)) * scale
        xi = jnp.exp(lg * (pos + 1.0))
        chunk_decay = jnp.exp(jnp.full((1, 1), lg * RET_CHUNK, F32))
        consts.append((decay, zeta, xi, chunk_decay))

    for c in range(tb // RET_CHUNK):
        sl = slice(c * RET_CHUNK, (c + 1) * RET_CHUNK)
        for i in range(2):
            decay, zeta, xi, chunk_decay = consts[i]
            hs = slice(i * HEAD_DIM, (i + 1) * HEAD_DIM)
            q = q_ref[sl, hs]
            k = k_ref[sl, hs]
            v = v_ref[sl, hs]
            scores = _dot_nt(q, k) * decay
            inner = _dot(scores.astype(BF16), v)
            r_prev = r_ref[i]
            cross = _dot(q, r_prev.astype(BF16)) * xi
            kz = (k.astype(F32) * zeta).astype(BF16)
            r_ref[i] = _dot_tn(kz, v) + chunk_decay * r_prev
            y = inner + cross
            mu = jnp.mean(y, axis=-1, keepdims=True)
            yc = y - mu
            var = jnp.mean(yc * yc, axis=-1, keepdims=True)
            y = yc * lax.rsqrt(var + EPS) * gn_ref[0, :, hs]
            gate = g_ref[sl, hs].astype(F32)
            o_ref[sl, hs] = (gate * jax.nn.sigmoid(gate) * y).astype(o_ref.dtype)


def _retention(proj, gn_gain, batch, seq, tb=2048):
    m = proj.shape[0]
    nt = seq // tb
    log_gamma = np.log1p(-(2.0 ** (-5.0 - np.arange(RET_HEADS, dtype=np.float32)))).astype(np.float32)
    wide = 2 * HEAD_DIM

    def blk(col):
        return pl.BlockSpec((tb, wide), lambda b, h, t: (b * nt + t, col // 2 + h))

    return pl.pallas_call(
        functools.partial(_retention_kernel, tb=tb),
        grid=(batch, RET_HEADS // 2, nt),
        in_specs=[pl.BlockSpec(memory_space=pltpu.SMEM),
                  blk(COL_RQ), blk(COL_RK), blk(COL_RV), blk(COL_RG),
                  pl.BlockSpec((1, 1, wide), lambda b, h, t: (h, 0, 0))],
        out_specs=blk(0),
        out_shape=jax.ShapeDtypeStruct((m, RET_W), BF16),
        scratch_shapes=[pltpu.VMEM((2, HEAD_DIM, HEAD_DIM), F32)],
        compiler_params=_params("arbitrary", "arbitrary", "arbitrary"),
        name="retention",
    )(jnp.asarray(log_gamma), proj, proj, proj, proj, gn_gain.reshape(RET_HEADS // 2, 1, wide))


def _diffattn_kernel(slope_ref, q_ref, k_ref, v_ref, lq1_ref, lk1_ref, lq2_ref, lk2_ref, sg_ref, o_ref,
                     k1_ref, k2_ref, vt_ref, *bufs, bq, bk, lambda_init):
    acc = bufs[0:2]
    s_buf = [bufs[2:4], bufs[4:6]]
    p_buf = [bufs[6:8], bufs[8:10]]
    h = pl.program_id(1)
    qi = pl.program_id(2)
    nkb = k1_ref.shape[0]
    per_q = bq // bk

    slope2 = slope_ref[h] * LOG2E

    def block_bias(j):
        return jnp.full((1, 1), j * bk, jnp.int32).astype(F32) * slope2

    @pl.when(qi == 0)
    def _():
        lane = lax.broadcasted_iota(jnp.int32, (bk, HEAD_DIM), 1)
        bias = lax.broadcasted_iota(jnp.int32, (bk, HEAD_DIM), 0).astype(F32) * slope2
        hi = bias.astype(BF16).astype(F32)
        r1 = bias - hi
        mid = r1.astype(BF16).astype(F32)
        lo = r1 - mid
        zero = jnp.zeros_like(bias)
        e1 = jnp.where(lane == 64, hi, jnp.where(lane == 65, mid, jnp.where(lane == 66, lo, zero)))
        e2 = jnp.where(lane == 0, hi, jnp.where(lane == 1, mid, jnp.where(lane == 2, lo, zero)))

        def prep(j, carry):
            start = pl.multiple_of(j * bk, bk)
            kb = k_ref[pl.ds(start, bk), :].astype(F32)
            k1_ref[j] = jnp.where(lane < 64, kb, e1).astype(BF16)
            k2_ref[j] = jnp.where(lane >= 64, kb, e2).astype(BF16)
            vt_ref[j, 0:HEAD_DIM, :] = v_ref[pl.ds(start, bk), :].astype(F32).T.astype(BF16)
            vt_ref[j, HEAD_DIM:, :] = jnp.ones((SUM_ROWS, bk), BF16)
            return carry

        lax.fori_loop(0, nkb, prep, 0)

    lane = lax.broadcasted_iota(jnp.int32, (bq, HEAD_DIM), 1)
    qs = q_ref[...].astype(F32) * (DIFF_MAP_DIM ** -0.5 * LOG2E)
    one = jnp.ones_like(qs)
    zero = jnp.zeros_like(qs)
    qa = (jnp.where(lane < 64, qs, jnp.where(lane < 67, one, zero)).astype(BF16),
          jnp.where(lane >= 64, qs, jnp.where(lane < 3, one, zero)).astype(BF16))
    ka = (k1_ref, k2_ref)
    chunks = [slice(c * QCHUNK, (c + 1) * QCHUNK) for c in range(bq // QCHUNK)]
    krow = lax.broadcasted_iota(jnp.int32, (bk, QCHUNK), 0)
    qcol = lax.broadcasted_iota(jnp.int32, (bk, QCHUNK), 1)

    def issue_scores(mp, j, s_out, diag=None):
        mbs = []
        for cs in chunks:
            first_key = 0 if diag is None else diag * bk
            if first_key >= cs.stop:
                mbs.append(jnp.full((1, QCHUNK), 2 * NEG_BIG, F32))
                continue
            s = _dot_nt(ka[mp][j], qa[mp][cs, :])
            if diag is not None and first_key + bk - 1 > cs.start:
                s = jnp.where(krow + first_key <= qcol + cs.start, s, 2 * NEG_BIG)
            s_out[mp][:, cs] = s
            mbs.append(jnp.max(s, axis=0, keepdims=True))
        return jnp.concatenate(mbs, axis=1) + block_bias(j)

    def live(cs, diag):
        return diag is None or diag * bk < cs.stop

    def stage(slot, carry, cur, nxt, pending, cur_diag=None, nxt_diag=None, pend_diag=None, first=False):
        s_in, p_in = s_buf[slot], p_buf[slot]
        s_out, p_out = s_buf[1 - slot], p_buf[1 - slot]
        ms, alphas, mbs, dep = carry[0:6:3], carry[1:6:3], carry[2:6:3], carry[6]
        mns = [jnp.maximum(ms[mp], mbs[mp]) for mp in range(2)]
        shift = dep - block_bias(cur)
        new_mb = []
        for mp in range(2):
            new_mb.append(issue_scores(mp, nxt, s_out, nxt_diag))
            for cs in chunks:
                if live(cs, cur_diag):
                    p = jnp.exp2(s_in[mp][:, cs] - (mns[mp][:, cs] + shift))
                    p_out[mp][:, cs] = p.astype(BF16)
                else:
                    p_out[mp][:, cs] = jnp.zeros((bk, QCHUNK), BF16)
        if pending is not None:
            for mp in range(2):
                for cs in chunks:
                    if live(cs, pend_diag):
                        pv = _dot(vt_ref[pending], p_in[mp][:, cs])
                        acc[mp][:, cs] = pv if first else alphas[mp][:, cs] * acc[mp][:, cs] + pv
        out = []
        for mp in range(2):
            out += [mns[mp], jnp.exp2(ms[mp] - mns[mp]), new_mb[mp]]
        return tuple(out) + (p[0:1, :] * 0.0,)

    d0 = per_q * qi
    nfull = per_q * qi
    carry = []
    for mp in range(2):
        carry += [jnp.full((1, bq), NEG_BIG, F32), jnp.ones((1, bq), F32), issue_scores(mp, d0, s_buf[0], diag=0)]
    carry = tuple(carry) + (jnp.zeros((1, QCHUNK), F32),)
    for d in range(per_q):
        more = d + 1 < per_q
        carry = stage(d % 2, carry, d0 + d, d0 + d + 1 if more else 0, d0 + d - 1 if d else None,
                      cur_diag=d, nxt_diag=d + 1 if more else None, pend_diag=d - 1 if d else None, first=d == 1)

    def pair(u, carry):
        t = 2 * u
        carry = stage(0, carry, t, t + 1, jnp.where(t == 0, d0 + per_q - 1, t - 1))
        return stage(1, carry, t + 1, jnp.minimum(t + 2, nfull - 1), t)

    carry = lax.fori_loop(0, nfull // 2, pair, carry)
    last = jnp.where(qi == 0, per_q - 1, nfull - 1)
    for mp in range(2):
        for cs in chunks:
            acc[mp][:, cs] = carry[3 * mp + 1][:, cs] * acc[mp][:, cs] + _dot(vt_ref[last], p_buf[0][mp][:, cs])

    lam = (jnp.exp(jnp.sum(lq1_ref[...] * lk1_ref[...], axis=-1, keepdims=True))
           - jnp.exp(jnp.sum(lq2_ref[...] * lk2_ref[...], axis=-1, keepdims=True)) + lambda_init)
    l1 = acc[0][HEAD_DIM:HEAD_DIM + 1, :]
    l2 = acc[1][HEAD_DIM:HEAD_DIM + 1, :]
    o_t = acc[0][0:HEAD_DIM, :] * (1.0 / l1) - acc[1][0:HEAD_DIM, :] * (lam / l2)
    o = o_t.T
    o = _rms(o, sg_ref[0]) * (1.0 - lambda_init)
    o_ref[...] = o.astype(o_ref.dtype)


def _diffattn(proj, lq1, lk1, lq2, lk2, subln_g, lambda_init, batch, seq, bq=2048, bk=512):
    m = proj.shape[0]
    nq, nk = seq // bq, seq // bk
    assert nq * bq == seq and nk * bk == seq and (bq // bk) % 2 == 0 and bq % QCHUNK == 0, (seq, bq, bk)
    slopes = (2.0 ** (-8.0 * np.arange(1, DIFF_HEADS + 1, dtype=np.float32) / DIFF_HEADS)).astype(np.float32)
    vec = pl.BlockSpec((1, DIFF_MAP_DIM), lambda b, h, i: (0, 0))
    return pl.pallas_call(
        functools.partial(_diffattn_kernel, bq=bq, bk=bk, lambda_init=lambda_init),
        grid=(batch, DIFF_HEADS, nq),
        in_specs=[pl.BlockSpec(memory_space=pltpu.SMEM),
                  pl.BlockSpec((bq, HEAD_DIM), lambda b, h, i: (b * nq + i, COL_DQ + h)),
                  pl.BlockSpec((seq, HEAD_DIM), lambda b, h, i: (b, COL_DK + h)),
                  pl.BlockSpec((seq, HEAD_DIM), lambda b, h, i: (b, COL_DV + h)),
                  vec, vec, vec, vec,
                  pl.BlockSpec((1, 1, HEAD_DIM), lambda b, h, i: (h, 0, 0))],
        out_specs=pl.BlockSpec((bq, HEAD_DIM), lambda b, h, i: (b * nq + i, h)),
        out_shape=jax.ShapeDtypeStruct((m, DIFF_W), BF16),
        scratch_shapes=[pltpu.VMEM((nk, bk, HEAD_DIM), BF16),
                        pltpu.VMEM((nk, bk, HEAD_DIM), BF16),
                        pltpu.VMEM((nk, HEAD_DIM + SUM_ROWS, bk), BF16),
                        *[pltpu.VMEM((HEAD_DIM + SUM_ROWS, bq), F32)] * 2,
                        *[pltpu.VMEM((bk, bq), F32)] * 4,
                        *[pltpu.VMEM((bk, bq), BF16)] * 4],
        compiler_params=_params("arbitrary", "arbitrary", "arbitrary"),
        name="diffattn",
    )(jnp.asarray(slopes), proj, proj, proj,
      lq1.reshape(1, -1), lk1.reshape(1, -1), lq2.reshape(1, -1), lk2.reshape(1, -1),
      subln_g.reshape(DIFF_HEADS, 1, HEAD_DIM))


def _sgu_kernel(u_ref, v_ref, lng_ref, lnb_ref, w_ref, b_ref, o_ref, *, tb):
    row = lax.broadcasted_iota(jnp.int32, (CHUNK, CHUNK), 0)
    col = lax.broadcasted_iota(jnp.int32, (CHUNK, CHUNK), 1)
    for i in range(2):
        gs = slice(i * HEAD_DIM, (i + 1) * HEAD_DIM)
        w = jnp.where(row >= col, w_ref[i], 0.0).astype(BF16)
        bias = b_ref[i]
        lng = lng_ref[0, :, gs]
        lnb = lnb_ref[0, :, gs]
        for c in range(tb // CHUNK):
            sl = slice(c * CHUNK, (c + 1) * CHUNK)
            u = jax.nn.gelu(u_ref[sl, gs].astype(F32))
            v = jax.nn.gelu(v_ref[sl, gs].astype(F32))
            mu = jnp.mean(v, axis=-1, keepdims=True)
            vc = v - mu
            var = jnp.mean(vc * vc, axis=-1, keepdims=True)
            vn = vc * lax.rsqrt(var + EPS) * lng + lnb
            mixed = _dot(w, vn.astype(BF16)) + bias
            o_ref[sl, gs] = (u * mixed).astype(o_ref.dtype)


def _sgu(proj, ln_g, ln_b, w_s, b_s, tb=2048):
    m = proj.shape[0]
    pairs = SGU_GROUPS // 2
    wide = 2 * HEAD_DIM
    per_pair = lambda i, j: (j, 0, 0)
    return pl.pallas_call(
        functools.partial(_sgu_kernel, tb=tb),
        grid=(m // tb, pairs),
        in_specs=[pl.BlockSpec((tb, wide), lambda i, j: (i, COL_SU // 2 + j)),
                  pl.BlockSpec((tb, wide), lambda i, j: (i, COL_SV // 2 + j)),
                  pl.BlockSpec((1, 1, wide), per_pair),
                  pl.BlockSpec((1, 1, wide), per_pair),
                  pl.BlockSpec((2, CHUNK, CHUNK), per_pair),
                  pl.BlockSpec((2, CHUNK, 1), per_pair)],
        out_specs=pl.BlockSpec((tb, wide), lambda i, j: (i, j)),
        out_shape=jax.ShapeDtypeStruct((m, SGU_W), BF16),
        compiler_params=_params("parallel", "arbitrary"),
        name="sgu",
    )(proj, proj, ln_g.reshape(pairs, 1, wide), ln_b.reshape(pairs, 1, wide), w_s,
      b_s.reshape(SGU_GROUPS, CHUNK, 1))


def _outproj_kernel(yr_ref, yd_ref, ys_ref, wf_ref, x_ref, gpost_ref, gnext_ref, xo_ref, ho_ref, w_ref):
    @pl.when(pl.program_id(0) == 0)
    def _():
        w_ref[...] = wf_ref[0].astype(BF16)

    mix = (_dot(yr_ref[...], w_ref[0:RET_W, :])
           + _dot(yd_ref[...], w_ref[RET_W:RET_W + DIFF_W, :])
           + _dot(ys_ref[...], w_ref[RET_W + DIFF_W:D_MODEL, :]))
    xn = x_ref[...] + _rms(mix, gpost_ref[...])
    xo_ref[...] = xn
    ho_ref[...] = _rms(xn, gnext_ref[...]).astype(ho_ref.dtype)


def _outproj(y_ret, y_diff, y_sgu, w_stack, layer, x, g_post, g_next, tm=512):
    m, d = x.shape
    row = lambda i: (i, 0)
    const = lambda i: (0, 0)
    return pl.pallas_call(
        _outproj_kernel,
        grid=(m // tm,),
        in_specs=[pl.BlockSpec((tm, RET_W), row), pl.BlockSpec((tm, DIFF_W), row), pl.BlockSpec((tm, SGU_W), row),
                  pl.BlockSpec((1, d, d), lambda i: (layer, 0, 0), pipeline_mode=pl.Buffered(1)),
                  pl.BlockSpec((tm, d), row),
                  pl.BlockSpec((1, d), const), pl.BlockSpec((1, d), const)],
        out_specs=[pl.BlockSpec((tm, d), row), pl.BlockSpec((tm, d), row)],
        out_shape=[jax.ShapeDtypeStruct((m, d), F32), jax.ShapeDtypeStruct((m, d), BF16)],
        scratch_shapes=[pltpu.VMEM((d, d), BF16)],
        compiler_params=_params("arbitrary"),
        name="outproj",
    )(y_ret, y_diff, y_sgu, w_stack, x, g_post.reshape(1, d), g_next.reshape(1, d))


def _ffn_up_kernel(h_ref, wg_ref, wu_ref, a_ref, w_ref):
    th = wg_ref.shape[2]

    @pl.when(pl.program_id(1) == 0)
    def _():
        w_ref[:, 0:th] = wg_ref[0].astype(BF16)
        w_ref[:, th:] = wu_ref[0].astype(BF16)

    gu = _dot(h_ref[...], w_ref[...])
    gate = gu[:, 0:th]
    a_ref[...] = (gate * jax.nn.sigmoid(gate) * gu[:, th:]).astype(a_ref.dtype)


def _ffn_up(h, wg_stack, wu_stack, layer, tm=2048):
    m, d = h.shape
    hidden = wg_stack.shape[2]
    th = hidden // FFN_TILES
    w_tile = pl.BlockSpec((1, d, th), lambda j, i: (layer, 0, j))
    return pl.pallas_call(
        _ffn_up_kernel,
        grid=(FFN_TILES, m // tm),
        in_specs=[pl.BlockSpec((tm, d), lambda j, i: (i, 0)), w_tile, w_tile],
        out_specs=pl.BlockSpec((tm, th), lambda j, i: (i, j)),
        out_shape=jax.ShapeDtypeStruct((m, hidden), BF16),
        scratch_shapes=[pltpu.VMEM((d, 2 * th), BF16)],
        compiler_params=_params("arbitrary", "arbitrary"),
        name="ffn_up",
    )(h, wg_stack, wu_stack)


def _ffn_down_kernel(a_ref, w_ref, x_ref, gpost_ref, gnext_ref, xo_ref, *maybe_ho_ref):
    f = _dot(a_ref[...], w_ref[...])
    xn = x_ref[...] + _rms(f, gpost_ref[...])
    xo_ref[...] = xn
    for ho_ref in maybe_ho_ref:
        ho_ref[...] = _rms(xn, gnext_ref[...]).astype(ho_ref.dtype)


def _ffn_down(act, wd, x, g_post, g_next, emit_h, tm=512):
    m, d = x.shape
    hidden = act.shape[1]
    row = lambda i: (i, 0)
    const = lambda i: (0, 0)
    out = pl.pallas_call(
        _ffn_down_kernel,
        grid=(m // tm,),
        in_specs=[pl.BlockSpec((tm, hidden), row),
                  pl.BlockSpec((hidden, d), const, pipeline_mode=pl.Buffered(1)),
                  pl.BlockSpec((tm, d), row),
                  pl.BlockSpec((1, d), const), pl.BlockSpec((1, d), const)],
        out_specs=[pl.BlockSpec((tm, d), row)] + [pl.BlockSpec((tm, d), row)] * emit_h,
        out_shape=[jax.ShapeDtypeStruct((m, d), F32)] + [jax.ShapeDtypeStruct((m, d), BF16)] * emit_h,
        compiler_params=pltpu.CompilerParams(dimension_semantics=("parallel",), vmem_limit_bytes=VMEM_LIMIT_FFN),
        name="ffn_down",
    )(act, wd, x, g_post.reshape(1, d), g_next.reshape(1, d))
    return (out[0], out[1]) if emit_h else (out[0], None)


def kernel(x, pre_mix_g, w_in, ret_gn_g, diff_lam_q1, diff_lam_k1, diff_lam_q2, diff_lam_k2, diff_subln_g,
           sgu_ln_g, sgu_ln_b, sgu_w, sgu_b, w_out, post_mix_g, pre_ffn_g, w_gate, w_up, w_down, post_ffn_g):
    batch, seq, d = x.shape
    depth = w_in.shape[0]
    xf = x.reshape(batch * seq, d)
    h = _rmsnorm(xf, pre_mix_g[0])
    for l in range(depth):
        lambda_init = 0.8 - 0.6 * math.exp(-0.3 * l)
        proj, wd = _inproj(h, w_in, w_down, l)
        y_ret = _retention(proj, ret_gn_g[l], batch, seq)
        y_diff = _diffattn(proj, diff_lam_q1[l], diff_lam_k1[l], diff_lam_q2[l], diff_lam_k2[l],
                           diff_subln_g[l], lambda_init, batch, seq)
        y_sgu = _sgu(proj, sgu_ln_g[l], sgu_ln_b[l], sgu_w[l], sgu_b[l])
        xf, h = _outproj(y_ret, y_diff, y_sgu, w_out, l, xf, post_mix_g[l], pre_ffn_g[l])
        g_next = pre_mix_g[(l + 1) % depth]
        act = _ffn_up(h, w_gate, w_up, l)
        xf, h = _ffn_down(act, wd, xf, post_ffn_g[l], g_next, emit_h=l + 1 < depth)
    return xf.reshape(batch, seq, d)
```

```python
import functools
import math

import numpy as np
import jax
import jax.numpy as jnp
from jax import lax
from jax.experimental import pallas as pl
from jax.experimental.pallas import tpu as pltpu

D_MODEL = 2048
HEAD_DIM = 128
RET_HEADS = 6
DIFF_HEADS = 6
SGU_GROUPS = 4
DIFF_MAP_DIM = 64
RET_W = RET_HEADS * HEAD_DIM
DIFF_W = DIFF_HEADS * HEAD_DIM
SGU_W = SGU_GROUPS * HEAD_DIM
IN_W = 4 * RET_W + 3 * DIFF_W + 2 * SGU_W
CHUNK = 128
RET_CHUNK = 256
EPS = 1e-6
LOG2E = math.log2(math.e)
NEG_BIG = -1e30
SUM_ROWS = 16
FFN_TILES = 11
QCHUNK = 256

COL_RQ, COL_RK, COL_RV, COL_RG = 0, 6, 12, 18
COL_DQ, COL_DK, COL_DV = 24, 30, 36
COL_SU, COL_SV = 42, 46

V7X_VMEM_BYTES = 64 * 1024 * 1024
VMEM_LIMIT = 56 * 1024 * 1024
VMEM_LIMIT_FFN = 60 * 1024 * 1024
assert VMEM_LIMIT < VMEM_LIMIT_FFN < V7X_VMEM_BYTES

BF16 = jnp.bfloat16
F32 = jnp.float32


def _dot(a, b):
    return jnp.dot(a, b, preferred_element_type=F32)


def _dot_nt(a, b):
    return lax.dot_general(a, b, (((1,), (1,)), ((), ())), preferred_element_type=F32)


def _dot_tn(a, b):
    return lax.dot_general(a, b, (((0,), (0,)), ((), ())), preferred_element_type=F32)


def _rms(x, g):
    return x * lax.rsqrt(jnp.mean(x * x, axis=-1, keepdims=True) + EPS) * g


def _params(*sem):
    return pltpu.CompilerParams(dimension_semantics=sem, vmem_limit_bytes=VMEM_LIMIT)


def _rmsnorm_kernel(x_ref, g_ref, o_ref):
    o_ref[...] = _rms(x_ref[...], g_ref[...]).astype(o_ref.dtype)


def _rmsnorm(x, g, tm=512):
    m, d = x.shape
    return pl.pallas_call(
        _rmsnorm_kernel,
        grid=(m // tm,),
        in_specs=[pl.BlockSpec((tm, d), lambda i: (i, 0)), pl.BlockSpec((1, d), lambda i: (0, 0))],
        out_specs=pl.BlockSpec((tm, d), lambda i: (i, 0)),
        out_shape=jax.ShapeDtypeStruct((m, d), BF16),
        compiler_params=_params("parallel"),
        name="rmsnorm",
    )(x, g.reshape(1, d))


def _inproj_kernel(h_ref, w_ref, wd_ref, o_ref, wdb_ref, wb_ref):
    @pl.when(pl.program_id(1) == 0)
    def _():
        wb_ref[...] = w_ref[0].astype(BF16)

    @pl.when(pl.program_id(0) == 0)
    def _():
        wdb_ref[...] = wd_ref[0].astype(BF16)

    res = _dot(h_ref[...], wb_ref[...])
    for slot in range(o_ref.shape[0]):
        o_ref[slot] = res[:, slot * HEAD_DIM:(slot + 1) * HEAD_DIM].astype(o_ref.dtype)


def _inproj(h, w_stack, wd_stack, layer, tm=1024, tn=1280):
    m, k = h.shape
    n = w_stack.shape[2]
    nrow = m // tm
    hidden, d = wd_stack.shape[1:]
    slab = hidden // nrow
    assert slab * nrow == hidden and slab % 16 == 0, (hidden, nrow)
    slab_idx = lambda j, i: jnp.where(j == 0, i, nrow - 1)
    return pl.pallas_call(
        _inproj_kernel,
        grid=(n // tn, nrow),
        in_specs=[pl.BlockSpec((tm, k), lambda j, i: (i, 0)),
                  pl.BlockSpec((1, k, tn), lambda j, i: (layer, 0, j)),
                  pl.BlockSpec((1, slab, d), lambda j, i: (layer, slab_idx(j, i), 0))],
        out_specs=[pl.BlockSpec((tn // HEAD_DIM, tm, HEAD_DIM), lambda j, i: (j, i, 0)),
                   pl.BlockSpec((slab, d), lambda j, i: (slab_idx(j, i), 0))],
        out_shape=[jax.ShapeDtypeStruct((n // HEAD_DIM, m, HEAD_DIM), BF16),
                   jax.ShapeDtypeStruct((hidden, d), BF16)],
        scratch_shapes=[pltpu.VMEM((k, tn), BF16)],
        compiler_params=_params("arbitrary", "arbitrary"),
        name="inproj",
    )(h, w_stack, wd_stack)


def _retention_kernel(lg_ref, q_ref, k_ref, v_ref, g_ref, gn_ref, o_ref, r_ref, *, tb):
    pair = pl.program_id(1)
    t = pl.program_id(2)

    @pl.when(t == 0)
    def _():
        r_ref[...] = jnp.zeros_like(r_ref)

    scale = HEAD_DIM ** -0.5
    row = lax.broadcasted_iota(jnp.int32, (RET_CHUNK, RET_CHUNK), 0)
    col = lax.broadcasted_iota(jnp.int32, (RET_CHUNK, RET_CHUNK), 1)
    rel = (row - col).astype(F32)
    pos = lax.broadcasted_iota(jnp.int32, (RET_CHUNK, 1), 0).astype(F32)
    consts = []
    for i in range(2):
        lg = lg_ref[2 * pair + i]
        decay = jnp.where(rel >= 0, jnp.exp(lg * jnp.maximum(rel, 0.0)), 0.0) * scale
        zeta = jnp.exp(lg * (RET_CHUNK - 1.0 - pos)) * scale
        xi = jnp.exp(lg * (pos + 1.0))
        chunk_decay = jnp.exp(jnp.full((1, 1), lg * RET_CHUNK, F32))
        consts.append((decay, zeta, xi, chunk_decay))

    for c in range(tb // RET_CHUNK):
        sl = slice(c * RET_CHUNK, (c + 1) * RET_CHUNK)
        for i in range(2):
            decay, zeta, xi, chunk_decay = consts[i]
            hs = slice(i * HEAD_DIM, (i + 1) * HEAD_DIM)
            q = q_ref[i, sl, :]
            k = k_ref[i, sl, :]
            v = v_ref[i, sl, :]
            scores = _dot_nt(q, k) * decay
            inner = _dot(scores.astype(BF16), v)
            r_prev = r_ref[i]
            cross = _dot(q, r_prev.astype(BF16)) * xi
            kz = (k.astype(F32) * zeta).astype(BF16)
            r_ref[i] = _dot_tn(kz, v) + chunk_decay * r_prev
            y = inner + cross
            mu = jnp.mean(y, axis=-1, keepdims=True)
            yc = y - mu
            var = jnp.mean(yc * yc, axis=-1, keepdims=True)
            y = yc * lax.rsqrt(var + EPS) * gn_ref[0, :, hs]
            gate = g_ref[i, sl, :].astype(F32)
            o_ref[sl, hs] = (gate * jax.nn.sigmoid(gate) * y).astype(o_ref.dtype)


def _retention(proj, gn_gain, batch, seq, tb=2048):
    m = proj.shape[1]
    nt = seq // tb
    log_gamma = np.log1p(-(2.0 ** (-5.0 - np.arange(RET_HEADS, dtype=np.float32)))).astype(np.float32)
    wide = 2 * HEAD_DIM

    def blk(col):
        return pl.BlockSpec((2, tb, HEAD_DIM), lambda b, h, t: (col // 2 + h, b * nt + t, 0))

    return pl.pallas_call(
        functools.partial(_retention_kernel, tb=tb),
        grid=(batch, RET_HEADS // 2, nt),
        in_specs=[pl.BlockSpec(memory_space=pltpu.SMEM),
                  blk(COL_RQ), blk(COL_RK), blk(COL_RV), blk(COL_RG),
                  pl.BlockSpec((1, 1, wide), lambda b, h, t: (h, 0, 0))],
        out_specs=pl.BlockSpec((tb, wide), lambda b, h, t: (b * nt + t, h)),
        out_shape=jax.ShapeDtypeStruct((m, RET_W), BF16),
        scratch_shapes=[pltpu.VMEM((2, HEAD_DIM, HEAD_DIM), F32)],
        compiler_params=_params("arbitrary", "arbitrary", "arbitrary"),
        name="retention",
    )(jnp.asarray(log_gamma), proj, proj, proj, proj, gn_gain.reshape(RET_HEADS // 2, 1, wide))


def _diffattn_kernel(slope_ref, q_ref, k_ref, v_ref, lq1_ref, lk1_ref, lq2_ref, lk2_ref, sg_ref, o_ref,
                     k1_ref, k2_ref, vt_ref, *bufs, bq, bk, lambda_init):
    acc = bufs[0:2]
    s_buf = [bufs[2:4], bufs[4:6]]
    p_buf = [bufs[6:8], bufs[8:10]]
    h = pl.program_id(1)
    qi = pl.program_id(2)
    nkb = k1_ref.shape[0]
    per_q = bq // bk

    slope2 = slope_ref[h] * LOG2E

    def block_bias(j):
        return jnp.full((1, 1), j * bk, jnp.int32).astype(F32) * slope2

    @pl.when(qi == 0)
    def _():
        lane = lax.broadcasted_iota(jnp.int32, (bk, HEAD_DIM), 1)
        bias = lax.broadcasted_iota(jnp.int32, (bk, HEAD_DIM), 0).astype(F32) * slope2
        hi = bias.astype(BF16).astype(F32)
        r1 = bias - hi
        mid = r1.astype(BF16).astype(F32)
        lo = r1 - mid
        zero = jnp.zeros_like(bias)
        e1 = jnp.where(lane == 64, hi, jnp.where(lane == 65, mid, jnp.where(lane == 66, lo, zero)))
        e2 = jnp.where(lane == 0, hi, jnp.where(lane == 1, mid, jnp.where(lane == 2, lo, zero)))

        def prep(j, carry):
            start = pl.multiple_of(j * bk, bk)
            kb = k_ref[0, pl.ds(start, bk), :].astype(F32)
            k1_ref[j] = jnp.where(lane < 64, kb, e1).astype(BF16)
            k2_ref[j] = jnp.where(lane >= 64, kb, e2).astype(BF16)
            vt_ref[j, 0:HEAD_DIM, :] = v_ref[0, pl.ds(start, bk), :].astype(F32).T.astype(BF16)
            vt_ref[j, HEAD_DIM:, :] = jnp.ones((SUM_ROWS, bk), BF16)
            return carry

        lax.fori_loop(0, nkb, prep, 0)

    lane = lax.broadcasted_iota(jnp.int32, (bq, HEAD_DIM), 1)
    qs = q_ref[0].astype(F32) * (DIFF_MAP_DIM ** -0.5 * LOG2E)
    one = jnp.ones_like(qs)
    zero = jnp.zeros_like(qs)
    qa = (jnp.where(lane < 64, qs, jnp.where(lane < 67, one, zero)).astype(BF16),
          jnp.where(lane >= 64, qs, jnp.where(lane < 3, one, zero)).astype(BF16))
    ka = (k1_ref, k2_ref)
    chunks = [slice(c * QCHUNK, (c + 1) * QCHUNK) for c in range(bq // QCHUNK)]
    krow = lax.broadcasted_iota(jnp.int32, (bk, QCHUNK), 0)
    qcol = lax.broadcasted_iota(jnp.int32, (bk, QCHUNK), 1)

    def issue_scores(mp, j, s_out, diag=None):
        mbs = []
        for cs in chunks:
            first_key = 0 if diag is None else diag * bk
            if first_key >= cs.stop:
                mbs.append(jnp.full((1, QCHUNK), 2 * NEG_BIG, F32))
                continue
            s = _dot_nt(ka[mp][j], qa[mp][cs, :])
            if diag is not None and first_key + bk - 1 > cs.start:
                s = jnp.where(krow + first_key <= qcol + cs.start, s, 2 * NEG_BIG)
            s_out[mp][:, cs] = s
            mbs.append(jnp.max(s, axis=0, keepdims=True))
        return jnp.concatenate(mbs, axis=1) + block_bias(j)

    def live(cs, diag):
        return diag is None or diag * bk < cs.stop

    def stage(slot, carry, cur, nxt, pending, cur_diag=None, nxt_diag=None, pend_diag=None, first=False):
        s_in, p_in = s_buf[slot], p_buf[slot]
        s_out, p_out = s_buf[1 - slot], p_buf[1 - slot]
        ms, alphas, mbs, dep = carry[0:6:3], carry[1:6:3], carry[2:6:3], carry[6]
        mns = [jnp.maximum(ms[mp], mbs[mp]) for mp in range(2)]
        shift = dep - block_bias(cur)
        new_mb = []
        for mp in range(2):
            new_mb.append(issue_scores(mp, nxt, s_out, nxt_diag))
            for cs in chunks:
                if live(cs, cur_diag):
                    p = jnp.exp2(s_in[mp][:, cs] - (mns[mp][:, cs] + shift))
                    p_out[mp][:, cs] = p.astype(BF16)
                else:
                    p_out[mp][:, cs] = jnp.zeros((bk, QCHUNK), BF16)
        if pending is not None:
            for mp in range(2):
                for cs in chunks:
                    if live(cs, pend_diag):
                        pv = _dot(vt_ref[pending], p_in[mp][:, cs])
                        acc[mp][:, cs] = pv if first else alphas[mp][:, cs] * acc[mp][:, cs] + pv
        out = []
        for mp in range(2):
            out += [mns[mp], jnp.exp2(ms[mp] - mns[mp]), new_mb[mp]]
        return tuple(out) + (p[0:1, :] * 0.0,)

    d0 = per_q * qi
    nfull = per_q * qi
    carry = []
    for mp in range(2):
        carry += [jnp.full((1, bq), NEG_BIG, F32), jnp.ones((1, bq), F32), issue_scores(mp, d0, s_buf[0], diag=0)]
    carry = tuple(carry) + (jnp.zeros((1, QCHUNK), F32),)
    for d in range(per_q):
        more = d + 1 < per_q
        carry = stage(d % 2, carry, d0 + d, d0 + d + 1 if more else 0, d0 + d - 1 if d else None,
                      cur_diag=d, nxt_diag=d + 1 if more else None, pend_diag=d - 1 if d else None, first=d == 1)

    def pair(u, carry):
        t = 2 * u
        carry = stage(0, carry, t, t + 1, jnp.where(t == 0, d0 + per_q - 1, t - 1))
        return stage(1, carry, t + 1, jnp.minimum(t + 2, nfull - 1), t)

    carry = lax.fori_loop(0, nfull // 2, pair, carry)
    last = jnp.where(qi == 0, per_q - 1, nfull - 1)
    for mp in range(2):
        for cs in chunks:
            acc[mp][:, cs] = carry[3 * mp + 1][:, cs] * acc[mp][:, cs] + _dot(vt_ref[last], p_buf[0][mp][:, cs])

    lam = (jnp.exp(jnp.sum(lq1_ref[...] * lk1_ref[...], axis=-1, keepdims=True))
           - jnp.exp(jnp.sum(lq2_ref[...] * lk2_ref[...], axis=-1, keepdims=True)) + lambda_init)
    l1 = acc[0][HEAD_DIM:HEAD_DIM + 1, :]
    l2 = acc[1][HEAD_DIM:HEAD_DIM + 1, :]
    o_t = acc[0][0:HEAD_DIM, :] * (1.0 / l1) - acc[1][0:HEAD_DIM, :] * (lam / l2)
    o = o_t.T
    o = _rms(o, sg_ref[0]) * (1.0 - lambda_init)
    o_ref[...] = o.astype(o_ref.dtype)


def _diffattn(proj, lq1, lk1, lq2, lk2, subln_g, lambda_init, batch, seq, bq=2048, bk=512):
    m = proj.shape[1]
    nq, nk = seq // bq, seq // bk
    assert nq * bq == seq and nk * bk == seq and (bq // bk) % 2 == 0 and bq % QCHUNK == 0, (seq, bq, bk)
    slopes = (2.0 ** (-8.0 * np.arange(1, DIFF_HEADS + 1, dtype=np.float32) / DIFF_HEADS)).astype(np.float32)
    vec = pl.BlockSpec((1, DIFF_MAP_DIM), lambda b, h, i: (0, 0))
    return pl.pallas_call(
        functools.partial(_diffattn_kernel, bq=bq, bk=bk, lambda_init=lambda_init),
        grid=(batch, DIFF_HEADS, nq),
        in_specs=[pl.BlockSpec(memory_space=pltpu.SMEM),
                  pl.BlockSpec((1, bq, HEAD_DIM), lambda b, h, i: (COL_DQ + h, b * nq + i, 0)),
                  pl.BlockSpec((1, seq, HEAD_DIM), lambda b, h, i: (COL_DK + h, b, 0)),
                  pl.BlockSpec((1, seq, HEAD_DIM), lambda b, h, i: (COL_DV + h, b, 0)),
                  vec, vec, vec, vec,
                  pl.BlockSpec((1, 1, HEAD_DIM), lambda b, h, i: (h, 0, 0))],
        out_specs=pl.BlockSpec((bq, HEAD_DIM), lambda b, h, i: (b * nq + i, h)),
        out_shape=jax.ShapeDtypeStruct((m, DIFF_W), BF16),
        scratch_shapes=[pltpu.VMEM((nk, bk, HEAD_DIM), BF16),
                        pltpu.VMEM((nk, bk, HEAD_DIM), BF16),
                        pltpu.VMEM((nk, HEAD_DIM + SUM_ROWS, bk), BF16),
                        *[pltpu.VMEM((HEAD_DIM + SUM_ROWS, bq), F32)] * 2,
                        *[pltpu.VMEM((bk, bq), F32)] * 4,
                        *[pltpu.VMEM((bk, bq), BF16)] * 4],
        compiler_params=_params("arbitrary", "arbitrary", "arbitrary"),
        name="diffattn",
    )(jnp.asarray(slopes), proj, proj, proj,
      lq1.reshape(1, -1), lk1.reshape(1, -1), lq2.reshape(1, -1), lk2.reshape(1, -1),
      subln_g.reshape(DIFF_HEADS, 1, HEAD_DIM))


def _sgu_kernel(u_ref, v_ref, lng_ref, lnb_ref, w_ref, b_ref, o_ref, *, tb):
    row = lax.broadcasted_iota(jnp.int32, (CHUNK, CHUNK), 0)
    col = lax.broadcasted_iota(jnp.int32, (CHUNK, CHUNK), 1)
    for i in range(2):
        gs = slice(i * HEAD_DIM, (i + 1) * HEAD_DIM)
        w = jnp.where(row >= col, w_ref[i], 0.0).astype(BF16)
        bias = b_ref[i]
        lng = lng_ref[0, :, gs]
        lnb = lnb_ref[0, :, gs]
        for c in range(tb // CHUNK):
            sl = slice(c * CHUNK, (c + 1) * CHUNK)
            u = jax.nn.gelu(u_ref[i, sl, :].astype(F32))
            v = jax.nn.gelu(v_ref[i, sl, :].astype(F32))
            mu = jnp.mean(v, axis=-1, keepdims=True)
            vc = v - mu
            var = jnp.mean(vc * vc, axis=-1, keepdims=True)
            vn = vc * lax.rsqrt(var + EPS) * lng + lnb
            mixed = _dot(w, vn.astype(BF16)) + bias
            o_ref[sl, gs] = (u * mixed).astype(o_ref.dtype)


def _sgu(proj, ln_g, ln_b, w_s, b_s, tb=2048):
    m = proj.shape[1]
    pairs = SGU_GROUPS // 2
    wide = 2 * HEAD_DIM
    per_pair = lambda i, j: (j, 0, 0)
    return pl.pallas_call(
        functools.partial(_sgu_kernel, tb=tb),
        grid=(m // tb, pairs),
        in_specs=[pl.BlockSpec((2, tb, HEAD_DIM), lambda i, j: (COL_SU // 2 + j, i, 0)),
                  pl.BlockSpec((2, tb, HEAD_DIM), lambda i, j: (COL_SV // 2 + j, i, 0)),
                  pl.BlockSpec((1, 1, wide), per_pair),
                  pl.BlockSpec((1, 1, wide), per_pair),
                  pl.BlockSpec((2, CHUNK, CHUNK), per_pair),
                  pl.BlockSpec((2, CHUNK, 1), per_pair)],
        out_specs=pl.BlockSpec((tb, wide), lambda i, j: (i, j)),
        out_shape=jax.ShapeDtypeStruct((m, SGU_W), BF16),
        compiler_params=_params("parallel", "arbitrary"),
        name="sgu",
    )(proj, proj, ln_g.reshape(pairs, 1, wide), ln_b.reshape(pairs, 1, wide), w_s,
      b_s.reshape(SGU_GROUPS, CHUNK, 1))


def _outproj_kernel(yr_ref, yd_ref, ys_ref, wf_ref, x_ref, gpost_ref, gnext_ref, xo_ref, ho_ref, w_ref):
    @pl.when(pl.program_id(0) == 0)
    def _():
        w_ref[...] = wf_ref[0].astype(BF16)

    mix = (_dot(yr_ref[...], w_ref[0:RET_W, :])
           + _dot(yd_ref[...], w_ref[RET_W:RET_W + DIFF_W, :])
           + _dot(ys_ref[...], w_ref[RET_W + DIFF_W:D_MODEL, :]))
    xn = x_ref[...] + _rms(mix, gpost_ref[...])
    xo_ref[...] = xn
    ho_ref[...] = _rms(xn, gnext_ref[...]).astype(ho_ref.dtype)


def _outproj(y_ret, y_diff, y_sgu, w_stack, layer, x, g_post, g_next, tm=512):
    m, d = x.shape
    row = lambda i: (i, 0)
    const = lambda i: (0, 0)
    return pl.pallas_call(
        _outproj_kernel,
        grid=(m // tm,),
        in_specs=[pl.BlockSpec((tm, RET_W), row), pl.BlockSpec((tm, DIFF_W), row), pl.BlockSpec((tm, SGU_W), row),
                  pl.BlockSpec((1, d, d), lambda i: (layer, 0, 0), pipeline_mode=pl.Buffered(1)),
                  pl.BlockSpec((tm, d), row),
                  pl.BlockSpec((1, d), const), pl.BlockSpec((1, d), const)],
        out_specs=[pl.BlockSpec((tm, d), row), pl.BlockSpec((tm, d), row)],
        out_shape=[jax.ShapeDtypeStruct((m, d), F32), jax.ShapeDtypeStruct((m, d), BF16)],
        scratch_shapes=[pltpu.VMEM((d, d), BF16)],
        compiler_params=_params("arbitrary"),
        name="outproj",
    )(y_ret, y_diff, y_sgu, w_stack, x, g_post.reshape(1, d), g_next.reshape(1, d))


def _ffn_up_kernel(h_ref, wg_ref, wu_ref, a_ref, w_ref):
    th = wg_ref.shape[2]

    @pl.when(pl.program_id(1) == 0)
    def _():
        w_ref[:, 0:th] = wg_ref[0].astype(BF16)
        w_ref[:, th:] = wu_ref[0].astype(BF16)

    gu = _dot(h_ref[...], w_ref[...])
    gate = gu[:, 0:th]
    a_ref[...] = (gate * jax.nn.sigmoid(gate) * gu[:, th:]).astype(a_ref.dtype)


def _ffn_up(h, wg_stack, wu_stack, layer, tm=2048):
    m, d = h.shape
    hidden = wg_stack.shape[2]
    th = hidden // FFN_TILES
    w_tile = pl.BlockSpec((1, d, th), lambda j, i: (layer, 0, j))
    return pl.pallas_call(
        _ffn_up_kernel,
        grid=(FFN_TILES, m // tm),
        in_specs=[pl.BlockSpec((tm, d), lambda j, i: (i, 0)), w_tile, w_tile],
        out_specs=pl.BlockSpec((tm, th), lambda j, i: (i, j)),
        out_shape=jax.ShapeDtypeStruct((m, hidden), BF16),
        scratch_shapes=[pltpu.VMEM((d, 2 * th), BF16)],
        compiler_params=_params("arbitrary", "arbitrary"),
        name="ffn_up",
    )(h, wg_stack, wu_stack)


def _ffn_down_kernel(a_ref, w_ref, x_ref, gpost_ref, gnext_ref, xo_ref, *maybe_ho_ref):
    f = _dot(a_ref[...], w_ref[...])
    xn = x_ref[...] + _rms(f, gpost_ref[...])
    xo_ref[...] = xn
    for ho_ref in maybe_ho_ref:
        ho_ref[...] = _rms(xn, gnext_ref[...]).astype(ho_ref.dtype)


def _ffn_down(act, wd, x, g_post, g_next, emit_h, tm=512):
    m, d = x.shape
    hidden = act.shape[1]
    row = lambda i: (i, 0)
    const = lambda i: (0, 0)
    out = pl.pallas_call(
        _ffn_down_kernel,
        grid=(m // tm,),
        in_specs=[pl.BlockSpec((tm, hidden), row),
                  pl.BlockSpec((hidden, d), const, pipeline_mode=pl.Buffered(1)),
                  pl.BlockSpec((tm, d), row),
                  pl.BlockSpec((1, d), const), pl.BlockSpec((1, d), const)],
        out_specs=[pl.BlockSpec((tm, d), row)] + [pl.BlockSpec((tm, d), row)] * emit_h,
        out_shape=[jax.ShapeDtypeStruct((m, d), F32)] + [jax.ShapeDtypeStruct((m, d), BF16)] * emit_h,
        compiler_params=pltpu.CompilerParams(dimension_semantics=("parallel",), vmem_limit_bytes=VMEM_LIMIT_FFN),
        name="ffn_down",
    )(act, wd, x, g_post.reshape(1, d), g_next.reshape(1, d))
    return (out[0], out[1]) if emit_h else (out[0], None)


def kernel(x, pre_mix_g, w_in, ret_gn_g, diff_lam_q1, diff_lam_k1, diff_lam_q2, diff_lam_k2, diff_subln_g,
           sgu_ln_g, sgu_ln_b, sgu_w, sgu_b, w_out, post_mix_g, pre_ffn_g, w_gate, w_up, w_down, post_ffn_g):
    batch, seq, d = x.shape
    depth = w_in.shape[0]
    xf = x.reshape(batch * seq, d)
    h = _rmsnorm(xf, pre_mix_g[0])
    for l in range(depth):
        lambda_init = 0.8 - 0.6 * math.exp(-0.3 * l)
        proj, wd = _inproj(h, w_in, w_down, l)
        y_ret = _retention(proj, ret_gn_g[l], batch, seq)
        y_diff = _diffattn(proj, diff_lam_q1[l], diff_lam_k1[l], diff_lam_q2[l], diff_lam_k2[l],
                           diff_subln_g[l], lambda_init, batch, seq)
        y_sgu = _sgu(proj, sgu_ln_g[l], sgu_ln_b[l], sgu_w[l], sgu_b[l])
        xf, h = _outproj(y_ret, y_diff, y_sgu, w_out, l, xf, post_mix_g[l], pre_ffn_g[l])
        g_next = pre_mix_g[(l + 1) % depth]
        act = _ffn_up(h, w_gate, w_up, l)
        xf, h = _ffn_down(act, wd, xf, post_ffn_g[l], g_next, emit_h=l + 1 < depth)
    return xf.reshape(batch, seq, d)
```

```python
import functools
import math

import numpy as np
import jax
import jax.numpy as jnp
from jax import lax
from jax.experimental import pallas as pl
from jax.experimental.pallas import tpu as pltpu

D_MODEL = 2048
HEAD_DIM = 128
RET_HEADS = 6
DIFF_HEADS = 6
SGU_GROUPS = 4
DIFF_MAP_DIM = 64
RET_W = RET_HEADS * HEAD_DIM
DIFF_W = DIFF_HEADS * HEAD_DIM
SGU_W = SGU_GROUPS * HEAD_DIM
IN_W = 4 * RET_W + 3 * DIFF_W + 2 * SGU_W
CHUNK = 128
RET_CHUNK = 256
EPS = 1e-6
LOG2E = math.log2(math.e)
NEG_BIG = -1e30
SUM_ROWS = 16
FFN_TILES = 11
QCHUNK = 256

COL_RQ, COL_RK, COL_RV, COL_RG = 0, 6, 12, 18
COL_DQ, COL_DK, COL_DV = 24, 30, 36
COL_SU, COL_SV = 42, 46

V7X_VMEM_BYTES = 64 * 1024 * 1024
VMEM_LIMIT = 56 * 1024 * 1024
VMEM_LIMIT_FFN = 60 * 1024 * 1024
assert VMEM_LIMIT < VMEM_LIMIT_FFN < V7X_VMEM_BYTES

BF16 = jnp.bfloat16
F32 = jnp.float32


def _dot(a, b):
    return jnp.dot(a, b, preferred_element_type=F32)


def _dot_nt(a, b):
    return lax.dot_general(a, b, (((1,), (1,)), ((), ())), preferred_element_type=F32)


def _dot_tn(a, b):
    return lax.dot_general(a, b, (((0,), (0,)), ((), ())), preferred_element_type=F32)


def _rms(x, g):
    return x * lax.rsqrt(jnp.mean(x * x, axis=-1, keepdims=True) + EPS) * g


def _params(*sem):
    return pltpu.CompilerParams(dimension_semantics=sem, vmem_limit_bytes=VMEM_LIMIT)


def _rmsnorm_kernel(x_ref, g_ref, o_ref):
    o_ref[...] = _rms(x_ref[...], g_ref[...]).astype(o_ref.dtype)


def _rmsnorm(x, g, tm=512):
    m, d = x.shape
    return pl.pallas_call(
        _rmsnorm_kernel,
        grid=(m // tm,),
        in_specs=[pl.BlockSpec((tm, d), lambda i: (i, 0)), pl.BlockSpec((1, d), lambda i: (0, 0))],
        out_specs=pl.BlockSpec((tm, d), lambda i: (i, 0)),
        out_shape=jax.ShapeDtypeStruct((m, d), BF16),
        compiler_params=_params("parallel"),
        name="rmsnorm",
    )(x, g.reshape(1, d))


def _inproj_kernel(h_ref, w_ref, wd_ref, o_ref, wdb_ref, wb_ref):
    @pl.when(pl.program_id(1) == 0)
    def _():
        wb_ref[...] = w_ref[0].astype(BF16)

    @pl.when(pl.program_id(0) == 0)
    def _():
        wdb_ref[...] = wd_ref[0].astype(BF16)

    res = _dot(h_ref[...], wb_ref[...])
    for slot in range(o_ref.shape[0]):
        o_ref[slot] = res[:, slot * HEAD_DIM:(slot + 1) * HEAD_DIM].astype(o_ref.dtype)


def _inproj(h, w_stack, wd_stack, layer, tm=1024, tn=1280):
    m, k = h.shape
    n = w_stack.shape[2]
    nrow = m // tm
    hidden, d = wd_stack.shape[1:]
    slab = hidden // nrow
    assert slab * nrow == hidden and slab % 16 == 0, (hidden, nrow)
    slab_idx = lambda j, i: jnp.where(j == 0, i, nrow - 1)
    return pl.pallas_call(
        _inproj_kernel,
        grid=(n // tn, nrow),
        in_specs=[pl.BlockSpec((tm, k), lambda j, i: (i, 0)),
                  pl.BlockSpec((1, k, tn), lambda j, i: (layer, 0, j)),
                  pl.BlockSpec((1, slab, d), lambda j, i: (layer, slab_idx(j, i), 0))],
        out_specs=[pl.BlockSpec((tn // HEAD_DIM, tm, HEAD_DIM), lambda j, i: (j, i, 0)),
                   pl.BlockSpec((slab, d), lambda j, i: (slab_idx(j, i), 0))],
        out_shape=[jax.ShapeDtypeStruct((n // HEAD_DIM, m, HEAD_DIM), BF16),
                   jax.ShapeDtypeStruct((hidden, d), BF16)],
        scratch_shapes=[pltpu.VMEM((k, tn), BF16)],
        compiler_params=_params("arbitrary", "arbitrary"),
        name="inproj",
    )(h, w_stack, wd_stack)


def _retention_kernel(lg_ref, q_ref, k_ref, v_ref, g_ref, gn_ref, o_ref, r_ref, *, tb):
    pair = pl.program_id(1)
    r_ref[...] = jnp.zeros_like(r_ref)

    scale = HEAD_DIM ** -0.5
    row = lax.broadcasted_iota(jnp.int32, (RET_CHUNK, RET_CHUNK), 0)
    col = lax.broadcasted_iota(jnp.int32, (RET_CHUNK, RET_CHUNK), 1)
    rel = (row - col).astype(F32)
    pos = lax.broadcasted_iota(jnp.int32, (RET_CHUNK, 1), 0).astype(F32)
    consts = []
    for i in range(2):
        lg = lg_ref[2 * pair + i]
        decay = jnp.where(rel >= 0, jnp.exp(lg * jnp.maximum(rel, 0.0)), 0.0) * scale
        zeta = jnp.exp(lg * (RET_CHUNK - 1.0 - pos)) * scale
        xi = jnp.exp(lg * (pos + 1.0))
        chunk_decay = jnp.exp(jnp.full((1, 1), lg * RET_CHUNK, F32))
        consts.append((decay, zeta, xi, chunk_decay))

    def rows_block(t, carry):
        for c in range(tb // RET_CHUNK):
            sl = pl.ds(pl.multiple_of(t * tb + c * RET_CHUNK, RET_CHUNK), RET_CHUNK)
            for i in range(2):
                decay, zeta, xi, chunk_decay = consts[i]
                hs = slice(i * HEAD_DIM, (i + 1) * HEAD_DIM)
                q = q_ref[i, sl, :]
                k = k_ref[i, sl, :]
                v = v_ref[i, sl, :]
                scores = _dot_nt(q, k) * decay
                inner = _dot(scores.astype(BF16), v)
                r_prev = r_ref[i]
                cross = _dot(q, r_prev.astype(BF16)) * xi
                kz = (k.astype(F32) * zeta).astype(BF16)
                r_ref[i] = _dot_tn(kz, v) + chunk_decay * r_prev
                y = inner + cross
                mu = jnp.mean(y, axis=-1, keepdims=True)
                yc = y - mu
                var = jnp.mean(yc * yc, axis=-1, keepdims=True)
                y = yc * lax.rsqrt(var + EPS) * gn_ref[0, :, hs]
                gate = g_ref[i, sl, :].astype(F32)
                o_ref[sl, hs] = (gate * jax.nn.sigmoid(gate) * y).astype(o_ref.dtype)
        return carry

    lax.fori_loop(0, q_ref.shape[1] // tb, rows_block, 0)


def _retention(proj, gn_gain, batch, seq, tb=2048):
    m = proj.shape[1]
    assert seq % tb == 0, (seq, tb)
    log_gamma = np.log1p(-(2.0 ** (-5.0 - np.arange(RET_HEADS, dtype=np.float32)))).astype(np.float32)
    wide = 2 * HEAD_DIM

    def blk(col):
        return pl.BlockSpec((2, seq, HEAD_DIM), lambda b, h: (col // 2 + h, b, 0))

    return pl.pallas_call(
        functools.partial(_retention_kernel, tb=tb),
        grid=(batch, RET_HEADS // 2),
        in_specs=[pl.BlockSpec(memory_space=pltpu.SMEM),
                  blk(COL_RQ), blk(COL_RK), blk(COL_RV), blk(COL_RG),
                  pl.BlockSpec((1, 1, wide), lambda b, h: (h, 0, 0))],
        out_specs=pl.BlockSpec((seq, wide), lambda b, h: (b, h)),
        out_shape=jax.ShapeDtypeStruct((m, RET_W), BF16),
        scratch_shapes=[pltpu.VMEM((2, HEAD_DIM, HEAD_DIM), F32)],
        compiler_params=_params("arbitrary", "arbitrary"),
        name="retention",
    )(jnp.asarray(log_gamma), proj, proj, proj, proj, gn_gain.reshape(RET_HEADS // 2, 1, wide))


def _diffattn_kernel(slope_ref, q_ref, k_ref, v_ref, lq1_ref, lk1_ref, lq2_ref, lk2_ref, sg_ref, o_ref,
                     k1_ref, k2_ref, vt_ref, *bufs, bq, bk, lambda_init):
    acc = bufs[0:2]
    s_buf = [bufs[2:4], bufs[4:6]]
    p_buf = [bufs[6:8], bufs[8:10]]
    h = pl.program_id(1)
    qi = pl.program_id(2)
    nkb = k1_ref.shape[0]
    per_q = bq // bk

    slope2 = slope_ref[h] * LOG2E

    def block_bias(j):
        return jnp.full((1, 1), j * bk, jnp.int32).astype(F32) * slope2

    @pl.when(qi == 0)
    def _():
        lane = lax.broadcasted_iota(jnp.int32, (bk, HEAD_DIM), 1)
        bias = lax.broadcasted_iota(jnp.int32, (bk, HEAD_DIM), 0).astype(F32) * slope2
        hi = bias.astype(BF16).astype(F32)
        r1 = bias - hi
        mid = r1.astype(BF16).astype(F32)
        lo = r1 - mid
        zero = jnp.zeros_like(bias)
        e1 = jnp.where(lane == 64, hi, jnp.where(lane == 65, mid, jnp.where(lane == 66, lo, zero)))
        e2 = jnp.where(lane == 0, hi, jnp.where(lane == 1, mid, jnp.where(lane == 2, lo, zero)))

        def prep(j, carry):
            start = pl.multiple_of(j * bk, bk)
            kb = k_ref[0, pl.ds(start, bk), :].astype(F32)
            k1_ref[j] = jnp.where(lane < 64, kb, e1).astype(BF16)
            k2_ref[j] = jnp.where(lane >= 64, kb, e2).astype(BF16)
            vt_ref[j, 0:HEAD_DIM, :] = v_ref[0, pl.ds(start, bk), :].astype(F32).T.astype(BF16)
            vt_ref[j, HEAD_DIM:, :] = jnp.ones((SUM_ROWS, bk), BF16)
            return carry

        lax.fori_loop(0, nkb, prep, 0)

    lane = lax.broadcasted_iota(jnp.int32, (bq, HEAD_DIM), 1)
    qs = q_ref[0].astype(F32) * (DIFF_MAP_DIM ** -0.5 * LOG2E)
    one = jnp.ones_like(qs)
    zero = jnp.zeros_like(qs)
    qa = (jnp.where(lane < 64, qs, jnp.where(lane < 67, one, zero)).astype(BF16),
          jnp.where(lane >= 64, qs, jnp.where(lane < 3, one, zero)).astype(BF16))
    ka = (k1_ref, k2_ref)
    chunks = [slice(c * QCHUNK, (c + 1) * QCHUNK) for c in range(bq // QCHUNK)]
    krow = lax.broadcasted_iota(jnp.int32, (bk, QCHUNK), 0)
    qcol = lax.broadcasted_iota(jnp.int32, (bk, QCHUNK), 1)

    def issue_scores(mp, j, s_out, diag=None):
        mbs = []
        for cs in chunks:
            first_key = 0 if diag is None else diag * bk
            if first_key >= cs.stop:
                mbs.append(jnp.full((1, QCHUNK), 2 * NEG_BIG, F32))
                continue
            s = _dot_nt(ka[mp][j], qa[mp][cs, :])
            if diag is not None and first_key + bk - 1 > cs.start:
                s = jnp.where(krow + first_key <= qcol + cs.start, s, 2 * NEG_BIG)
            s_out[mp][:, cs] = s
            mbs.append(jnp.max(s, axis=0, keepdims=True))
        return jnp.concatenate(mbs, axis=1) + block_bias(j)

    def live(cs, diag):
        return diag is None or diag * bk < cs.stop

    def stage(slot, carry, cur, nxt, pending, cur_diag=None, nxt_diag=None, pend_diag=None, first=False):
        s_in, p_in = s_buf[slot], p_buf[slot]
        s_out, p_out = s_buf[1 - slot], p_buf[1 - slot]
        ms, alphas, mbs, dep = carry[0:6:3], carry[1:6:3], carry[2:6:3], carry[6]
        mns = [jnp.maximum(ms[mp], mbs[mp]) for mp in range(2)]
        shift = dep - block_bias(cur)
        new_mb = []
        for mp in range(2):
            new_mb.append(issue_scores(mp, nxt, s_out, nxt_diag))
            for cs in chunks:
                if live(cs, cur_diag):
                    p = jnp.exp2(s_in[mp][:, cs] - (mns[mp][:, cs] + shift))
                    p_out[mp][:, cs] = p.astype(BF16)
                else:
                    p_out[mp][:, cs] = jnp.zeros((bk, QCHUNK), BF16)
        if pending is not None:
            for mp in range(2):
                for cs in chunks:
                    if live(cs, pend_diag):
                        pv = _dot(vt_ref[pending], p_in[mp][:, cs])
                        acc[mp][:, cs] = pv if first else alphas[mp][:, cs] * acc[mp][:, cs] + pv
        out = []
        for mp in range(2):
            out += [mns[mp], jnp.exp2(ms[mp] - mns[mp]), new_mb[mp]]
        return tuple(out) + (p[0:1, :] * 0.0,)

    d0 = per_q * qi
    nfull = per_q * qi
    carry = []
    for mp in range(2):
        carry += [jnp.full((1, bq), NEG_BIG, F32), jnp.ones((1, bq), F32), issue_scores(mp, d0, s_buf[0], diag=0)]
    carry = tuple(carry) + (jnp.zeros((1, QCHUNK), F32),)
    for d in range(per_q):
        more = d + 1 < per_q
        carry = stage(d % 2, carry, d0 + d, d0 + d + 1 if more else 0, d0 + d - 1 if d else None,
                      cur_diag=d, nxt_diag=d + 1 if more else None, pend_diag=d - 1 if d else None, first=d == 1)

    def pair(u, carry):
        t = 2 * u
        carry = stage(0, carry, t, t + 1, jnp.where(t == 0, d0 + per_q - 1, t - 1))
        return stage(1, carry, t + 1, jnp.minimum(t + 2, nfull - 1), t)

    carry = lax.fori_loop(0, nfull // 2, pair, carry)
    last = jnp.where(qi == 0, per_q - 1, nfull - 1)
    for mp in range(2):
        for cs in chunks:
            acc[mp][:, cs] = carry[3 * mp + 1][:, cs] * acc[mp][:, cs] + _dot(vt_ref[last], p_buf[0][mp][:, cs])

    lam = (jnp.exp(jnp.sum(lq1_ref[...] * lk1_ref[...], axis=-1, keepdims=True))
           - jnp.exp(jnp.sum(lq2_ref[...] * lk2_ref[...], axis=-1, keepdims=True)) + lambda_init)
    l1 = acc[0][HEAD_DIM:HEAD_DIM + 1, :]
    l2 = acc[1][HEAD_DIM:HEAD_DIM + 1, :]
    o_t = acc[0][0:HEAD_DIM, :] * (1.0 / l1) - acc[1][0:HEAD_DIM, :] * (lam / l2)
    o = o_t.T
    o = _rms(o, sg_ref[0]) * (1.0 - lambda_init)
    o_ref[...] = o.astype(o_ref.dtype)


def _diffattn(proj, lq1, lk1, lq2, lk2, subln_g, lambda_init, batch, seq, bq=2048, bk=512):
    m = proj.shape[1]
    nq, nk = seq // bq, seq // bk
    assert nq * bq == seq and nk * bk == seq and (bq // bk) % 2 == 0 and bq % QCHUNK == 0, (seq, bq, bk)
    slopes = (2.0 ** (-8.0 * np.arange(1, DIFF_HEADS + 1, dtype=np.float32) / DIFF_HEADS)).astype(np.float32)
    vec = pl.BlockSpec((1, DIFF_MAP_DIM), lambda b, h, i: (0, 0))
    return pl.pallas_call(
        functools.partial(_diffattn_kernel, bq=bq, bk=bk, lambda_init=lambda_init),
        grid=(batch, DIFF_HEADS, nq),
        in_specs=[pl.BlockSpec(memory_space=pltpu.SMEM),
                  pl.BlockSpec((1, bq, HEAD_DIM), lambda b, h, i: (COL_DQ + h, b * nq + i, 0)),
                  pl.BlockSpec((1, seq, HEAD_DIM), lambda b, h, i: (COL_DK + h, b, 0)),
                  pl.BlockSpec((1, seq, HEAD_DIM), lambda b, h, i: (COL_DV + h, b, 0)),
                  vec, vec, vec, vec,
                  pl.BlockSpec((1, 1, HEAD_DIM), lambda b, h, i: (h, 0, 0))],
        out_specs=pl.BlockSpec((bq, HEAD_DIM), lambda b, h, i: (b * nq + i, h)),
        out_shape=jax.ShapeDtypeStruct((m, DIFF_W), BF16),
        scratch_shapes=[pltpu.VMEM((nk, bk, HEAD_DIM), BF16),
                        pltpu.VMEM((nk, bk, HEAD_DIM), BF16),
                        pltpu.VMEM((nk, HEAD_DIM + SUM_ROWS, bk), BF16),
                        *[pltpu.VMEM((HEAD_DIM + SUM_ROWS, bq), F32)] * 2,
                        *[pltpu.VMEM((bk, bq), F32)] * 4,
                        *[pltpu.VMEM((bk, bq), BF16)] * 4],
        compiler_params=_params("arbitrary", "arbitrary", "arbitrary"),
        name="diffattn",
    )(jnp.asarray(slopes), proj, proj, proj,
      lq1.reshape(1, -1), lk1.reshape(1, -1), lq2.reshape(1, -1), lk2.reshape(1, -1),
      subln_g.reshape(DIFF_HEADS, 1, HEAD_DIM))


def _sgu_kernel(u_ref, v_ref, lng_ref, lnb_ref, w_ref, b_ref, o_ref, *, tb):
    row = lax.broadcasted_iota(jnp.int32, (CHUNK, CHUNK), 0)
    col = lax.broadcasted_iota(jnp.int32, (CHUNK, CHUNK), 1)
    for i in range(2):
        gs = slice(i * HEAD_DIM, (i + 1) * HEAD_DIM)
        w = jnp.where(row >= col, w_ref[i], 0.0).astype(BF16)
        bias = b_ref[i]
        lng = lng_ref[0, :, gs]
        lnb = lnb_ref[0, :, gs]
        for c in range(tb // CHUNK):
            sl = slice(c * CHUNK, (c + 1) * CHUNK)
            u = jax.nn.gelu(u_ref[i, sl, :].astype(F32))
            v = jax.nn.gelu(v_ref[i, sl, :].astype(F32))
            mu = jnp.mean(v, axis=-1, keepdims=True)
            vc = v - mu
            var = jnp.mean(vc * vc, axis=-1, keepdims=True)
            vn = vc * lax.rsqrt(var + EPS) * lng + lnb
            mixed = _dot(w, vn.astype(BF16)) + bias
            o_ref[sl, gs] = (u * mixed).astype(o_ref.dtype)


def _sgu(proj, ln_g, ln_b, w_s, b_s, tb=2048):
    m = proj.shape[1]
    pairs = SGU_GROUPS // 2
    wide = 2 * HEAD_DIM
    per_pair = lambda i, j: (j, 0, 0)
    return pl.pallas_call(
        functools.partial(_sgu_kernel, tb=tb),
        grid=(m // tb, pairs),
        in_specs=[pl.BlockSpec((2, tb, HEAD_DIM), lambda i, j: (COL_SU // 2 + j, i, 0)),
                  pl.BlockSpec((2, tb, HEAD_DIM), lambda i, j: (COL_SV // 2 + j, i, 0)),
                  pl.BlockSpec((1, 1, wide), per_pair),
                  pl.BlockSpec((1, 1, wide), per_pair),
                  pl.BlockSpec((2, CHUNK, CHUNK), per_pair),
                  pl.BlockSpec((2, CHUNK, 1), per_pair)],
        out_specs=pl.BlockSpec((tb, wide), lambda i, j: (i, j)),
        out_shape=jax.ShapeDtypeStruct((m, SGU_W), BF16),
        compiler_params=_params("parallel", "arbitrary"),
        name="sgu",
    )(proj, proj, ln_g.reshape(pairs, 1, wide), ln_b.reshape(pairs, 1, wide), w_s,
      b_s.reshape(SGU_GROUPS, CHUNK, 1))


def _outproj_kernel(yr_ref, yd_ref, ys_ref, wf_ref, x_ref, gpost_ref, gnext_ref, xo_ref, ho_ref, w_ref):
    @pl.when(pl.program_id(0) == 0)
    def _():
        w_ref[...] = wf_ref[0].astype(BF16)

    mix = (_dot(yr_ref[...], w_ref[0:RET_W, :])
           + _dot(yd_ref[...], w_ref[RET_W:RET_W + DIFF_W, :])
           + _dot(ys_ref[...], w_ref[RET_W + DIFF_W:D_MODEL, :]))
    xn = x_ref[...] + _rms(mix, gpost_ref[...])
    xo_ref[...] = xn
    ho_ref[...] = _rms(xn, gnext_ref[...]).astype(ho_ref.dtype)


def _outproj(y_ret, y_diff, y_sgu, w_stack, layer, x, g_post, g_next, tm=512):
    m, d = x.shape
    row = lambda i: (i, 0)
    const = lambda i: (0, 0)
    return pl.pallas_call(
        _outproj_kernel,
        grid=(m // tm,),
        in_specs=[pl.BlockSpec((tm, RET_W), row), pl.BlockSpec((tm, DIFF_W), row), pl.BlockSpec((tm, SGU_W), row),
                  pl.BlockSpec((1, d, d), lambda i: (layer, 0, 0), pipeline_mode=pl.Buffered(1)),
                  pl.BlockSpec((tm, d), row),
                  pl.BlockSpec((1, d), const), pl.BlockSpec((1, d), const)],
        out_specs=[pl.BlockSpec((tm, d), row), pl.BlockSpec((tm, d), row)],
        out_shape=[jax.ShapeDtypeStruct((m, d), F32), jax.ShapeDtypeStruct((m, d), BF16)],
        scratch_shapes=[pltpu.VMEM((d, d), BF16)],
        compiler_params=_params("arbitrary"),
        name="outproj",
    )(y_ret, y_diff, y_sgu, w_stack, x, g_post.reshape(1, d), g_next.reshape(1, d))


def _ffn_up_kernel(h_ref, wg_ref, wu_ref, a_ref, w_ref):
    th = wg_ref.shape[2]

    @pl.when(pl.program_id(1) == 0)
    def _():
        w_ref[:, 0:th] = wg_ref[0].astype(BF16)
        w_ref[:, th:] = wu_ref[0].astype(BF16)

    gu = _dot(h_ref[...], w_ref[...])
    gate = gu[:, 0:th]
    a_ref[...] = (gate * jax.nn.sigmoid(gate) * gu[:, th:]).astype(a_ref.dtype)


def _ffn_up(h, wg_stack, wu_stack, layer, tm=2048):
    m, d = h.shape
    hidden = wg_stack.shape[2]
    th = hidden // FFN_TILES
    w_tile = pl.BlockSpec((1, d, th), lambda j, i: (layer, 0, j))
    return pl.pallas_call(
        _ffn_up_kernel,
        grid=(FFN_TILES, m // tm),
        in_specs=[pl.BlockSpec((tm, d), lambda j, i: (i, 0)), w_tile, w_tile],
        out_specs=pl.BlockSpec((tm, th), lambda j, i: (i, j)),
        out_shape=jax.ShapeDtypeStruct((m, hidden), BF16),
        scratch_shapes=[pltpu.VMEM((d, 2 * th), BF16)],
        compiler_params=_params("arbitrary", "arbitrary"),
        name="ffn_up",
    )(h, wg_stack, wu_stack)


def _ffn_down_kernel(a_ref, w_ref, x_ref, gpost_ref, gnext_ref, xo_ref, *maybe_ho_ref):
    f = _dot(a_ref[...], w_ref[...])
    xn = x_ref[...] + _rms(f, gpost_ref[...])
    xo_ref[...] = xn
    for ho_ref in maybe_ho_ref:
        ho_ref[...] = _rms(xn, gnext_ref[...]).astype(ho_ref.dtype)


def _ffn_down(act, wd, x, g_post, g_next, emit_h, tm=512):
    m, d = x.shape
    hidden = act.shape[1]
    row = lambda i: (i, 0)
    const = lambda i: (0, 0)
    out = pl.pallas_call(
        _ffn_down_kernel,
        grid=(m // tm,),
        in_specs=[pl.BlockSpec((tm, hidden), row),
                  pl.BlockSpec((hidden, d), const, pipeline_mode=pl.Buffered(1)),
                  pl.BlockSpec((tm, d), row),
                  pl.BlockSpec((1, d), const), pl.BlockSpec((1, d), const)],
        out_specs=[pl.BlockSpec((tm, d), row)] + [pl.BlockSpec((tm, d), row)] * emit_h,
        out_shape=[jax.ShapeDtypeStruct((m, d), F32)] + [jax.ShapeDtypeStruct((m, d), BF16)] * emit_h,
        compiler_params=pltpu.CompilerParams(dimension_semantics=("parallel",), vmem_limit_bytes=VMEM_LIMIT_FFN),
        name="ffn_down",
    )(act, wd, x, g_post.reshape(1, d), g_next.reshape(1, d))
    return (out[0], out[1]) if emit_h else (out[0], None)


def kernel(x, pre_mix_g, w_in, ret_gn_g, diff_lam_q1, diff_lam_k1, diff_lam_q2, diff_lam_k2, diff_subln_g,
           sgu_ln_g, sgu_ln_b, sgu_w, sgu_b, w_out, post_mix_g, pre_ffn_g, w_gate, w_up, w_down, post_ffn_g):
    batch, seq, d = x.shape
    depth = w_in.shape[0]
    xf = x.reshape(batch * seq, d)
    h = _rmsnorm(xf, pre_mix_g[0])
    for l in range(depth):
        lambda_init = 0.8 - 0.6 * math.exp(-0.3 * l)
        proj, wd = _inproj(h, w_in, w_down, l)
        y_ret = _retention(proj, ret_gn_g[l], batch, seq)
        y_diff = _diffattn(proj, diff_lam_q1[l], diff_lam_k1[l], diff_lam_q2[l], diff_lam_k2[l],
                           diff_subln_g[l], lambda_init, batch, seq)
        y_sgu = _sgu(proj, sgu_ln_g[l], sgu_ln_b[l], sgu_w[l], sgu_b[l])
        xf, h = _outproj(y_ret, y_diff, y_sgu, w_out, l, xf, post_mix_g[l], pre_ffn_g[l])
        g_next = pre_mix_g[(l + 1) % depth]
        act = _ffn_up(h, w_gate, w_up, l)
        xf, h = _ffn_down(act, wd, xf, post_ffn_g[l], g_next, emit_h=l + 1 < depth)
    return xf.reshape(batch, seq, d)
```
